```python
import jax, jax.numpy as jnp
from jax import lax
import numpy as np

D_MODEL = 1024
BATCH = 4
SEQ = 8192
DEPTH = 1
DEC_BATCH = 8
DEC_SEQ = 8192
PAST_LEN = 128

HEAD_DIM = 64
A_HEADS = 12
A_WIDTH = A_HEADS * HEAD_DIM
DILATED_BRANCHES = ((128, 1), (512, 4), (2048, 16))
ROT_DIMS_A = HEAD_DIM // 4
B_HEADS = 4
QK_NOPE = 64
QK_ROPE = 32
QK_HEAD = QK_NOPE + QK_ROPE
V_HEAD = 64
Q_LORA = 256
KV_LORA = 256
B_WIDTH = B_HEADS * V_HEAD
MIX_WIDTH = A_WIDTH + B_WIDTH
IN_COLS = 3 * A_WIDTH + Q_LORA + KV_LORA + QK_ROPE
ROPE_THETA = 500000.0
Q_BLOCK = 128
N_EXPERTS = 64
TOP_K = 6
N_GROUPS = 8
TOPK_GROUPS = 4
D_EXPERT = 256
D_SHARED = 256
ROUTED_SCALE = 2.5
EXPERT_BLOCK = 256
EPS = 1e-6

kernel_name = "hybrid_dilated_mla_moe_encoder"


def rms_norm(x, g):
    xf = x.astype(jnp.float32)
    y = xf * lax.rsqrt(jnp.mean(xf * xf, axis=-1, keepdims=True) + EPS)
    return (y * g.astype(jnp.float32)).astype(x.dtype)


def rope(x, pos):
    r = x.shape[-1]
    half = r // 2
    inv = ROPE_THETA ** (-jnp.arange(half, dtype=jnp.float32) * 2.0 / r)
    ang = pos.astype(jnp.float32)[:, None] * inv[None, :]
    cos = jnp.cos(ang)[:, None, :]
    sin = jnp.sin(ang)[:, None, :]
    xf = x.astype(jnp.float32)
    x1, x2 = xf[..., :half], xf[..., half:]
    return jnp.concatenate([x1 * cos - x2 * sin, x1 * sin + x2 * cos], axis=-1).astype(x.dtype)


def partial_rope(x, pos):
    return jnp.concatenate([rope(x[..., :ROT_DIMS_A], pos), x[..., ROT_DIMS_A:]], axis=-1)


def dilated_branch(q, k, v, dil, radius):
    B, S, H, Dh = q.shape
    L = S // dil
    def by_residue(t):
        return t.reshape(B, L, dil, H, Dh).transpose(0, 2, 1, 3, 4).reshape(B * dil, L, H, Dh)
    qs, ks, vs = by_residue(q), by_residue(k), by_residue(v)
    W = radius
    nb = -(-L // W)
    Lp = nb * W
    G = B * dil
    qb = jnp.pad(qs, ((0, 0), (0, Lp - L), (0, 0), (0, 0))).reshape(G, nb, W, H, Dh)
    def windows(t):
        tp = jnp.pad(t, ((0, 0), (W, Lp - L + W), (0, 0), (0, 0))).reshape(G, nb + 2, W, H, Dh)
        return jnp.concatenate([tp[:, :-2], tp[:, 1:-1], tp[:, 2:]], axis=2)
    kw, vw = windows(ks), windows(vs)
    qpos = jnp.arange(Lp).reshape(nb, W)
    kpos = (jnp.arange(nb)[:, None] - 1) * W + jnp.arange(3 * W)[None, :]
    valid = ((jnp.abs(qpos[:, :, None] - kpos[:, None, :]) <= radius)
             & (kpos[:, None, :] >= 0) & (kpos[:, None, :] < L))
    s = jnp.einsum('gnqhd,gnkhd->gnhqk', qb, kw).astype(jnp.float32)
    s = jnp.where(valid[None, :, None], s, -jnp.inf)
    m = jnp.max(s, axis=-1, keepdims=True)
    p = jnp.exp(s - m)
    den = jnp.sum(p, axis=-1)
    o = jnp.einsum('gnhqk,gnkhd->gnqhd', p.astype(vw.dtype), vw).astype(jnp.float32)
    o = o / jnp.swapaxes(den, 2, 3)[..., None]
    lse = jnp.swapaxes(m[..., 0] + jnp.log(den), 2, 3)
    o = o.reshape(G, Lp, H, Dh)[:, :L]
    lse = lse.reshape(G, Lp, H)[:, :L]
    o = o.reshape(B, dil, L, H, Dh).transpose(0, 2, 1, 3, 4).reshape(B, S, H, Dh)
    lse = lse.reshape(B, dil, L, H).transpose(0, 2, 1, 3).reshape(B, S, H)
    return o, lse


def dilated_attention(q, k, v):
    outs, lses = [], []
    for window, dil in DILATED_BRANCHES:
        o, lse = dilated_branch(q, k, v, dil, window // (2 * dil))
        outs.append(o)
        lses.append(lse)
    wts = jax.nn.softmax(jnp.stack(lses, axis=0), axis=0)
    o = jnp.sum(wts[..., None] * jnp.stack(outs, axis=0), axis=0)
    return o.astype(q.dtype)


def dense_attention(q, k, v):
    B, S, H, Dq = q.shape
    nb = S // Q_BLOCK
    qb = q.reshape(B, nb, Q_BLOCK, H, Dq).transpose(1, 0, 2, 3, 4)
    def one(qblk):
        s = jnp.einsum('bqhd,bkhd->bhqk', qblk, k).astype(jnp.float32)
        p = jax.nn.softmax(s, axis=-1).astype(v.dtype)
        return jnp.einsum('bhqk,bkhd->bqhd', p, v)
    o = lax.map(one, qb)
    return o.transpose(1, 0, 2, 3, 4).reshape(B, S, H, v.shape[-1])


def moe(h, w_router, b_router, w_e_gate, w_e_up, w_e_down, w_s_gate, w_s_up, w_s_down):
    N, D = h.shape
    scores = jax.nn.sigmoid((h @ w_router).astype(jnp.float32))
    biased = scores + b_router.astype(jnp.float32)
    per_group = N_EXPERTS // N_GROUPS
    grp_score = jnp.sum(lax.top_k(biased.reshape(N, N_GROUPS, per_group), 2)[0], axis=-1)
    _, grp_idx = lax.top_k(grp_score, TOPK_GROUPS)
    grp_mask = jnp.sum(jax.nn.one_hot(grp_idx, N_GROUPS, dtype=jnp.float32), axis=1)
    expert_ok = jnp.repeat(grp_mask, per_group, axis=1) > 0
    _, idx = lax.top_k(jnp.where(expert_ok, biased, -jnp.inf), TOP_K)
    wts = jnp.take_along_axis(scores, idx, axis=1)
    wts = wts / jnp.sum(wts, axis=-1, keepdims=True) * ROUTED_SCALE
    NK = N * TOP_K
    e_flat = idx.reshape(NK)
    t_flat = jnp.repeat(jnp.arange(N), TOP_K)
    g_flat = wts.reshape(NK).astype(h.dtype)
    order = jnp.argsort(e_flat)
    e_sorted = e_flat[order]
    counts = jnp.bincount(e_flat, length=N_EXPERTS)
    padded = (counts + EXPERT_BLOCK - 1) // EXPERT_BLOCK * EXPERT_BLOCK
    start = jnp.cumsum(counts) - counts
    pend = jnp.cumsum(padded)
    pstart = pend - padded
    dest = pstart[e_sorted] + (jnp.arange(NK) - start[e_sorted])
    n_blocks = (NK + N_EXPERTS * (EXPERT_BLOCK - 1)) // EXPERT_BLOCK
    P = n_blocks * EXPERT_BLOCK
    slot_tok = jnp.full((P,), N, dtype=jnp.int32).at[dest].set(t_flat[order].astype(jnp.int32))
    slot_gate = jnp.zeros((P,), h.dtype).at[dest].set(g_flat[order])
    block_e = jnp.clip(jnp.searchsorted(pend, jnp.arange(n_blocks) * EXPERT_BLOCK, side='right'), 0, N_EXPERTS - 1)
    hp = jnp.concatenate([h, jnp.zeros((1, D), h.dtype)], axis=0)
    def body(y, blk):
        tok, gate, e = blk
        xb = hp[tok]
        a = xb @ w_e_gate[e]
        u = xb @ w_e_up[e]
        o = (jax.nn.silu(a) * u) @ w_e_down[e]
        return y.at[tok].add(o * gate[:, None]), None
    routed, _ = lax.scan(body, jnp.zeros((N + 1, D), h.dtype),
                         (slot_tok.reshape(n_blocks, EXPERT_BLOCK), slot_gate.reshape(n_blocks, EXPERT_BLOCK), block_e))
    shared = (jax.nn.silu(h @ w_s_gate) * (h @ w_s_up)) @ w_s_down
    return shared + routed[:N]


def encoder_layer(x, c, w_ada, b_ada, g_norm1, w_in, g_qa, g_ka, g_qlat, w_qb, g_kvlat, w_kvb, g_qb, g_kb,
                  w_o, g_norm2, w_router, b_router, w_e_gate, w_e_up, w_e_down, w_s_gate, w_s_up, w_s_down):
    Bt, S, D = x.shape
    mod = (jax.nn.silu(c) @ w_ada + b_ada)[:, None, :]
    sh1, sc1, gt1, sh2, sc2, gt2 = jnp.split(mod, 6, axis=-1)
    pos = jnp.arange(S)
    h = rms_norm(x, g_norm1) * (1 + sc1) + sh1
    z = h @ w_in
    cuts = [A_WIDTH, 2 * A_WIDTH, 3 * A_WIDTH, 3 * A_WIDTH + Q_LORA, 3 * A_WIDTH + Q_LORA + KV_LORA]
    qa, ka, va, q_lat, kv_lat, k_rope = jnp.split(z, cuts, axis=-1)
    qa = rms_norm(qa.reshape(Bt, S, A_HEADS, HEAD_DIM), g_qa)
    ka = rms_norm(ka.reshape(Bt, S, A_HEADS, HEAD_DIM), g_ka)
    va = va.reshape(Bt, S, A_HEADS, HEAD_DIM)
    qa = partial_rope(qa, pos) * (HEAD_DIM ** -0.5)
    ka = partial_rope(ka, pos)
    oa = dilated_attention(qa, ka, va)
    qb = (rms_norm(q_lat, g_qlat) @ w_qb).reshape(Bt, S, B_HEADS, QK_HEAD)
    kv = (rms_norm(kv_lat, g_kvlat) @ w_kvb).reshape(Bt, S, B_HEADS, QK_NOPE + V_HEAD)
    k_nope, vb = kv[..., :QK_NOPE], kv[..., QK_NOPE:]
    k_rope = jnp.broadcast_to(k_rope[:, :, None, :], (Bt, S, B_HEADS, QK_ROPE))
    kb = jnp.concatenate([k_nope, k_rope], axis=-1)
    qb = rms_norm(qb, g_qb)
    kb = rms_norm(kb, g_kb)
    qb = jnp.concatenate([qb[..., :QK_NOPE], rope(qb[..., QK_NOPE:], pos)], axis=-1) * (QK_HEAD ** -0.5)
    kb = jnp.concatenate([kb[..., :QK_NOPE], rope(kb[..., QK_NOPE:], pos)], axis=-1)
    ob = dense_attention(qb, kb, vb)
    mixed = jnp.concatenate([oa.reshape(Bt, S, A_WIDTH), ob.reshape(Bt, S, B_WIDTH)], axis=-1) @ w_o
    x = x + gt1 * mixed
    h = rms_norm(x, g_norm2) * (1 + sc2) + sh2
    y = moe(h.reshape(Bt * S, D), w_router, b_router, w_e_gate, w_e_up, w_e_down, w_s_gate, w_s_up, w_s_down)
    return x + gt2 * y.reshape(Bt, S, D)


def setup_inputs(seed: int = 0) -> dict:
    key = jax.random.key(seed)
    ks = jax.random.split(key, 32)
    f32 = jnp.float32
    def nrm(k, shape, scale):
        return jax.random.normal(k, shape, f32) * scale
    def gain(k, n):
        return 1.0 + 0.02 * jax.random.normal(k, (DEPTH, n), f32)
    D = D_MODEL
    return {
        "x_prompt": nrm(ks[0], (BATCH, SEQ, D), 1.0),
        "x_sample": nrm(ks[1], (DEC_BATCH, DEC_SEQ, D), 1.0),
        "c_prompt": nrm(ks[2], (BATCH, D), 1.0),
        "c_sample": nrm(ks[3], (DEC_BATCH, D), 1.0),
        "w_ada": nrm(ks[4], (DEPTH, D, 6 * D), 0.5 * D ** -0.5),
        "b_ada": nrm(ks[5], (DEPTH, 6 * D), 0.02),
        "g_norm1": gain(ks[6], D),
        "w_in": nrm(ks[7], (DEPTH, D, IN_COLS), D ** -0.5),
        "g_qa": gain(ks[8], HEAD_DIM),
        "g_ka": gain(ks[9], HEAD_DIM),
        "g_qlat": gain(ks[10], Q_LORA),
        "w_qb": nrm(ks[11], (DEPTH, Q_LORA, B_HEADS * QK_HEAD), Q_LORA ** -0.5),
        "g_kvlat": gain(ks[12], KV_LORA),
        "w_kvb": nrm(ks[13], (DEPTH, KV_LORA, B_HEADS * (QK_NOPE + V_HEAD)), KV_LORA ** -0.5),
        "g_qb": gain(ks[14], QK_HEAD),
        "g_kb": gain(ks[15], QK_HEAD),
        "w_o": nrm(ks[16], (DEPTH, MIX_WIDTH, D), MIX_WIDTH ** -0.5),
        "g_norm2": gain(ks[17], D),
        "w_router": nrm(ks[18], (DEPTH, D, N_EXPERTS), D ** -0.5),
        "b_router": nrm(ks[19], (DEPTH, N_EXPERTS), 0.01),
        "w_e_gate": nrm(ks[20], (DEPTH, N_EXPERTS, D, D_EXPERT), D ** -0.5),
        "w_e_up": nrm(ks[21], (DEPTH, N_EXPERTS, D, D_EXPERT), D ** -0.5),
        "w_e_down": nrm(ks[22], (DEPTH, N_EXPERTS, D_EXPERT, D), D_EXPERT ** -0.5),
        "w_s_gate": nrm(ks[23], (DEPTH, D, D_SHARED), D ** -0.5),
        "w_s_up": nrm(ks[24], (DEPTH, D, D_SHARED), D ** -0.5),
        "w_s_down": nrm(ks[25], (DEPTH, D_SHARED, D), D_SHARED ** -0.5),
    }


def reference(x_prompt, x_sample, c_prompt, c_sample, w_ada, b_ada, g_norm1, w_in, g_qa, g_ka, g_qlat, w_qb,
              g_kvlat, w_kvb, g_qb, g_kb, w_o, g_norm2, w_router, b_router, w_e_gate, w_e_up, w_e_down,
              w_s_gate, w_s_up, w_s_down):
    y_prompt = x_prompt
    y_sample = x_sample
    for l in range(DEPTH):
        p = (w_ada[l], b_ada[l], g_norm1[l], w_in[l], g_qa[l], g_ka[l], g_qlat[l], w_qb[l], g_kvlat[l], w_kvb[l],
             g_qb[l], g_kb[l], w_o[l], g_norm2[l], w_router[l], b_router[l], w_e_gate[l], w_e_up[l], w_e_down[l],
             w_s_gate[l], w_s_up[l], w_s_down[l])
        y_prompt = encoder_layer(y_prompt, c_prompt, *p)
        y_sample = encoder_layer(y_sample, c_sample, *p)
    return (y_prompt, y_sample)
```

```python
import functools

import jax
import jax.numpy as jnp
from jax import lax
from jax.experimental import pallas as pl
from jax.experimental.pallas import tpu as pltpu

F32 = jnp.float32
BF16 = jnp.bfloat16
I32 = jnp.int32

HEAD_DIM = 64
A_HEADS = 12
A_WIDTH = A_HEADS * HEAD_DIM
DILATED_BRANCHES = ((128, 1), (512, 4), (2048, 16))
ROT_DIMS_A = HEAD_DIM // 4
B_HEADS = 4
QK_NOPE = 64
QK_ROPE = 32
QK_HEAD = QK_NOPE + QK_ROPE
V_HEAD = 64
Q_LORA = 256
KV_LORA = 256
B_WIDTH = B_HEADS * V_HEAD
ROPE_THETA = 500000.0
N_EXPERTS = 64
TOP_K = 6
N_GROUPS = 8
TOPK_GROUPS = 4
PER_GROUP = N_EXPERTS // N_GROUPS
ROUTED_SCALE = 2.5
EXPERT_BLOCK = 256
EPS = 1e-6

LANES = 128
PAD_HEAD = 128
NEG_BIG = -1e30
VMEM_LIMIT = 56 * 1024 * 1024

TM_IN = 512
TM_OUT = 512
TQ_A = 128
TQ_B = 512
TK_B = 512
TM_ROW = 128


def _nt_dot(a, b):
    return lax.dot_general(a, b, (((1,), (1,)), ((), ())), preferred_element_type=F32)


def _dot(a, b):
    return jnp.dot(a, b, preferred_element_type=F32)


def _split_bf16(x):
    hi = x.astype(BF16)
    lo = (x - hi.astype(F32)).astype(BF16)
    return hi, lo


def _sigmoid(x):
    return 1.0 / (1.0 + jnp.exp(-x))


def _silu(x):
    return x * _sigmoid(x)


def _cparams(*sem):
    return pltpu.CompilerParams(dimension_semantics=sem, vmem_limit_bytes=VMEM_LIMIT)


def _adaln_kernel(c_ref, w_ref, b_ref, o_ref):
    s = _silu(c_ref[...])
    o_ref[...] = _dot(s.astype(BF16), w_ref[...].astype(BF16)) + b_ref[...]


def _adaln(c, w_ada, b_ada):
    bt, d = c.shape
    n_chunks = w_ada.shape[1] // d
    return pl.pallas_call(
        _adaln_kernel,
        grid=(n_chunks,),
        in_specs=[
            pl.BlockSpec((bt, d), lambda j: (0, 0)),
            pl.BlockSpec((d, d), lambda j: (0, j)),
            pl.BlockSpec((1, d), lambda j: (0, j)),
        ],
        out_specs=pl.BlockSpec((bt, d), lambda j: (0, j)),
        out_shape=jax.ShapeDtypeStruct((bt, w_ada.shape[1]), F32),
        compiler_params=_cparams("arbitrary"),
        name="adaln",
    )(c, w_ada, b_ada.reshape(1, -1))


def _seg_sumsq(x, seg_ref):
    hi, lo = _split_bf16(x * x)
    width = seg_ref.shape[0]
    outs = []
    for c in range(x.shape[1] // width):
        sl = slice(c * width, (c + 1) * width)
        outs.append(_dot(hi[:, sl], seg_ref[...]) + _dot(lo[:, sl], seg_ref[...]))
    return outs[0] if len(outs) == 1 else jnp.concatenate(outs, axis=1)


def _tile_lanes(t, reps):
    return t if reps == 1 else jnp.concatenate([t] * reps, axis=1)


def _rope_lanes(x, tab_ref, shift, reps):
    width = x.shape[1]
    cos = _tile_lanes(tab_ref[:, 0:LANES], reps)
    sin_p = _tile_lanes(tab_ref[:, LANES:2 * LANES], reps)
    sin_m = _tile_lanes(tab_ref[:, 2 * LANES:3 * LANES], reps)
    return x * cos + pltpu.roll(x, shift, 1) * sin_p + pltpu.roll(x, width - shift, 1) * sin_m


def _inproj_kernel(x_ref, mod_ref, g1_ref, wqkv_ref, wlat_ref, wkr_ref, wqb_ref, wkn_ref, wv_ref,
                   gqa_ref, gka_ref, gql_ref, gkvl_ref, gqb_ref, gkb_ref, seg64_ref, seg128_ref,
                   ropea_ref, ropeb_ref,
                   qa_ref, ka_ref, va_ref, qb_ref, kb_ref, vb_ref):
    x = x_ref[...]
    d = x.shape[1]
    h = x * lax.rsqrt(jnp.sum(x * x, axis=-1, keepdims=True) * (1.0 / d) + EPS) * g1_ref[...]
    h = h * (1.0 + mod_ref[1:2, :]) + mod_ref[0:1, :]
    hb = h.astype(BF16)

    z = _dot(hb, wqkv_ref[...])
    q = z[:, 0:A_WIDTH]
    k = z[:, A_WIDTH:2 * A_WIDTH]
    q = q * lax.rsqrt(_seg_sumsq(q, seg64_ref) * (1.0 / HEAD_DIM) + EPS) * gqa_ref[...]
    k = k * lax.rsqrt(_seg_sumsq(k, seg64_ref) * (1.0 / HEAD_DIM) + EPS) * gka_ref[...]
    reps_a = A_WIDTH // LANES
    q = _rope_lanes(q, ropea_ref, ROT_DIMS_A // 2, reps_a) * (HEAD_DIM ** -0.5)
    k = _rope_lanes(k, ropea_ref, ROT_DIMS_A // 2, reps_a)
    qa_ref[...] = q.astype(BF16)
    ka_ref[...] = k.astype(BF16)
    va_ref[...] = z[:, 2 * A_WIDTH:3 * A_WIDTH].astype(BF16)

    zl = _dot(hb, wlat_ref[...])
    ql = zl[:, 0:Q_LORA]
    kvl = zl[:, Q_LORA:Q_LORA + KV_LORA]
    ql = ql * lax.rsqrt(jnp.sum(ql * ql, axis=-1, keepdims=True) * (1.0 / Q_LORA) + EPS) * gql_ref[...]
    kvl = kvl * lax.rsqrt(jnp.sum(kvl * kvl, axis=-1, keepdims=True) * (1.0 / KV_LORA) + EPS) * gkvl_ref[...]
    qlb = ql.astype(BF16)
    kvlb = kvl.astype(BF16)
    reps_b = B_HEADS
    qb = _dot(qlb, wqb_ref[...])
    qb = qb * lax.rsqrt(_seg_sumsq(qb, seg128_ref) * (1.0 / QK_HEAD) + EPS) * gqb_ref[...]
    qb = _rope_lanes(qb, ropeb_ref, QK_ROPE // 2, reps_b) * (QK_HEAD ** -0.5)
    qb_ref[...] = qb.astype(BF16)
    kr = _dot(hb, wkr_ref[...])
    kb = _dot(kvlb, wkn_ref[...]) + _tile_lanes(kr, reps_b)
    kb = kb * lax.rsqrt(_seg_sumsq(kb, seg128_ref) * (1.0 / QK_HEAD) + EPS) * gkb_ref[...]
    kb = _rope_lanes(kb, ropeb_ref, QK_ROPE // 2, reps_b)
    kb_ref[...] = kb.astype(BF16)
    vb_ref[...] = _dot(kvlb, wv_ref[...]).astype(BF16)


def _rope_table(seq, rot, lane_of_first, period):
    half = rot // 2
    inv = ROPE_THETA ** (-jnp.arange(half, dtype=F32) * 2.0 / rot)
    ang = jnp.arange(seq).astype(F32)[:, None] * inv[None, :]
    cos, sin = jnp.cos(ang), jnp.sin(ang)
    one = jnp.ones((seq, period), F32)
    zero = jnp.zeros((seq, period), F32)
    a, b = lane_of_first, lane_of_first + half
    cos_t = one.at[:, a:a + half].set(cos).at[:, b:b + half].set(cos)
    sin_p = zero.at[:, b:b + half].set(sin)
    sin_m = zero.at[:, a:a + half].set(-sin)
    reps = LANES // period
    return jnp.concatenate([jnp.tile(t, (1, reps)) for t in (cos_t, sin_p, sin_m)], axis=1)


def _pad_heads(w, heads, width, offset=0):
    lead = w.shape[:-1]
    w = w.reshape(lead + (heads, width))
    pad = [(0, 0)] * len(lead) + [(0, 0), (offset, PAD_HEAD - width - offset)]
    return jnp.pad(w, pad).reshape(lead + (heads * PAD_HEAD,))


def _inproj(x, mod, p):
    bt, s, d = x.shape
    tm = min(TM_IN, s)
    ropea = _rope_table(s, ROT_DIMS_A, 0, HEAD_DIM)
    ropeb = _rope_table(s, QK_ROPE, QK_NOPE, PAD_HEAD)
    w_in = p["w_in"]
    wqkv = w_in[:, 0:3 * A_WIDTH].astype(BF16)
    wlat = w_in[:, 3 * A_WIDTH:3 * A_WIDTH + Q_LORA + KV_LORA].astype(BF16)
    wkr = _pad_heads(w_in[:, 3 * A_WIDTH + Q_LORA + KV_LORA:], 1, QK_ROPE, QK_NOPE).astype(BF16)
    wqb = _pad_heads(p["w_qb"], B_HEADS, QK_HEAD).astype(BF16)
    wkv = p["w_kvb"].reshape(KV_LORA, B_HEADS, QK_NOPE + V_HEAD)
    wkn = _pad_heads(wkv[:, :, :QK_NOPE].reshape(KV_LORA, B_HEADS * QK_NOPE), B_HEADS, QK_NOPE).astype(BF16)
    wv = wkv[:, :, QK_NOPE:].reshape(KV_LORA, B_WIDTH).astype(BF16)
    gqa = jnp.tile(p["g_qa"], A_HEADS).reshape(1, -1)
    gka = jnp.tile(p["g_ka"], A_HEADS).reshape(1, -1)
    gqb = _pad_heads(jnp.tile(p["g_qb"], B_HEADS), B_HEADS, QK_HEAD).reshape(1, -1)
    gkb = _pad_heads(jnp.tile(p["g_kb"], B_HEADS), B_HEADS, QK_HEAD).reshape(1, -1)
    seg64 = jnp.kron(jnp.eye(256 // HEAD_DIM, dtype=F32), jnp.ones((HEAD_DIM, HEAD_DIM), F32)).astype(BF16)
    seg128 = jnp.kron(jnp.eye(2, dtype=F32), jnp.ones((PAD_HEAD, PAD_HEAD), F32)).astype(BF16)

    def full(a):
        return pl.BlockSpec(a.shape, lambda i, b: (0,) * a.ndim)

    consts = [p["g_norm1"].reshape(1, -1), wqkv, wlat, wkr, wqb, wkn, wv, gqa, gka,
              p["g_qlat"].reshape(1, -1), p["g_kvlat"].reshape(1, -1), gqb, gkb, seg64, seg128]
    tok = lambda w: pl.BlockSpec((None, tm, w), lambda i, b: (b, i, 0))
    out_widths = (A_WIDTH, A_WIDTH, A_WIDTH, B_HEADS * PAD_HEAD, B_HEADS * PAD_HEAD, B_WIDTH)
    return pl.pallas_call(
        _inproj_kernel,
        grid=(s // tm, bt),
        in_specs=[tok(d), pl.BlockSpec((None, 6, d), lambda i, b: (b, 0, 0))]
        + [full(a) for a in consts]
        + [pl.BlockSpec((tm, 3 * LANES), lambda i, b: (i, 0))] * 2,
        out_specs=[tok(w) for w in out_widths],
        out_shape=[jax.ShapeDtypeStruct((bt, s, w), BF16) for w in out_widths],
        compiler_params=_cparams("arbitrary", "arbitrary"),
        name="inproj",
    )(x, mod, *consts, ropea, ropeb)


def _dilated_kernel(q_ref, k_ref, v_ref, o_ref, lse_ref, *, radius):
    length = q_ref.shape[0]
    win = TQ_A + 2 * radius
    lane = lax.broadcasted_iota(I32, (TQ_A, LANES), 1)
    first = lane < HEAD_DIM
    rel = lax.broadcasted_iota(I32, (TQ_A, win), 0) - lax.broadcasted_iota(I32, (TQ_A, win), 1)

    def body(n, carry):
        q0 = pl.multiple_of(n * TQ_A, TQ_A)
        w0 = pl.multiple_of(jnp.clip(q0 - radius, 0, length - win), radius)
        q = q_ref[pl.ds(q0, TQ_A), :]
        kw = k_ref[pl.ds(w0, win), :]
        vw = v_ref[pl.ds(w0, win), :]
        valid = jnp.abs(rel + (q0 - w0)) <= radius
        outs, lses = [], []
        for keep in (first, jnp.logical_not(first)):
            s = _nt_dot(jnp.where(keep, q, jnp.zeros_like(q)), kw)
            s = jnp.where(valid, s, NEG_BIG)
            m = jnp.max(s, axis=-1, keepdims=True)
            pexp = jnp.exp(s - m)
            den = jnp.sum(pexp, axis=-1, keepdims=True)
            outs.append(_dot(pexp.astype(BF16), vw) / den)
            lses.append(m + jnp.log(den))
        o_ref[pl.ds(q0, TQ_A), :] = jnp.where(first, outs[0], outs[1]).astype(o_ref.dtype)
        lse_ref[pl.ds(q0, TQ_A), :] = jnp.where(first, lses[0], lses[1])
        return carry

    lax.fori_loop(0, length // TQ_A, body, 0)


def _dilated_branch(qa, ka, va, window, dil):
    bt, s, width = qa.shape
    length = s // dil
    radius = window // (2 * dil)
    cols = width // LANES
    view = lambda t: t.reshape(bt, length, dil * width)
    spec = pl.BlockSpec((None, length, LANES), lambda b, j: (b, 0, j))
    o, lse = pl.pallas_call(
        functools.partial(_dilated_kernel, radius=radius),
        grid=(bt, dil * cols),
        in_specs=[spec, spec, spec],
        out_specs=[spec, spec],
        out_shape=[jax.ShapeDtypeStruct((bt, length, dil * width), BF16),
                   jax.ShapeDtypeStruct((bt, length, dil * width), F32)],
        compiler_params=_cparams("arbitrary", "arbitrary"),
        name=f"dilated_{dil}",
    )(view(qa), view(ka), view(va))
    return o.reshape(bt, s, width), lse.reshape(bt, s, width)


def _latent_kernel(q_ref, k_ref, v_ref, o_ref):
    s_len = k_ref.shape[0]
    lane = lax.broadcasted_iota(I32, (TQ_B, LANES), 1)
    first = lane < V_HEAD
    outs = []
    for h in range(2):
        q = q_ref[:, h * PAD_HEAD:(h + 1) * PAD_HEAD]

        def body(j, carry, q=q, h=h):
            m, l, acc = carry
            k0 = pl.multiple_of(j * TK_B, TK_B)
            kt = k_ref[pl.ds(k0, TK_B), h * PAD_HEAD:(h + 1) * PAD_HEAD]
            vt = v_ref[pl.ds(k0, TK_B), :]
            s = _nt_dot(q, kt)
            m_new = jnp.maximum(m, jnp.max(s, axis=-1, keepdims=True))
            alpha = jnp.exp(m - m_new)
            pexp = jnp.exp(s - m_new)
            l = alpha * l + jnp.sum(pexp, axis=-1, keepdims=True)
            acc = alpha * acc + _dot(pexp.astype(BF16), vt)
            return m_new, l, acc

        init = (jnp.full((TQ_B, 1), NEG_BIG, F32), jnp.zeros((TQ_B, 1), F32), jnp.zeros((TQ_B, LANES), F32))
        m, l, acc = lax.fori_loop(0, s_len // TK_B, body, init)
        outs.append(acc / l)
    o_ref[...] = jnp.where(first, outs[0], outs[1]).astype(o_ref.dtype)


def _latent_attention(qb, kb, vb):
    bt, s, _ = qb.shape
    pairs = B_HEADS // 2
    return pl.pallas_call(
        _latent_kernel,
        grid=(bt, pairs, s // TQ_B),
        in_specs=[
            pl.BlockSpec((None, TQ_B, 2 * PAD_HEAD), lambda b, hp, i: (b, i, hp)),
            pl.BlockSpec((None, s, 2 * PAD_HEAD), lambda b, hp, i: (b, 0, hp)),
            pl.BlockSpec((None, s, 2 * V_HEAD), lambda b, hp, i: (b, 0, hp)),
        ],
        out_specs=pl.BlockSpec((None, TQ_B, 2 * V_HEAD), lambda b, hp, i: (b, i, hp)),
        out_shape=jax.ShapeDtypeStruct((bt, s, B_WIDTH), BF16),
        compiler_params=_cparams("arbitrary", "arbitrary", "arbitrary"),
        name="latent_attention",
    )(qb, kb, vb)


def _outproj_kernel(o1_ref, o2_ref, o3_ref, l1_ref, l2_ref, l3_ref, ob_ref, x_ref, mod_ref,
                    woa_ref, wob_ref, g2_ref, wrh_ref, wrl_ref, br_ref, tri_ref,
                    x1_ref, h2_ref, idx_ref, wts_ref, rank_ref, cnt_ref, base_ref):
    first_step = jnp.logical_and(pl.program_id(0) == 0, pl.program_id(1) == 0)

    @pl.when(first_step)
    def _():
        base_ref[...] = jnp.zeros_like(base_ref)

    l1, l2, l3 = l1_ref[...], l2_ref[...], l3_ref[...]
    mx = jnp.maximum(jnp.maximum(l1, l2), l3)
    e1, e2, e3 = jnp.exp(l1 - mx), jnp.exp(l2 - mx), jnp.exp(l3 - mx)
    oa = (e1 * o1_ref[...].astype(F32) + e2 * o2_ref[...].astype(F32) + e3 * o3_ref[...].astype(F32)) / (e1 + e2 + e3)
    mixed = _dot(oa.astype(BF16), woa_ref[...]) + _dot(ob_ref[...], wob_ref[...])
    x1 = x_ref[...] + mod_ref[2:3, :] * mixed
    x1_ref[...] = x1
    d = x1.shape[1]
    h2 = x1 * lax.rsqrt(jnp.sum(x1 * x1, axis=-1, keepdims=True) * (1.0 / d) + EPS) * g2_ref[...]
    h2 = h2 * (1.0 + mod_ref[4:5, :]) + mod_ref[3:4, :]
    h2_ref[...] = h2

    hh, hl = _split_bf16(h2)
    logits = _nt_dot(wrh_ref[...], hh) + _nt_dot(wrh_ref[...], hl) + _nt_dot(wrl_ref[...], hh)
    scores = _sigmoid(logits)
    biased = scores + br_ref[...]
    tm = biased.shape[1]
    sub = lax.broadcasted_iota(I32, (PER_GROUP, tm), 0)
    groups, gscore = [], []
    for g in range(N_GROUPS):
        bg = biased[g * PER_GROUP:(g + 1) * PER_GROUP, :]
        m1 = jnp.max(bg, axis=0, keepdims=True)
        i1 = jnp.min(jnp.where(bg == m1, sub, PER_GROUP), axis=0, keepdims=True)
        m2 = jnp.max(jnp.where(sub == i1, -jnp.inf, bg), axis=0, keepdims=True)
        groups.append(bg)
        gscore.append(m1 + m2)
    masked = []
    for g in range(N_GROUPS):
        ahead = jnp.zeros((1, tm), I32)
        for g2 in range(N_GROUPS):
            if g2 == g:
                continue
            beats = (gscore[g2] > gscore[g]) if g2 > g else (gscore[g2] >= gscore[g])
            ahead = ahead + beats.astype(I32)
        masked.append(jnp.where(ahead < TOPK_GROUPS, groups[g], -jnp.inf))
    cand = jnp.concatenate(masked, axis=0)
    eid = lax.broadcasted_iota(I32, (N_EXPERTS, tm), 0)
    chosen = jnp.zeros((N_EXPERTS, tm), F32)
    sel_idx, sel_w, sel_hot = [], [], []
    for _ in range(TOP_K):
        mv = jnp.max(cand, axis=0, keepdims=True)
        ie = jnp.min(jnp.where(cand == mv, eid, N_EXPERTS), axis=0, keepdims=True)
        hot = eid == ie
        sel_idx.append(ie)
        sel_hot.append(hot)
        sel_w.append(jnp.sum(jnp.where(hot, scores, 0.0), axis=0, keepdims=True))
        cand = jnp.where(hot, -jnp.inf, cand)
        chosen = chosen + hot.astype(F32)
    wsum = sel_w[0]
    for w in sel_w[1:]:
        wsum = wsum + w
    within = _dot(chosen.astype(BF16), tri_ref[...]) + base_ref[...]
    ranks = [jnp.sum(jnp.where(hot, within, 0.0), axis=0, keepdims=True) for hot in sel_hot]
    pad_i = [jnp.zeros((1, tm), I32)] * (8 - TOP_K)
    pad_f = [jnp.zeros((1, tm), F32)] * (8 - TOP_K)
    idx_ref[...] = jnp.concatenate(sel_idx + pad_i, axis=0)
    wts_ref[...] = jnp.concatenate([w / wsum * ROUTED_SCALE for w in sel_w] + pad_f, axis=0)
    rank_ref[...] = jnp.concatenate([r.astype(I32) for r in ranks] + pad_i, axis=0)
    base_ref[...] = base_ref[...] + jnp.sum(chosen, axis=1, keepdims=True)
    cnt_ref[...] = jnp.broadcast_to(base_ref[...], cnt_ref.shape)


def _outproj(o_branches, lse_branches, ob, x, mod, p):
    bt, s, d = x.shape
    tm = min(TM_OUT, s)
    n = bt * s
    woa = p["w_o"][:A_WIDTH].astype(BF16)
    wob = p["w_o"][A_WIDTH:].astype(BF16)
    wr_t = p["w_router"].T
    wrh, wrl = _split_bf16(wr_t)
    tri = (jnp.arange(tm)[:, None] < jnp.arange(tm)[None, :]).astype(BF16)
    consts = [woa, wob, p["g_norm2"].reshape(1, -1), wrh, wrl, p["b_router"].reshape(-1, 1), tri]
    tiles = s // tm
    tok = lambda w: pl.BlockSpec((None, tm, w), lambda b, i: (b, i, 0))
    full = lambda a: pl.BlockSpec(a.shape, lambda b, i: (0,) * a.ndim)
    flat = lambda w: pl.BlockSpec((tm, w), lambda b, i: (b * tiles + i, 0))
    slab = pl.BlockSpec((8, tm), lambda b, i: (0, b * tiles + i))
    return pl.pallas_call(
        _outproj_kernel,
        grid=(bt, tiles),
        in_specs=[tok(A_WIDTH)] * 6 + [tok(B_WIDTH), tok(d), pl.BlockSpec((None, 6, d), lambda b, i: (b, 0, 0))]
        + [full(a) for a in consts],
        out_specs=[flat(d), flat(d), slab, slab, slab, pl.BlockSpec((N_EXPERTS, LANES), lambda b, i: (0, 0))],
        out_shape=[jax.ShapeDtypeStruct((n, d), F32), jax.ShapeDtypeStruct((n, d), F32),
                   jax.ShapeDtypeStruct((8, n), I32), jax.ShapeDtypeStruct((8, n), F32),
                   jax.ShapeDtypeStruct((8, n), I32), jax.ShapeDtypeStruct((N_EXPERTS, LANES), F32)],
        scratch_shapes=[pltpu.VMEM((N_EXPERTS, 1), F32)],
        compiler_params=_cparams("arbitrary", "arbitrary"),
        name="outproj_router",
    )(*o_branches, *lse_branches, ob, x, mod, *consts)


def _dest_kernel(pstart_ref, idx_ref, rank_ref, dest_ref):
    idx = idx_ref[...]
    dest = rank_ref[...]
    for e in range(N_EXPERTS):
        dest = dest + jnp.where(idx == e, pstart_ref[e], 0)
    dest_ref[...] = dest


def _dest(pstart, idx_t, rank_t):
    n = idx_t.shape[1]
    tn = min(2048, n)
    slab = pl.BlockSpec((8, tn), lambda i, ps: (0, i))
    return pl.pallas_call(
        _dest_kernel,
        grid_spec=pltpu.PrefetchScalarGridSpec(
            num_scalar_prefetch=1, grid=(n // tn,), in_specs=[slab, slab], out_specs=slab),
        out_shape=jax.ShapeDtypeStruct((8, n), I32),
        compiler_params=_cparams("arbitrary"),
        name="moe_dest",
    )(pstart, idx_t, rank_t)


def _dispatch_kernel(dest_ref, h_ref, xs_in_ref, xs_ref, sem):
    del xs_in_ref
    tm = h_ref.shape[0]

    def row_copy(j, k):
        return pltpu.make_async_copy(h_ref.at[pl.ds(j, 1), :], xs_ref.at[pl.ds(dest_ref[k, j], 1), :], sem)

    def issue(j, carry):
        for k in range(TOP_K):
            row_copy(j, k).start()
        return carry

    lax.fori_loop(0, tm, issue, 0)
    for k in range(TOP_K):
        pltpu.make_async_copy(h_ref, xs_ref.at[pl.ds(0, tm), :], sem).wait()


def _dispatch(dest_t, h2, n_slots):
    n, d = h2.shape
    tm = min(TM_ROW, n)
    xs0 = jnp.zeros((n_slots, d), F32)
    return pl.pallas_call(
        _dispatch_kernel,
        grid=(n // tm,),
        in_specs=[
            pl.BlockSpec((8, tm), lambda i: (0, i), memory_space=pltpu.SMEM),
            pl.BlockSpec((tm, d), lambda i: (i, 0)),
            pl.BlockSpec(memory_space=pl.ANY),
        ],
        out_specs=pl.BlockSpec(memory_space=pl.ANY),
        out_shape=jax.ShapeDtypeStruct((n_slots, d), F32),
        scratch_shapes=[pltpu.SemaphoreType.DMA(())],
        input_output_aliases={2: 0},
        compiler_params=_cparams("arbitrary"),
        name="moe_dispatch",
    )(dest_t, h2, xs0)


def _expert_kernel(be_ref, nused_ref, xs_ref, wg_ref, wu_ref, wd_ref, ys_ref):
    i = pl.program_id(0)

    @pl.when(i < nused_ref[0])
    def _():
        xb = xs_ref[...].astype(BF16)
        a = _dot(xb, wg_ref[...])
        u = _dot(xb, wu_ref[...])
        ys_ref[...] = _dot((_silu(a) * u).astype(BF16), wd_ref[...])

    @pl.when(i >= nused_ref[0])
    def _():
        ys_ref[...] = jnp.zeros_like(ys_ref)


def _experts(block_e, n_used, xs, wg, wu, wd):
    n_slots, d = xs.shape
    n_blocks = n_slots // EXPERT_BLOCK
    de = wg.shape[2]
    rows = pl.BlockSpec((EXPERT_BLOCK, d), lambda i, be, nu: (i, 0))
    return pl.pallas_call(
        _expert_kernel,
        grid_spec=pltpu.PrefetchScalarGridSpec(
            num_scalar_prefetch=2, grid=(n_blocks,),
            in_specs=[rows,
                      pl.BlockSpec((None, d, de), lambda i, be, nu: (be[i], 0, 0)),
                      pl.BlockSpec((None, d, de), lambda i, be, nu: (be[i], 0, 0)),
                      pl.BlockSpec((None, de, d), lambda i, be, nu: (be[i], 0, 0))],
            out_specs=rows),
        out_shape=jax.ShapeDtypeStruct((n_slots, d), F32),
        compiler_params=_cparams("arbitrary"),
        name="moe_experts",
    )(block_e, n_used, xs, wg, wu, wd)


def _combine_kernel(dest_ref, h_ref, x1_ref, mod_ref, wts_ref, wsg_ref, wsu_ref, wsd_ref, ys_ref,
                    out_ref, buf_ref, sem):
    tm = h_ref.shape[0]

    def issue(j, carry):
        for k in range(TOP_K):
            pltpu.make_async_copy(ys_ref.at[pl.ds(dest_ref[k, j], 1), :],
                                  buf_ref.at[k, pl.ds(j, 1), :], sem).start()
        return carry

    lax.fori_loop(0, tm, issue, 0)
    hb = h_ref[...].astype(BF16)
    act = _silu(_dot(hb, wsg_ref[...])) * _dot(hb, wsu_ref[...])
    y = _dot(act.astype(BF16), wsd_ref[...])
    for k in range(TOP_K):
        pltpu.make_async_copy(ys_ref.at[pl.ds(0, tm), :], buf_ref.at[k], sem).wait()
    wts = wts_ref[...]
    for k in range(TOP_K):
        y = y + wts[:, k:k + 1] * buf_ref[k]
    out_ref[...] = x1_ref[...] + mod_ref[5:6, :] * y


def _combine(dest_t, h2, x1, mod, wts, ys, p, bt, s):
    n, d = h2.shape
    tm = min(TM_ROW, s)
    tiles = s // tm
    wsg, wsu, wsd = (p[k].astype(BF16) for k in ("w_s_gate", "w_s_up", "w_s_down"))
    full = lambda a: pl.BlockSpec(a.shape, lambda i: (0,) * a.ndim)
    rows = pl.BlockSpec((tm, d), lambda i: (i, 0))
    out = pl.pallas_call(
        _combine_kernel,
        grid=(n // tm,),
        in_specs=[
            pl.BlockSpec((8, tm), lambda i: (0, i), memory_space=pltpu.SMEM),
            rows, rows,
            pl.BlockSpec((None, 6, d), lambda i: (i // tiles, 0, 0)),
            pl.BlockSpec((tm, 8), lambda i: (i, 0)),
            full(wsg), full(wsu), full(wsd),
            pl.BlockSpec(memory_space=pl.ANY),
        ],
        out_specs=rows,
        out_shape=jax.ShapeDtypeStruct((n, d), F32),
        scratch_shapes=[pltpu.VMEM((TOP_K, tm, d), F32), pltpu.SemaphoreType.DMA(())],
        compiler_params=_cparams("arbitrary"),
        name="moe_combine",
    )(dest_t, h2, x1, mod, wts, wsg, wsu, wsd, ys)
    return out.reshape(bt, s, d)


def _layer(x, c, p):
    bt, s, d = x.shape
    n = bt * s
    mod = _adaln(c, p["w_ada"], p["b_ada"]).reshape(bt, 6, d)
    qa, ka, va, qb, kb, vb = _inproj(x, mod, p)
    branches = [_dilated_branch(qa, ka, va, window, dil) for window, dil in DILATED_BRANCHES]
    ob = _latent_attention(qb, kb, vb)
    x1, h2, idx_t, wts_t, rank_t, cnt = _outproj(
        [o for o, _ in branches], [l for _, l in branches], ob, x, mod, p)

    counts = cnt[:, 0].astype(I32)
    padded = (counts + EXPERT_BLOCK - 1) // EXPERT_BLOCK * EXPERT_BLOCK
    pend = jnp.cumsum(padded)
    pstart = pend - padded
    n_blocks = (n * TOP_K + N_EXPERTS * (EXPERT_BLOCK - 1)) // EXPERT_BLOCK
    block_e = jnp.clip(jnp.searchsorted(pend, jnp.arange(n_blocks) * EXPERT_BLOCK, side="right"),
                       0, N_EXPERTS - 1).astype(I32)
    n_used = (pend[-1:] // EXPERT_BLOCK).astype(I32)

    dest_t = _dest(pstart.astype(I32), idx_t, rank_t)
    xs = _dispatch(dest_t, h2, n_blocks * EXPERT_BLOCK)
    ys = _experts(block_e, n_used, xs, p["w_e_gate"].astype(BF16), p["w_e_up"].astype(BF16),
                  p["w_e_down"].astype(BF16))
    return _combine(dest_t, h2, x1, mod, wts_t.T, ys, p, bt, s)


_PARAM_NAMES = ("w_ada", "b_ada", "g_norm1", "w_in", "g_qa", "g_ka", "g_qlat", "w_qb", "g_kvlat", "w_kvb",
                "g_qb", "g_kb", "w_o", "g_norm2", "w_router", "b_router", "w_e_gate", "w_e_up", "w_e_down",
                "w_s_gate", "w_s_up", "w_s_down")


def kernel(x_prompt, x_sample, c_prompt, c_sample, w_ada, b_ada, g_norm1, w_in, g_qa, g_ka, g_qlat, w_qb,
           g_kvlat, w_kvb, g_qb, g_kb, w_o, g_norm2, w_router, b_router, w_e_gate, w_e_up, w_e_down,
           w_s_gate, w_s_up, w_s_down):
    stacked = (w_ada, b_ada, g_norm1, w_in, g_qa, g_ka, g_qlat, w_qb, g_kvlat, w_kvb, g_qb, g_kb, w_o,
               g_norm2, w_router, b_router, w_e_gate, w_e_up, w_e_down, w_s_gate, w_s_up, w_s_down)
    y_prompt, y_sample = x_prompt, x_sample
    for layer in range(w_ada.shape[0]):
        p = {name: w[layer] for name, w in zip(_PARAM_NAMES, stacked)}
        y_prompt = _layer(y_prompt, c_prompt, p)
        y_sample = _layer(y_sample, c_sample, p)
    return (y_prompt, y_sample)
```

```python
import functools

import jax
import jax.numpy as jnp
from jax import lax
from jax.experimental import pallas as pl
from jax.experimental.pallas import tpu as pltpu

F32 = jnp.float32
BF16 = jnp.bfloat16
I32 = jnp.int32

HEAD_DIM = 64
A_HEADS = 12
A_WIDTH = A_HEADS * HEAD_DIM
DILATED_BRANCHES = ((128, 1), (512, 4), (2048, 16))
ROT_DIMS_A = HEAD_DIM // 4
B_HEADS = 4
QK_NOPE = 64
QK_ROPE = 32
QK_HEAD = QK_NOPE + QK_ROPE
V_HEAD = 64
Q_LORA = 256
KV_LORA = 256
B_WIDTH = B_HEADS * V_HEAD
ROPE_THETA = 500000.0
N_EXPERTS = 64
TOP_K = 6
N_GROUPS = 8
TOPK_GROUPS = 4
PER_GROUP = N_EXPERTS // N_GROUPS
ROUTED_SCALE = 2.5
EXPERT_BLOCK = 256
EPS = 1e-6

LANES = 128
PAD_HEAD = 128
NEG_BIG = -1e30
VMEM_LIMIT = 56 * 1024 * 1024

TM_IN = 512
TM_OUT = 512
TQ_A = 128
UNROLL_A = 4
TQ_B = 256
KC_B = 128
KV_TILES_B = 2
TM_ROW = 128


def _nt_dot(a, b):
    return lax.dot_general(a, b, (((1,), (1,)), ((), ())), preferred_element_type=F32)


def _dot(a, b):
    return jnp.dot(a, b, preferred_element_type=F32)


def _split_bf16(x):
    hi = x.astype(BF16)
    lo = (x - hi.astype(F32)).astype(BF16)
    return hi, lo


def _sigmoid(x):
    return 1.0 / (1.0 + jnp.exp(-x))


def _silu(x):
    return x * _sigmoid(x)


def _cparams(*sem):
    return pltpu.CompilerParams(dimension_semantics=sem, vmem_limit_bytes=VMEM_LIMIT)


def _adaln_kernel(c_ref, w_ref, b_ref, o_ref):
    s = _silu(c_ref[...])
    o_ref[...] = _dot(s.astype(BF16), w_ref[...].astype(BF16)) + b_ref[...]


def _adaln(c, w_ada, b_ada):
    bt, d = c.shape
    n_chunks = w_ada.shape[1] // d
    return pl.pallas_call(
        _adaln_kernel,
        grid=(n_chunks,),
        in_specs=[
            pl.BlockSpec((bt, d), lambda j: (0, 0)),
            pl.BlockSpec((d, d), lambda j: (0, j)),
            pl.BlockSpec((1, d), lambda j: (0, j)),
        ],
        out_specs=pl.BlockSpec((bt, d), lambda j: (0, j)),
        out_shape=jax.ShapeDtypeStruct((bt, w_ada.shape[1]), F32),
        compiler_params=_cparams("arbitrary"),
        name="adaln",
    )(c, w_ada, b_ada.reshape(1, -1))


def _seg_sumsq(x, seg_ref):
    hi, lo = _split_bf16(x * x)
    width = seg_ref.shape[0]
    outs = []
    for c in range(x.shape[1] // width):
        sl = slice(c * width, (c + 1) * width)
        outs.append(_dot(hi[:, sl], seg_ref[...]) + _dot(lo[:, sl], seg_ref[...]))
    return outs[0] if len(outs) == 1 else jnp.concatenate(outs, axis=1)


def _tile_lanes(t, reps):
    return t if reps == 1 else jnp.concatenate([t] * reps, axis=1)


def _rope_lanes(x, tab_ref, shift, reps):
    width = x.shape[1]
    cos = _tile_lanes(tab_ref[:, 0:LANES], reps)
    sin_p = _tile_lanes(tab_ref[:, LANES:2 * LANES], reps)
    sin_m = _tile_lanes(tab_ref[:, 2 * LANES:3 * LANES], reps)
    return x * cos + pltpu.roll(x, shift, 1) * sin_p + pltpu.roll(x, width - shift, 1) * sin_m


def _inproj_kernel(x_ref, mod_ref, g1_ref, wqkv_ref, wlat_ref, wkr_ref, wqb_ref, wkn_ref, wv_ref,
                   gqa_ref, gka_ref, gql_ref, gkvl_ref, gqb_ref, gkb_ref, seg64_ref, seg128_ref,
                   ropea_ref, ropeb_ref,
                   qa_ref, ka_ref, va_ref, qb_ref, kb_ref, vb_ref):
    x = x_ref[...]
    d = x.shape[1]
    h = x * lax.rsqrt(jnp.sum(x * x, axis=-1, keepdims=True) * (1.0 / d) + EPS) * g1_ref[...]
    h = h * (1.0 + mod_ref[1:2, :]) + mod_ref[0:1, :]
    hb = h.astype(BF16)

    z = _dot(hb, wqkv_ref[...])
    q = z[:, 0:A_WIDTH]
    k = z[:, A_WIDTH:2 * A_WIDTH]
    q = q * lax.rsqrt(_seg_sumsq(q, seg64_ref) * (1.0 / HEAD_DIM) + EPS) * gqa_ref[...]
    k = k * lax.rsqrt(_seg_sumsq(k, seg64_ref) * (1.0 / HEAD_DIM) + EPS) * gka_ref[...]
    reps_a = A_WIDTH // LANES
    q = _rope_lanes(q, ropea_ref, ROT_DIMS_A // 2, reps_a) * (HEAD_DIM ** -0.5)
    k = _rope_lanes(k, ropea_ref, ROT_DIMS_A // 2, reps_a)
    qa_ref[...] = q
    ka_ref[...] = k
    va_ref[...] = z[:, 2 * A_WIDTH:3 * A_WIDTH]

    zl = _dot(hb, wlat_ref[...])
    ql = zl[:, 0:Q_LORA]
    kvl = zl[:, Q_LORA:Q_LORA + KV_LORA]
    ql = ql * lax.rsqrt(jnp.sum(ql * ql, axis=-1, keepdims=True) * (1.0 / Q_LORA) + EPS) * gql_ref[...]
    kvl = kvl * lax.rsqrt(jnp.sum(kvl * kvl, axis=-1, keepdims=True) * (1.0 / KV_LORA) + EPS) * gkvl_ref[...]
    qlb = ql.astype(BF16)
    kvlb = kvl.astype(BF16)
    reps_b = B_HEADS
    qb = _dot(qlb, wqb_ref[...])
    qb = qb * lax.rsqrt(_seg_sumsq(qb, seg128_ref) * (1.0 / QK_HEAD) + EPS) * gqb_ref[...]
    qb = _rope_lanes(qb, ropeb_ref, QK_ROPE // 2, reps_b) * (QK_HEAD ** -0.5)
    qb_ref[...] = qb.T.astype(BF16)
    kr = _dot(hb, wkr_ref[...])
    kb = _dot(kvlb, wkn_ref[...]) + _tile_lanes(kr, reps_b)
    kb = kb * lax.rsqrt(_seg_sumsq(kb, seg128_ref) * (1.0 / QK_HEAD) + EPS) * gkb_ref[...]
    kb = _rope_lanes(kb, ropeb_ref, QK_ROPE // 2, reps_b)
    kb_ref[...] = kb.astype(BF16)
    vb_ref[...] = _dot(kvlb, wv_ref[...]).T.astype(BF16)


def _rope_table(seq, rot, lane_of_first, period):
    half = rot // 2
    inv = ROPE_THETA ** (-jnp.arange(half, dtype=F32) * 2.0 / rot)
    ang = jnp.arange(seq).astype(F32)[:, None] * inv[None, :]
    cos, sin = jnp.cos(ang), jnp.sin(ang)
    one = jnp.ones((seq, period), F32)
    zero = jnp.zeros((seq, period), F32)
    a, b = lane_of_first, lane_of_first + half
    cos_t = one.at[:, a:a + half].set(cos).at[:, b:b + half].set(cos)
    sin_p = zero.at[:, b:b + half].set(sin)
    sin_m = zero.at[:, a:a + half].set(-sin)
    reps = LANES // period
    return jnp.concatenate([jnp.tile(t, (1, reps)) for t in (cos_t, sin_p, sin_m)], axis=1)


def _pad_heads(w, heads, width, offset=0):
    lead = w.shape[:-1]
    w = w.reshape(lead + (heads, width))
    pad = [(0, 0)] * len(lead) + [(0, 0), (offset, PAD_HEAD - width - offset)]
    return jnp.pad(w, pad).reshape(lead + (heads * PAD_HEAD,))


def _inproj(x, mod, p):
    bt, s, d = x.shape
    tm = min(TM_IN, s)
    ropea = _rope_table(s, ROT_DIMS_A, 0, HEAD_DIM)
    ropeb = _rope_table(s, QK_ROPE, QK_NOPE, PAD_HEAD)
    w_in = p["w_in"]
    wqkv = w_in[:, 0:3 * A_WIDTH].astype(BF16)
    wlat = w_in[:, 3 * A_WIDTH:3 * A_WIDTH + Q_LORA + KV_LORA].astype(BF16)
    wkr = _pad_heads(w_in[:, 3 * A_WIDTH + Q_LORA + KV_LORA:], 1, QK_ROPE, QK_NOPE).astype(BF16)
    wqb = _pad_heads(p["w_qb"], B_HEADS, QK_HEAD).astype(BF16)
    wkv = p["w_kvb"].reshape(KV_LORA, B_HEADS, QK_NOPE + V_HEAD)
    wkn = _pad_heads(wkv[:, :, :QK_NOPE].reshape(KV_LORA, B_HEADS * QK_NOPE), B_HEADS, QK_NOPE).astype(BF16)
    wv = wkv[:, :, QK_NOPE:].reshape(KV_LORA, B_WIDTH).astype(BF16)
    gqa = jnp.tile(p["g_qa"], A_HEADS).reshape(1, -1)
    gka = jnp.tile(p["g_ka"], A_HEADS).reshape(1, -1)
    gqb = _pad_heads(jnp.tile(p["g_qb"], B_HEADS), B_HEADS, QK_HEAD).reshape(1, -1)
    gkb = _pad_heads(jnp.tile(p["g_kb"], B_HEADS), B_HEADS, QK_HEAD).reshape(1, -1)
    seg64 = jnp.kron(jnp.eye(256 // HEAD_DIM, dtype=F32), jnp.ones((HEAD_DIM, HEAD_DIM), F32)).astype(BF16)
    seg128 = jnp.kron(jnp.eye(2, dtype=F32), jnp.ones((PAD_HEAD, PAD_HEAD), F32)).astype(BF16)

    def full(a):
        return pl.BlockSpec(a.shape, lambda i, b: (0,) * a.ndim)

    consts = [p["g_norm1"].reshape(1, -1), wqkv, wlat, wkr, wqb, wkn, wv, gqa, gka,
              p["g_qlat"].reshape(1, -1), p["g_kvlat"].reshape(1, -1), gqb, gkb, seg64, seg128]
    tok = lambda w: pl.BlockSpec((None, tm, w), lambda i, b: (b, i, 0))
    qw = B_HEADS * PAD_HEAD
    tok_shape = lambda w, dt=F32: jax.ShapeDtypeStruct((bt, s, w), dt)
    return pl.pallas_call(
        _inproj_kernel,
        grid=(s // tm, bt),
        in_specs=[tok(d), pl.BlockSpec((None, 6, d), lambda i, b: (b, 0, 0))]
        + [full(a) for a in consts]
        + [pl.BlockSpec((tm, 3 * LANES), lambda i, b: (i, 0))] * 2,
        out_specs=[tok(A_WIDTH), tok(A_WIDTH), tok(A_WIDTH),
                   pl.BlockSpec((None, qw, tm), lambda i, b: (b, 0, i)),
                   tok(qw),
                   pl.BlockSpec((None, None, B_WIDTH, tm), lambda i, b: (b, i, 0, 0))],
        out_shape=[tok_shape(A_WIDTH), tok_shape(A_WIDTH), tok_shape(A_WIDTH),
                   jax.ShapeDtypeStruct((bt, qw, s), BF16),
                   tok_shape(qw, BF16),
                   jax.ShapeDtypeStruct((bt, s // tm, B_WIDTH, tm), BF16)],
        compiler_params=_cparams("arbitrary", "arbitrary"),
        name="inproj",
    )(x, mod, *consts, ropea, ropeb)


def _rows(start, size, stride):
    return pl.ds(start, size) if stride == 1 else pl.ds(start, size, stride=stride)


def _dilated_kernel(q_ref, k_ref, v_ref, o_ref, lse_ref):
    s_len = q_ref.shape[0]
    first = lax.broadcasted_iota(I32, (TQ_A, LANES), 1) < HEAD_DIM
    for branch, (window, dil) in enumerate(DILATED_BRANCHES):
        radius = window // (2 * dil)
        win = TQ_A + 2 * radius
        length = s_len // dil
        nq = length // TQ_A
        rel = lax.broadcasted_iota(I32, (TQ_A, win), 0) - lax.broadcasted_iota(I32, (TQ_A, win), 1)

        def group(g, carry, branch=branch, dil=dil, radius=radius, win=win, length=length, nq=nq, rel=rel):
            done = []
            for u in range(UNROLL_A):
                idx = g * UNROLL_A + u
                r = idx // nq
                q0 = (idx % nq) * TQ_A
                w0 = jnp.clip(q0 - radius, 0, length - win)
                qrows = _rows(r + dil * q0, TQ_A, dil)
                krows = _rows(r + dil * w0, win, dil)
                q = q_ref[qrows, :]
                kw = k_ref[krows, :].astype(BF16)
                vw = v_ref[krows, :].astype(BF16)
                valid = jnp.abs(rel + (q0 - w0)) <= radius
                outs, lses = [], []
                for keep in (first, jnp.logical_not(first)):
                    s = _nt_dot(jnp.where(keep, q, 0.0).astype(BF16), kw)
                    s = jnp.where(valid, s, NEG_BIG)
                    m = jnp.max(s, axis=-1, keepdims=True)
                    pexp = jnp.exp(s - m)
                    den = jnp.sum(pexp, axis=-1, keepdims=True)
                    outs.append(_dot(pexp.astype(BF16), vw) / den)
                    lses.append(m + jnp.log(den))
                o_new = jnp.where(first, outs[0], outs[1])
                lse_new = jnp.where(first, lses[0], lses[1])
                if branch > 0:
                    o_old, lse_old = o_ref[qrows, :], lse_ref[qrows, :]
                    mx = jnp.maximum(lse_old, lse_new)
                    w_old, w_new = jnp.exp(lse_old - mx), jnp.exp(lse_new - mx)
                    den = w_old + w_new
                    o_new = (w_old * o_old + w_new * o_new) / den
                    lse_new = mx + jnp.log(den)
                done.append((qrows, o_new, lse_new))
            for qrows, o_new, lse_new in done:
                o_ref[qrows, :] = o_new
                lse_ref[qrows, :] = lse_new
            return carry

        lax.fori_loop(0, dil * nq // UNROLL_A, group, 0)


def _dilated_attention(qa, ka, va):
    bt, s, width = qa.shape
    spec = pl.BlockSpec((None, s, LANES), lambda b, j: (b, 0, j))
    return pl.pallas_call(
        _dilated_kernel,
        grid=(bt, width // LANES),
        in_specs=[spec, spec, spec],
        out_specs=spec,
        out_shape=jax.ShapeDtypeStruct((bt, s, width), F32),
        scratch_shapes=[pltpu.VMEM((s, LANES), F32)],
        compiler_params=_cparams("arbitrary", "arbitrary"),
        name="dilated_attention",
    )(qa, ka, va)


def _latent_kernel(qt_ref, k_ref, vt_ref, o_ref):
    n_kv, _, tk = vt_ref.shape
    tq = qt_ref.shape[1]
    tiles = min(KV_TILES_B, n_kv)
    del tiles
    n_chunks = tk // KC_B
    first = lax.broadcasted_iota(I32, (2 * V_HEAD, tq), 0) < V_HEAD
    qts = [qt_ref[h * PAD_HEAD:(h + 1) * PAD_HEAD, :] for h in range(2)]

    def scores(j, h):
        k0 = pl.multiple_of(j * tk, tk)
        return tuple(_dot(k_ref[pl.ds(k0 + c * KC_B, KC_B), h * PAD_HEAD:(h + 1) * PAD_HEAD], qts[h])
                     for c in range(n_chunks))

    def body(j, carry):
        nxt = jnp.minimum(j + 1, n_kv - 1)
        new = []
        for h in range(2):
            m, l, acc, sts = carry[h]
            sts_next = scores(nxt, h)
            for c in range(n_chunks):
                m_new = jnp.maximum(m, jnp.max(sts[c], axis=0, keepdims=True))
                alpha = jnp.exp(m - m_new)
                pexp = jnp.exp(sts[c] - m_new)
                l = alpha * l + jnp.sum(pexp, axis=0, keepdims=True)
                vt = vt_ref[j, h * V_HEAD:(h + 1) * V_HEAD, c * KC_B:(c + 1) * KC_B]
                acc = alpha * acc + _dot(vt, pexp.astype(BF16))
                m = m_new
            new.append((m, l, acc, sts_next))
        return tuple(new)

    init = tuple((jnp.full((1, tq), NEG_BIG, F32), jnp.zeros((1, tq), F32), jnp.zeros((V_HEAD, tq), F32),
                  scores(0, h)) for h in range(2))
    (_, l0, acc0, _), (_, l1, acc1, _) = lax.fori_loop(0, n_kv, body, init)
    ot = jnp.concatenate([acc0 / l0, acc1 / l1], axis=0)
    o_ref[...] = ot.T.astype(o_ref.dtype)


def _latent_attention(qbt, kb, vbt):
    bt, s, _ = kb.shape
    pairs = B_HEADS // 2
    n_kv, tk = vbt.shape[1], vbt.shape[3]
    tq = min(TQ_B, s)
    return pl.pallas_call(
        _latent_kernel,
        grid=(bt, pairs, s // tq),
        in_specs=[
            pl.BlockSpec((None, 2 * PAD_HEAD, tq), lambda b, hp, i: (b, hp, i)),
            pl.BlockSpec((None, s, 2 * PAD_HEAD), lambda b, hp, i: (b, 0, hp)),
            pl.BlockSpec((None, n_kv, 2 * V_HEAD, tk), lambda b, hp, i: (b, 0, hp, 0)),
        ],
        out_specs=pl.BlockSpec((None, tq, 2 * V_HEAD), lambda b, hp, i: (b, i, hp)),
        out_shape=jax.ShapeDtypeStruct((bt, s, B_WIDTH), BF16),
        compiler_params=_cparams("arbitrary", "arbitrary", "arbitrary"),
        name="latent_attention",
    )(qbt, kb, vbt)


def _outproj_kernel(oa_ref, ob_ref, x_ref, mod_ref,
                    woa_ref, wob_ref, g2_ref, wrh_ref, wrl_ref, br_ref, tri_ref,
                    x1_ref, h2_ref, idx_ref, wts_ref, rank_ref, cnt_ref, base_ref):
    first_step = jnp.logical_and(pl.program_id(0) == 0, pl.program_id(1) == 0)

    @pl.when(first_step)
    def _():
        base_ref[...] = jnp.zeros_like(base_ref)

    mixed = _dot(oa_ref[...].astype(BF16), woa_ref[...]) + _dot(ob_ref[...], wob_ref[...])
    x1 = x_ref[...] + mod_ref[2:3, :] * mixed
    x1_ref[...] = x1
    d = x1.shape[1]
    h2 = x1 * lax.rsqrt(jnp.sum(x1 * x1, axis=-1, keepdims=True) * (1.0 / d) + EPS) * g2_ref[...]
    h2 = h2 * (1.0 + mod_ref[4:5, :]) + mod_ref[3:4, :]
    h2_ref[...] = h2

    hh, hl = _split_bf16(h2)
    logits = _nt_dot(wrh_ref[...], hh) + _nt_dot(wrh_ref[...], hl) + _nt_dot(wrl_ref[...], hh)
    scores = _sigmoid(logits)
    biased = scores + br_ref[...]
    tm = biased.shape[1]
    sub = lax.broadcasted_iota(I32, (PER_GROUP, tm), 0)
    groups, gscore = [], []
    for g in range(N_GROUPS):
        bg = biased[g * PER_GROUP:(g + 1) * PER_GROUP, :]
        m1 = jnp.max(bg, axis=0, keepdims=True)
        i1 = jnp.min(jnp.where(bg == m1, sub, PER_GROUP), axis=0, keepdims=True)
        m2 = jnp.max(jnp.where(sub == i1, -jnp.inf, bg), axis=0, keepdims=True)
        groups.append(bg)
        gscore.append(m1 + m2)
    masked = []
    for g in range(N_GROUPS):
        ahead = jnp.zeros((1, tm), I32)
        for g2 in range(N_GROUPS):
            if g2 == g:
                continue
            beats = (gscore[g2] > gscore[g]) if g2 > g else (gscore[g2] >= gscore[g])
            ahead = ahead + beats.astype(I32)
        masked.append(jnp.where(ahead < TOPK_GROUPS, groups[g], -jnp.inf))
    cand = jnp.concatenate(masked, axis=0)
    eid = lax.broadcasted_iota(I32, (N_EXPERTS, tm), 0)
    chosen = jnp.zeros((N_EXPERTS, tm), F32)
    sel_idx, sel_w, sel_hot = [], [], []
    for _ in range(TOP_K):
        mv = jnp.max(cand, axis=0, keepdims=True)
        ie = jnp.min(jnp.where(cand == mv, eid, N_EXPERTS), axis=0, keepdims=True)
        hot = eid == ie
        sel_idx.append(ie)
        sel_hot.append(hot)
        sel_w.append(jnp.sum(jnp.where(hot, scores, 0.0), axis=0, keepdims=True))
        cand = jnp.where(hot, -jnp.inf, cand)
        chosen = chosen + hot.astype(F32)
    wsum = sel_w[0]
    for w in sel_w[1:]:
        wsum = wsum + w
    within = _dot(chosen.astype(BF16), tri_ref[...]) + base_ref[...]
    ranks = [jnp.sum(jnp.where(hot, within, 0.0), axis=0, keepdims=True) for hot in sel_hot]
    pad_i = [jnp.zeros((1, tm), I32)] * (8 - TOP_K)
    pad_f = [jnp.zeros((1, tm), F32)] * (8 - TOP_K)
    idx_ref[...] = jnp.concatenate(sel_idx + pad_i, axis=0)
    wts_ref[...] = jnp.concatenate([w / wsum * ROUTED_SCALE for w in sel_w] + pad_f, axis=0)
    rank_ref[...] = jnp.concatenate([r.astype(I32) for r in ranks] + pad_i, axis=0)
    base_ref[...] = base_ref[...] + jnp.sum(chosen, axis=1, keepdims=True)
    cnt_ref[...] = jnp.broadcast_to(base_ref[...], cnt_ref.shape)


def _outproj(oa, ob, x, mod, p):
    bt, s, d = x.shape
    tm = min(TM_OUT, s)
    n = bt * s
    woa = p["w_o"][:A_WIDTH].astype(BF16)
    wob = p["w_o"][A_WIDTH:].astype(BF16)
    wr_t = p["w_router"].T
    wrh, wrl = _split_bf16(wr_t)
    tri = (jnp.arange(tm)[:, None] < jnp.arange(tm)[None, :]).astype(BF16)
    consts = [woa, wob, p["g_norm2"].reshape(1, -1), wrh, wrl, p["b_router"].reshape(-1, 1), tri]
    tiles = s // tm
    tok = lambda w: pl.BlockSpec((None, tm, w), lambda b, i: (b, i, 0))
    full = lambda a: pl.BlockSpec(a.shape, lambda b, i: (0,) * a.ndim)
    flat = lambda w: pl.BlockSpec((tm, w), lambda b, i: (b * tiles + i, 0))
    slab = pl.BlockSpec((8, tm), lambda b, i: (0, b * tiles + i))
    return pl.pallas_call(
        _outproj_kernel,
        grid=(bt, tiles),
        in_specs=[tok(A_WIDTH), tok(B_WIDTH), tok(d), pl.BlockSpec((None, 6, d), lambda b, i: (b, 0, 0))]
        + [full(a) for a in consts],
        out_specs=[flat(d), flat(d), slab, slab, slab, pl.BlockSpec((N_EXPERTS, LANES), lambda b, i: (0, 0))],
        out_shape=[jax.ShapeDtypeStruct((n, d), F32), jax.ShapeDtypeStruct((n, d), F32),
                   jax.ShapeDtypeStruct((8, n), I32), jax.ShapeDtypeStruct((8, n), F32),
                   jax.ShapeDtypeStruct((8, n), I32), jax.ShapeDtypeStruct((N_EXPERTS, LANES), F32)],
        scratch_shapes=[pltpu.VMEM((N_EXPERTS, 1), F32)],
        compiler_params=_cparams("arbitrary", "arbitrary"),
        name="outproj_router",
    )(oa, ob, x, mod, *consts)


def _dest_kernel(pstart_ref, idx_ref, rank_ref, dest_ref):
    idx = idx_ref[...]
    dest = rank_ref[...]
    for e in range(N_EXPERTS):
        dest = dest + jnp.where(idx == e, pstart_ref[e], 0)
    dest_ref[...] = dest


def _dest(pstart, idx_t, rank_t):
    n = idx_t.shape[1]
    tn = min(2048, n)
    slab = pl.BlockSpec((8, tn), lambda i, ps: (0, i))
    return pl.pallas_call(
        _dest_kernel,
        grid_spec=pltpu.PrefetchScalarGridSpec(
            num_scalar_prefetch=1, grid=(n // tn,), in_specs=[slab, slab], out_specs=slab),
        out_shape=jax.ShapeDtypeStruct((8, n), I32),
        compiler_params=_cparams("arbitrary"),
        name="moe_dest",
    )(pstart, idx_t, rank_t)


def _dispatch_kernel(pad_start_ref, pad_len_ref, nused_ref, dest_ref, h_ref, xs_ref, zero_ref, sem, pad_sem):
    tm = h_ref.shape[0]

    @pl.when(pl.program_id(0) == 0)
    def _():
        zero_ref[...] = jnp.zeros_like(zero_ref)

        def pad_copy(e, r):
            return pltpu.make_async_copy(zero_ref.at[pl.ds(0, 1), :],
                                         xs_ref.at[pl.ds(pad_start_ref[e] + r, 1), :], pad_sem)

        def per_expert(e, carry):
            lax.fori_loop(0, pad_len_ref[e], lambda r, c: (pad_copy(e, r).start(), c)[1], 0)
            lax.fori_loop(0, pad_len_ref[e], lambda r, c: (pad_copy(e, r).wait(), c)[1], 0)
            return carry

        lax.fori_loop(0, N_EXPERTS, per_expert, 0)

        def block_copy(b):
            return pltpu.make_async_copy(zero_ref, xs_ref.at[pl.ds(b * EXPERT_BLOCK, EXPERT_BLOCK), :], pad_sem)

        n_blocks = xs_ref.shape[0] // EXPERT_BLOCK
        lax.fori_loop(nused_ref[0], n_blocks, lambda b, c: (block_copy(b).start(), c)[1], 0)
        lax.fori_loop(nused_ref[0], n_blocks, lambda b, c: (block_copy(b).wait(), c)[1], 0)

    def issue(j, carry):
        for k in range(TOP_K):
            pltpu.make_async_copy(h_ref.at[pl.ds(j, 1), :], xs_ref.at[pl.ds(dest_ref[k, j], 1), :], sem).start()
        return carry

    lax.fori_loop(0, tm, issue, 0)
    for k in range(TOP_K):
        pltpu.make_async_copy(h_ref, xs_ref.at[pl.ds(0, tm), :], sem).wait()


def _dispatch(pad_start, pad_len, n_used, dest_t, h2, n_slots):
    n, d = h2.shape
    tm = min(TM_ROW, n)
    return pl.pallas_call(
        _dispatch_kernel,
        grid_spec=pltpu.PrefetchScalarGridSpec(
            num_scalar_prefetch=3, grid=(n // tm,),
            in_specs=[
                pl.BlockSpec((8, tm), lambda i, ps, pn, nu: (0, i), memory_space=pltpu.SMEM),
                pl.BlockSpec((tm, d), lambda i, ps, pn, nu: (i, 0)),
            ],
            out_specs=pl.BlockSpec(memory_space=pl.ANY),
            scratch_shapes=[pltpu.VMEM((EXPERT_BLOCK, d), F32), pltpu.SemaphoreType.DMA(()),
                            pltpu.SemaphoreType.DMA(())]),
        out_shape=jax.ShapeDtypeStruct((n_slots, d), F32),
        compiler_params=_cparams("arbitrary"),
        name="moe_dispatch",
    )(pad_start, pad_len, n_used, dest_t, h2)


def _expert_kernel(be_ref, nused_ref, xs_ref, wg_ref, wu_ref, wd_ref, ys_ref):
    i = pl.program_id(0)

    @pl.when(i < nused_ref[0])
    def _():
        xb = xs_ref[...].astype(BF16)
        a = _dot(xb, wg_ref[...])
        u = _dot(xb, wu_ref[...])
        ys_ref[...] = _dot((_silu(a) * u).astype(BF16), wd_ref[...])

    @pl.when(i >= nused_ref[0])
    def _():
        ys_ref[...] = jnp.zeros_like(ys_ref)


def _experts(block_e, n_used, xs, wg, wu, wd):
    n_slots, d = xs.shape
    n_blocks = n_slots // EXPERT_BLOCK
    de = wg.shape[2]
    rows = pl.BlockSpec((EXPERT_BLOCK, d), lambda i, be, nu: (i, 0))
    used_rows = pl.BlockSpec((EXPERT_BLOCK, d), lambda i, be, nu: (jnp.minimum(i, nu[0] - 1), 0))
    return pl.pallas_call(
        _expert_kernel,
        grid_spec=pltpu.PrefetchScalarGridSpec(
            num_scalar_prefetch=2, grid=(n_blocks,),
            in_specs=[used_rows,
                      pl.BlockSpec((None, d, de), lambda i, be, nu: (be[i], 0, 0)),
                      pl.BlockSpec((None, d, de), lambda i, be, nu: (be[i], 0, 0)),
                      pl.BlockSpec((None, de, d), lambda i, be, nu: (be[i], 0, 0))],
            out_specs=rows),
        out_shape=jax.ShapeDtypeStruct((n_slots, d), F32),
        compiler_params=_cparams("arbitrary"),
        name="moe_experts",
    )(block_e, n_used, xs, wg, wu, wd)


def _combine_kernel(dest_ref, h_ref, x1_ref, mod_ref, wts_ref, wsg_ref, wsu_ref, wsd_ref, ys_ref,
                    out_ref, buf_ref, sem):
    tm = h_ref.shape[0]

    def issue(j, carry):
        for k in range(TOP_K):
            pltpu.make_async_copy(ys_ref.at[pl.ds(dest_ref[k, j], 1), :],
                                  buf_ref.at[k, pl.ds(j, 1), :], sem).start()
        return carry

    lax.fori_loop(0, tm, issue, 0)
    hb = h_ref[...].astype(BF16)
    act = _silu(_dot(hb, wsg_ref[...])) * _dot(hb, wsu_ref[...])
    y = _dot(act.astype(BF16), wsd_ref[...])
    for k in range(TOP_K):
        pltpu.make_async_copy(ys_ref.at[pl.ds(0, tm), :], buf_ref.at[k], sem).wait()
    wts = wts_ref[...]
    for k in range(TOP_K):
        y = y + wts[:, k:k + 1] * buf_ref[k]
    out_ref[...] = x1_ref[...] + mod_ref[5:6, :] * y


def _combine(dest_t, h2, x1, mod, wts, ys, p, bt, s):
    n, d = h2.shape
    tm = min(TM_ROW, s)
    tiles = s // tm
    wsg, wsu, wsd = (p[k].astype(BF16) for k in ("w_s_gate", "w_s_up", "w_s_down"))
    full = lambda a: pl.BlockSpec(a.shape, lambda i: (0,) * a.ndim)
    rows = pl.BlockSpec((tm, d), lambda i: (i, 0))
    out = pl.pallas_call(
        _combine_kernel,
        grid=(n // tm,),
        in_specs=[
            pl.BlockSpec((8, tm), lambda i: (0, i), memory_space=pltpu.SMEM),
            rows, rows,
            pl.BlockSpec((None, 6, d), lambda i: (i // tiles, 0, 0)),
            pl.BlockSpec((tm, 8), lambda i: (i, 0)),
            full(wsg), full(wsu), full(wsd),
            pl.BlockSpec(memory_space=pl.ANY),
        ],
        out_specs=rows,
        out_shape=jax.ShapeDtypeStruct((n, d), F32),
        scratch_shapes=[pltpu.VMEM((TOP_K, tm, d), F32), pltpu.SemaphoreType.DMA(())],
        compiler_params=_cparams("arbitrary"),
        name="moe_combine",
    )(dest_t, h2, x1, mod, wts, wsg, wsu, wsd, ys)
    return out.reshape(bt, s, d)


def _layer(x, c, p):
    bt, s, d = x.shape
    n = bt * s
    mod = _adaln(c, p["w_ada"], p["b_ada"]).reshape(bt, 6, d)
    qa, ka, va, qb, kb, vb = _inproj(x, mod, p)
    oa = _dilated_attention(qa, ka, va)
    ob = _latent_attention(qb, kb, vb)
    x1, h2, idx_t, wts_t, rank_t, cnt = _outproj(oa, ob, x, mod, p)

    counts = cnt[:, 0].astype(I32)
    padded = (counts + EXPERT_BLOCK - 1) // EXPERT_BLOCK * EXPERT_BLOCK
    pend = jnp.cumsum(padded)
    pstart = pend - padded
    n_blocks = (n * TOP_K + N_EXPERTS * (EXPERT_BLOCK - 1)) // EXPERT_BLOCK
    block_first = jnp.arange(n_blocks, dtype=I32) * EXPERT_BLOCK
    block_e = jnp.minimum(jnp.sum((pend[None, :] <= block_first[:, None]).astype(I32), axis=1), N_EXPERTS - 1)
    n_used = (pend[-1:] // EXPERT_BLOCK).astype(I32)

    dest_t = _dest(pstart.astype(I32), idx_t, rank_t)
    xs = _dispatch((pstart + counts).astype(I32), (padded - counts).astype(I32), n_used, dest_t, h2,
                   n_blocks * EXPERT_BLOCK)
    ys = _experts(block_e, n_used, xs, p["w_e_gate"].astype(BF16), p["w_e_up"].astype(BF16),
                  p["w_e_down"].astype(BF16))
    return _combine(dest_t, h2, x1, mod, wts_t.T, ys, p, bt, s)


_PARAM_NAMES = ("w_ada", "b_ada", "g_norm1", "w_in", "g_qa", "g_ka", "g_qlat", "w_qb", "g_kvlat", "w_kvb",
                "g_qb", "g_kb", "w_o", "g_norm2", "w_router", "b_router", "w_e_gate", "w_e_up", "w_e_down",
                "w_s_gate", "w_s_up", "w_s_down")


def kernel(x_prompt, x_sample, c_prompt, c_sample, w_ada, b_ada, g_norm1, w_in, g_qa, g_ka, g_qlat, w_qb,
           g_kvlat, w_kvb, g_qb, g_kb, w_o, g_norm2, w_router, b_router, w_e_gate, w_e_up, w_e_down,
           w_s_gate, w_s_up, w_s_down):
    stacked = (w_ada, b_ada, g_norm1, w_in, g_qa, g_ka, g_qlat, w_qb, g_kvlat, w_kvb, g_qb, g_kb, w_o,
               g_norm2, w_router, b_router, w_e_gate, w_e_up, w_e_down, w_s_gate, w_s_up, w_s_down)
    y_prompt, y_sample = x_prompt, x_sample
    for layer in range(w_ada.shape[0]):
        p = {name: w[layer] for name, w in zip(_PARAM_NAMES, stacked)}
        y_prompt = _layer(y_prompt, c_prompt, p)
        y_sample = _layer(y_sample, c_sample, p)
    return (y_prompt, y_sample)
```

```python
import functools

import jax
import jax.numpy as jnp
from jax import lax
from jax.experimental import pallas as pl
from jax.experimental.pallas import tpu as pltpu

F32 = jnp.float32
BF16 = jnp.bfloat16
I32 = jnp.int32
U32 = jnp.uint32

HEAD_DIM = 64
A_HEADS = 12
A_WIDTH = A_HEADS * HEAD_DIM
DILATED_BRANCHES = ((128, 1), (512, 4), (2048, 16))
ROT_DIMS_A = HEAD_DIM // 4
B_HEADS = 4
QK_NOPE = 64
QK_ROPE = 32
QK_HEAD = QK_NOPE + QK_ROPE
V_HEAD = 64
Q_LORA = 256
KV_LORA = 256
B_WIDTH = B_HEADS * V_HEAD
ROPE_THETA = 500000.0
N_EXPERTS = 64
TOP_K = 6
N_GROUPS = 8
TOPK_GROUPS = 4
PER_GROUP = N_EXPERTS // N_GROUPS
ROUTED_SCALE = 2.5
EXPERT_BLOCK = 256
EPS = 1e-6

LOG2E = 1.4426950408889634
BOUND_SLACK = 1.02
EXP2_SAFE_RANGE = 120.0

LANES = 128
PAD_HEAD = 128
NEG_BIG = -1e30
VMEM_LIMIT = 56 * 1024 * 1024

TM_IN = 512
TM_OUT = 512
TQ_A = 128
UNROLL_A = 4
NORM_ROWS_A = 512
TQ_B = 256
KC_B = 128
KC_FAST_B = 256
KV_TILES_B = 2
TM_ROW = 128
ISSUE_UNROLL = 4
ROW_SUB = 4


def _nt_dot(a, b):
    return lax.dot_general(a, b, (((1,), (1,)), ((), ())), preferred_element_type=F32)


def _dot(a, b):
    return jnp.dot(a, b, preferred_element_type=F32)


def _split_bf16(x):
    hi = x.astype(BF16)
    lo = (x - hi.astype(F32)).astype(BF16)
    return hi, lo


def _sigmoid(x):
    return 1.0 / (1.0 + jnp.exp(-x))


def _silu(x):
    return x * _sigmoid(x)


def _pack_rows(x):
    w = x.shape[1] // 2
    lo = lax.bitcast_convert_type(x[:, :w].astype(BF16).astype(F32), U32)
    hi = lax.bitcast_convert_type(x[:, w:].astype(BF16).astype(F32), U32)
    return (lo >> 16) | (hi & jnp.uint32(0xFFFF0000))


def _unpack_words(words):
    lo = lax.bitcast_convert_type(words << 16, F32)
    hi = lax.bitcast_convert_type(words & jnp.uint32(0xFFFF0000), F32)
    return lo, hi


def _store_token_rows(ref, words):
    m = ref.shape[0] // ROW_SUB
    for c in range(ROW_SUB):
        ref[pl.ds(c, m, stride=ROW_SUB), :] = words[:, c * LANES:(c + 1) * LANES]


def _load_token_rows(ref):
    m = ref.shape[0] // ROW_SUB
    halves = [_unpack_words(ref[pl.ds(c, m, stride=ROW_SUB), :]) for c in range(ROW_SUB)]
    return jnp.concatenate([lo for lo, _ in halves] + [hi for _, hi in halves], axis=1)


def _cparams(*sem):
    return pltpu.CompilerParams(dimension_semantics=sem, vmem_limit_bytes=VMEM_LIMIT)


def _adaln_kernel(c_ref, w_ref, b_ref, o_ref):
    s = _silu(c_ref[...])
    o_ref[...] = _dot(s.astype(BF16), w_ref[...].astype(BF16)) + b_ref[...]


def _adaln(c, w_ada, b_ada):
    bt, d = c.shape
    n_chunks = w_ada.shape[1] // d
    return pl.pallas_call(
        _adaln_kernel,
        grid=(n_chunks,),
        in_specs=[
            pl.BlockSpec((bt, d), lambda j: (0, 0)),
            pl.BlockSpec((d, d), lambda j: (0, j)),
            pl.BlockSpec((1, d), lambda j: (0, j)),
        ],
        out_specs=pl.BlockSpec((bt, d), lambda j: (0, j)),
        out_shape=jax.ShapeDtypeStruct((bt, w_ada.shape[1]), F32),
        compiler_params=_cparams("arbitrary"),
        name="adaln",
    )(c, w_ada, b_ada.reshape(1, -1))


def _seg_sumsq(x, seg_ref):
    hi, lo = _split_bf16(x * x)
    width = seg_ref.shape[0]
    outs = []
    for c in range(x.shape[1] // width):
        sl = slice(c * width, (c + 1) * width)
        outs.append(_dot(hi[:, sl], seg_ref[...]) + _dot(lo[:, sl], seg_ref[...]))
    return outs[0] if len(outs) == 1 else jnp.concatenate(outs, axis=1)


def _tile_lanes(t, reps):
    return t if reps == 1 else jnp.concatenate([t] * reps, axis=1)


def _rope_lanes(x, tab_ref, shift, reps):
    width = x.shape[1]
    cos = _tile_lanes(tab_ref[:, 0:LANES], reps)
    sin_p = _tile_lanes(tab_ref[:, LANES:2 * LANES], reps)
    sin_m = _tile_lanes(tab_ref[:, 2 * LANES:3 * LANES], reps)
    return x * cos + pltpu.roll(x, shift, 1) * sin_p + pltpu.roll(x, width - shift, 1) * sin_m


def _inproj_kernel(x_ref, mod_ref, g1_ref, wqkv_ref, wlat_ref, wkr_ref, wqb_ref, wkn_ref, wv_ref,
                   gqa_ref, gka_ref, gql_ref, gkvl_ref, gqb_ref, gkb_ref, seg64_ref, seg128_ref, qpad_ref, kpad_ref,
                   ropea_ref, ropeb_ref,
                   qa_ref, ka_ref, va_ref, qb_ref, kb_ref, vb_ref):
    x = x_ref[...]
    d = x.shape[1]
    h = x * lax.rsqrt(jnp.sum(x * x, axis=-1, keepdims=True) * (1.0 / d) + EPS) * g1_ref[...]
    h = h * (1.0 + mod_ref[1:2, :]) + mod_ref[0:1, :]
    hb = h.astype(BF16)

    z = _dot(hb, wqkv_ref[...])
    q = z[:, 0:A_WIDTH]
    k = z[:, A_WIDTH:2 * A_WIDTH]
    q = q * lax.rsqrt(_seg_sumsq(q, seg64_ref) * (1.0 / HEAD_DIM) + EPS) * gqa_ref[...]
    k = k * lax.rsqrt(_seg_sumsq(k, seg64_ref) * (1.0 / HEAD_DIM) + EPS) * gka_ref[...]
    reps_a = A_WIDTH // LANES
    q = _rope_lanes(q, ropea_ref, ROT_DIMS_A // 2, reps_a) * (HEAD_DIM ** -0.5 * LOG2E)
    k = _rope_lanes(k, ropea_ref, ROT_DIMS_A // 2, reps_a)
    qa_ref[...] = q
    ka_ref[...] = k
    va_ref[...] = z[:, 2 * A_WIDTH:3 * A_WIDTH]

    zl = _dot(hb, wlat_ref[...])
    ql = zl[:, 0:Q_LORA]
    kvl = zl[:, Q_LORA:Q_LORA + KV_LORA]
    ql = ql * lax.rsqrt(jnp.sum(ql * ql, axis=-1, keepdims=True) * (1.0 / Q_LORA) + EPS) * gql_ref[...]
    kvl = kvl * lax.rsqrt(jnp.sum(kvl * kvl, axis=-1, keepdims=True) * (1.0 / KV_LORA) + EPS) * gkvl_ref[...]
    qlb = ql.astype(BF16)
    kvlb = kvl.astype(BF16)
    reps_b = B_HEADS
    qb = _dot(qlb, wqb_ref[...])
    qb = qb * lax.rsqrt(_seg_sumsq(qb, seg128_ref) * (1.0 / QK_HEAD) + EPS) * gqb_ref[...]
    qb = _rope_lanes(qb, ropeb_ref, QK_ROPE // 2, reps_b) * (QK_HEAD ** -0.5 * LOG2E) + qpad_ref[...]
    qb_ref[...] = qb.T.astype(BF16)
    kr = _dot(hb, wkr_ref[...])
    kb = _dot(kvlb, wkn_ref[...]) + _tile_lanes(kr, reps_b)
    kb = kb * lax.rsqrt(_seg_sumsq(kb, seg128_ref) * (1.0 / QK_HEAD) + EPS) * gkb_ref[...]
    kb = _rope_lanes(kb, ropeb_ref, QK_ROPE // 2, reps_b) + kpad_ref[...]
    kb_ref[...] = kb.astype(BF16)
    vb_ref[...] = _dot(kvlb, wv_ref[...]).T.astype(BF16)


def _score_bound(g_q, g_k, head_dim):
    bound = LOG2E * head_dim ** 0.5 * jnp.max(jnp.abs(g_q)) * jnp.max(jnp.abs(g_k))
    return bound * BOUND_SLACK


def _bound_is_safe(bound):
    return 2.0 * bound < EXP2_SAFE_RANGE


def _rope_table(seq, rot, lane_of_first, period):
    half = rot // 2
    inv = ROPE_THETA ** (-jnp.arange(half, dtype=F32) * 2.0 / rot)
    ang = jnp.arange(seq).astype(F32)[:, None] * inv[None, :]
    cos, sin = jnp.cos(ang), jnp.sin(ang)
    one = jnp.ones((seq, period), F32)
    zero = jnp.zeros((seq, period), F32)
    a, b = lane_of_first, lane_of_first + half
    cos_t = one.at[:, a:a + half].set(cos).at[:, b:b + half].set(cos)
    sin_p = zero.at[:, b:b + half].set(sin)
    sin_m = zero.at[:, a:a + half].set(-sin)
    reps = LANES // period
    return jnp.concatenate([jnp.tile(t, (1, reps)) for t in (cos_t, sin_p, sin_m)], axis=1)


def _pad_heads(w, heads, width, offset=0):
    lead = w.shape[:-1]
    w = w.reshape(lead + (heads, width))
    pad = [(0, 0)] * len(lead) + [(0, 0), (offset, PAD_HEAD - width - offset)]
    return jnp.pad(w, pad).reshape(lead + (heads * PAD_HEAD,))


def _inproj(x, mod, p):
    bt, s, d = x.shape
    tm = min(TM_IN, s)
    ropea = _rope_table(s, ROT_DIMS_A, 0, HEAD_DIM)
    ropeb = _rope_table(s, QK_ROPE, QK_NOPE, PAD_HEAD)
    w_in = p["w_in"]
    wqkv = w_in[:, 0:3 * A_WIDTH].astype(BF16)
    wlat = w_in[:, 3 * A_WIDTH:3 * A_WIDTH + Q_LORA + KV_LORA].astype(BF16)
    wkr = _pad_heads(w_in[:, 3 * A_WIDTH + Q_LORA + KV_LORA:], 1, QK_ROPE, QK_NOPE).astype(BF16)
    wqb = _pad_heads(p["w_qb"], B_HEADS, QK_HEAD).astype(BF16)
    wkv = p["w_kvb"].reshape(KV_LORA, B_HEADS, QK_NOPE + V_HEAD)
    wkn = _pad_heads(wkv[:, :, :QK_NOPE].reshape(KV_LORA, B_HEADS * QK_NOPE), B_HEADS, QK_NOPE).astype(BF16)
    wv = wkv[:, :, QK_NOPE:].reshape(KV_LORA, B_WIDTH).astype(BF16)
    gqa = jnp.tile(p["g_qa"], A_HEADS).reshape(1, -1)
    gka = jnp.tile(p["g_ka"], A_HEADS).reshape(1, -1)
    gqb = _pad_heads(jnp.tile(p["g_qb"], B_HEADS), B_HEADS, QK_HEAD).reshape(1, -1)
    gkb = _pad_heads(jnp.tile(p["g_kb"], B_HEADS), B_HEADS, QK_HEAD).reshape(1, -1)
    seg64 = jnp.kron(jnp.eye(256 // HEAD_DIM, dtype=F32), jnp.ones((HEAD_DIM, HEAD_DIM), F32)).astype(BF16)
    seg128 = jnp.kron(jnp.eye(2, dtype=F32), jnp.ones((PAD_HEAD, PAD_HEAD), F32)).astype(BF16)

    def full(a):
        return pl.BlockSpec(a.shape, lambda i, b: (0,) * a.ndim)

    spare = jnp.zeros((B_HEADS, PAD_HEAD), F32).at[:, QK_HEAD].set(1.0).reshape(1, -1)
    qpad = -_score_bound(p["g_qb"], p["g_kb"], QK_HEAD) * spare
    consts = [p["g_norm1"].reshape(1, -1), wqkv, wlat, wkr, wqb, wkn, wv, gqa, gka,
              p["g_qlat"].reshape(1, -1), p["g_kvlat"].reshape(1, -1), gqb, gkb, seg64, seg128, qpad, spare]
    tok = lambda w: pl.BlockSpec((None, tm, w), lambda i, b: (b, i, 0))
    qw = B_HEADS * PAD_HEAD
    tok_shape = lambda w, dt=F32: jax.ShapeDtypeStruct((bt, s, w), dt)
    return pl.pallas_call(
        _inproj_kernel,
        grid=(s // tm, bt),
        in_specs=[tok(d), pl.BlockSpec((None, 6, d), lambda i, b: (b, 0, 0))]
        + [full(a) for a in consts]
        + [pl.BlockSpec((tm, 3 * LANES), lambda i, b: (i, 0))] * 2,
        out_specs=[tok(A_WIDTH), tok(A_WIDTH), tok(A_WIDTH),
                   pl.BlockSpec((None, qw, tm), lambda i, b: (b, 0, i)),
                   tok(qw),
                   pl.BlockSpec((None, None, B_WIDTH, tm), lambda i, b: (b, i, 0, 0))],
        out_shape=[tok_shape(A_WIDTH), tok_shape(A_WIDTH), tok_shape(A_WIDTH),
                   jax.ShapeDtypeStruct((bt, qw, s), BF16),
                   tok_shape(qw, BF16),
                   jax.ShapeDtypeStruct((bt, s // tm, B_WIDTH, tm), BF16)],
        compiler_params=_cparams("arbitrary", "arbitrary"),
        name="inproj",
    )(x, mod, *consts, ropea, ropeb)


def _rows(start, size, stride):
    return pl.ds(start, size) if stride == 1 else pl.ds(start, size, stride=stride)


def _dilated_kernel(bound_ref, q_ref, k_ref, v_ref, o_ref, stat_ref, *, bounded):
    s_len = q_ref.shape[0]
    first = lax.broadcasted_iota(I32, (TQ_A, LANES), 1) < HEAD_DIM
    for branch, (window, dil) in enumerate(DILATED_BRANCHES):
        radius = window // (2 * dil)
        win = TQ_A + 2 * radius
        length = s_len // dil
        nq = length // TQ_A
        rel = lax.broadcasted_iota(I32, (TQ_A, win), 0) - lax.broadcasted_iota(I32, (TQ_A, win), 1)

        def group(g, carry, branch=branch, dil=dil, radius=radius, win=win, length=length, nq=nq, rel=rel):
            done = []
            for u in range(UNROLL_A):
                idx = g * UNROLL_A + u
                r = idx // nq
                q0 = (idx % nq) * TQ_A
                w0 = jnp.clip(q0 - radius, 0, length - win)
                qrows = _rows(r + dil * q0, TQ_A, dil)
                krows = _rows(r + dil * w0, win, dil)
                q = q_ref[qrows, :]
                kw = k_ref[krows, :].astype(BF16)
                vw = v_ref[krows, :].astype(BF16)
                valid = jnp.abs(rel + (q0 - w0)) <= radius
                outs, stats = [], []
                for keep in (first, jnp.logical_not(first)):
                    s = _nt_dot(jnp.where(keep, q, 0.0).astype(BF16), kw)
                    if bounded:
                        pexp = jnp.exp2(jnp.where(valid, s - bound_ref[0], NEG_BIG))
                        outs.append(_dot(pexp.astype(BF16), vw))
                        stats.append(jnp.sum(pexp, axis=-1, keepdims=True))
                    else:
                        s = jnp.where(valid, s, NEG_BIG)
                        m = jnp.max(s, axis=-1, keepdims=True)
                        pexp = jnp.exp2(s - m)
                        den = jnp.sum(pexp, axis=-1, keepdims=True)
                        outs.append(_dot(pexp.astype(BF16), vw) / den)
                        stats.append(m + jnp.log2(den))
                o_new = jnp.where(first, outs[0], outs[1])
                stat_new = jnp.where(first, stats[0], stats[1])
                if branch > 0:
                    o_old, stat_old = o_ref[qrows, :], stat_ref[qrows, :]
                    if bounded:
                        o_new, stat_new = o_old + o_new, stat_old + stat_new
                    else:
                        mx = jnp.maximum(stat_old, stat_new)
                        w_old, w_new = jnp.exp2(stat_old - mx), jnp.exp2(stat_new - mx)
                        den = w_old + w_new
                        o_new = (w_old * o_old + w_new * o_new) / den
                        stat_new = mx + jnp.log2(den)
                done.append((qrows, o_new, stat_new))
            for qrows, o_new, stat_new in done:
                o_ref[qrows, :] = o_new
                stat_ref[qrows, :] = stat_new
            return carry

        lax.fori_loop(0, dil * nq // UNROLL_A, group, 0)

    if bounded:
        def normalise(i, carry):
            rows = pl.ds(pl.multiple_of(i * NORM_ROWS_A, NORM_ROWS_A), NORM_ROWS_A)
            o_ref[rows, :] = o_ref[rows, :] / stat_ref[rows, :]
            return carry

        lax.fori_loop(0, s_len // NORM_ROWS_A, normalise, 0)


def _dilated_call(bounded, bound, qa, ka, va):
    bt, s, width = qa.shape
    spec = pl.BlockSpec((None, s, LANES), lambda b, j: (b, 0, j))
    return pl.pallas_call(
        functools.partial(_dilated_kernel, bounded=bounded),
        grid=(bt, width // LANES),
        in_specs=[pl.BlockSpec(memory_space=pltpu.SMEM), spec, spec, spec],
        out_specs=spec,
        out_shape=jax.ShapeDtypeStruct((bt, s, width), F32),
        scratch_shapes=[pltpu.VMEM((s, LANES), F32)],
        compiler_params=_cparams("arbitrary", "arbitrary"),
        name="dilated_bounded" if bounded else "dilated_online",
    )(bound.reshape(1), qa, ka, va)


def _dilated_attention(qa, ka, va, p):
    bound = _score_bound(p["g_qa"], p["g_ka"], HEAD_DIM)
    return lax.cond(_bound_is_safe(bound),
                    functools.partial(_dilated_call, True), functools.partial(_dilated_call, False),
                    bound, qa, ka, va)


def _latent_online_kernel(qt_ref, k_ref, vt_ref, o_ref):
    n_kv, _, tk = vt_ref.shape
    tq = qt_ref.shape[1]
    n_chunks = tk // KC_B
    qts = [qt_ref[h * PAD_HEAD:(h + 1) * PAD_HEAD, :] for h in range(2)]

    def scores(j, h):
        k0 = pl.multiple_of(j * tk, tk)
        return tuple(_dot(k_ref[pl.ds(k0 + c * KC_B, KC_B), h * PAD_HEAD:(h + 1) * PAD_HEAD], qts[h])
                     for c in range(n_chunks))

    def body(j, carry):
        nxt = jnp.minimum(j + 1, n_kv - 1)
        new = []
        for h in range(2):
            m, l, acc, sts = carry[h]
            sts_next = scores(nxt, h)
            for c in range(n_chunks):
                m_new = jnp.maximum(m, jnp.max(sts[c], axis=0, keepdims=True))
                alpha = jnp.exp2(m - m_new)
                pexp = jnp.exp2(sts[c] - m_new)
                l = alpha * l + jnp.sum(pexp, axis=0, keepdims=True)
                vt = vt_ref[j, h * V_HEAD:(h + 1) * V_HEAD, c * KC_B:(c + 1) * KC_B]
                acc = alpha * acc + _dot(vt, pexp.astype(BF16))
                m = m_new
            new.append((m, l, acc, sts_next))
        return tuple(new)

    init = tuple((jnp.full((1, tq), NEG_BIG, F32), jnp.zeros((1, tq), F32), jnp.zeros((V_HEAD, tq), F32),
                  scores(0, h)) for h in range(2))
    (_, l0, acc0, _), (_, l1, acc1, _) = lax.fori_loop(0, n_kv, body, init)
    ot = jnp.concatenate([acc0 / l0, acc1 / l1], axis=0)
    o_ref[...] = ot.T.astype(o_ref.dtype)


def _latent_bounded_kernel(qt_ref, k_ref, vt_ref, o_ref):
    n_kv, _, tk = vt_ref.shape
    tq = qt_ref.shape[1]
    n_chunks = tk // KC_FAST_B
    qts = [qt_ref[h * PAD_HEAD:(h + 1) * PAD_HEAD, :] for h in range(2)]

    group = min(KV_TILES_B, n_kv)

    def probs(jj, h):
        out = []
        for t in range(group):
            k0 = pl.multiple_of((jj * group + t) * tk, tk)
            for c in range(n_chunks):
                kt = k_ref[pl.ds(k0 + c * KC_FAST_B, KC_FAST_B), h * PAD_HEAD:(h + 1) * PAD_HEAD]
                pexp = jnp.exp2(_dot(kt, qts[h]))
                out.append((pexp.astype(BF16), jnp.sum(pexp, axis=0, keepdims=True)))
        return tuple(out)

    def body(jj, carry):
        nxt = jnp.minimum(jj + 1, n_kv // group - 1)
        new = []
        for h in range(2):
            l, acc, cur = carry[h]
            nxt_probs = probs(nxt, h)
            for t in range(group):
                for c in range(n_chunks):
                    pexp, psum = cur[t * n_chunks + c]
                    vt = vt_ref[jj * group + t, h * V_HEAD:(h + 1) * V_HEAD, c * KC_FAST_B:(c + 1) * KC_FAST_B]
                    acc = acc + _dot(vt, pexp)
                    l = l + psum
            new.append((l, acc, nxt_probs))
        return tuple(new)

    init = tuple((jnp.zeros((1, tq), F32), jnp.zeros((V_HEAD, tq), F32), probs(0, h)) for h in range(2))
    (l0, acc0, _), (l1, acc1, _) = lax.fori_loop(0, n_kv // group, body, init)
    ot = jnp.concatenate([acc0 / l0, acc1 / l1], axis=0)
    o_ref[...] = ot.T.astype(o_ref.dtype)


def _latent_call(bounded, qbt, kb, vbt):
    bt, s, _ = kb.shape
    pairs = B_HEADS // 2
    n_kv, tk = vbt.shape[1], vbt.shape[3]
    tq = min(TQ_B, s)
    return pl.pallas_call(
        _latent_bounded_kernel if bounded else _latent_online_kernel,
        grid=(bt, pairs, s // tq),
        in_specs=[
            pl.BlockSpec((None, 2 * PAD_HEAD, tq), lambda b, hp, i: (b, hp, i)),
            pl.BlockSpec((None, s, 2 * PAD_HEAD), lambda b, hp, i: (b, 0, hp)),
            pl.BlockSpec((None, n_kv, 2 * V_HEAD, tk), lambda b, hp, i: (b, 0, hp, 0)),
        ],
        out_specs=pl.BlockSpec((None, tq, 2 * V_HEAD), lambda b, hp, i: (b, i, hp)),
        out_shape=jax.ShapeDtypeStruct((bt, s, B_WIDTH), BF16),
        compiler_params=_cparams("arbitrary", "arbitrary", "arbitrary"),
        name="latent_bounded" if bounded else "latent_online",
    )(qbt, kb, vbt)


def _latent_attention(qbt, kb, vbt, p):
    safe = _bound_is_safe(_score_bound(p["g_qb"], p["g_kb"], QK_HEAD))
    return lax.cond(safe, functools.partial(_latent_call, True), functools.partial(_latent_call, False),
                    qbt, kb, vbt)


def _outproj_kernel(oa_ref, ob_ref, x_ref, mod_ref,
                    woa_ref, wob_ref, g2_ref, wrh_ref, wrl_ref, br_ref, tri_ref,
                    x1_ref, h2_ref, idx_ref, wts_ref, rank_ref, cnt_ref, base_ref):
    first_step = jnp.logical_and(pl.program_id(0) == 0, pl.program_id(1) == 0)

    @pl.when(first_step)
    def _():
        base_ref[...] = jnp.zeros_like(base_ref)

    mixed = _dot(oa_ref[...].astype(BF16), woa_ref[...]) + _dot(ob_ref[...], wob_ref[...])
    x1 = x_ref[...] + mod_ref[2:3, :] * mixed
    x1_ref[...] = x1
    d = x1.shape[1]
    h2 = x1 * lax.rsqrt(jnp.sum(x1 * x1, axis=-1, keepdims=True) * (1.0 / d) + EPS) * g2_ref[...]
    h2 = h2 * (1.0 + mod_ref[4:5, :]) + mod_ref[3:4, :]
    _store_token_rows(h2_ref, _pack_rows(h2))

    hh, hl = _split_bf16(h2)
    logits = _nt_dot(wrh_ref[...], hh) + _nt_dot(wrh_ref[...], hl) + _nt_dot(wrl_ref[...], hh)
    scores = _sigmoid(logits)
    biased = scores + br_ref[...]
    tm = biased.shape[1]
    sub = lax.broadcasted_iota(I32, (PER_GROUP, tm), 0)
    groups, gscore = [], []
    for g in range(N_GROUPS):
        bg = biased[g * PER_GROUP:(g + 1) * PER_GROUP, :]
        m1 = jnp.max(bg, axis=0, keepdims=True)
        i1 = jnp.min(jnp.where(bg == m1, sub, PER_GROUP), axis=0, keepdims=True)
        m2 = jnp.max(jnp.where(sub == i1, -jnp.inf, bg), axis=0, keepdims=True)
        groups.append(bg)
        gscore.append(m1 + m2)
    masked = []
    for g in range(N_GROUPS):
        ahead = jnp.zeros((1, tm), I32)
        for g2 in range(N_GROUPS):
            if g2 == g:
                continue
            beats = (gscore[g2] > gscore[g]) if g2 > g else (gscore[g2] >= gscore[g])
            ahead = ahead + beats.astype(I32)
        masked.append(jnp.where(ahead < TOPK_GROUPS, groups[g], -jnp.inf))
    cand = jnp.concatenate(masked, axis=0)
    eid = lax.broadcasted_iota(I32, (N_EXPERTS, tm), 0)
    chosen = jnp.zeros((N_EXPERTS, tm), F32)
    sel_idx, sel_w, sel_hot = [], [], []
    for _ in range(TOP_K):
        mv = jnp.max(cand, axis=0, keepdims=True)
        ie = jnp.min(jnp.where(cand == mv, eid, N_EXPERTS), axis=0, keepdims=True)
        hot = eid == ie
        sel_idx.append(ie)
        sel_hot.append(hot)
        sel_w.append(jnp.sum(jnp.where(hot, scores, 0.0), axis=0, keepdims=True))
        cand = jnp.where(hot, -jnp.inf, cand)
        chosen = chosen + hot.astype(F32)
    wsum = sel_w[0]
    for w in sel_w[1:]:
        wsum = wsum + w
    within = _dot(chosen.astype(BF16), tri_ref[...]) + base_ref[...]
    ranks = [jnp.sum(jnp.where(hot, within, 0.0), axis=0, keepdims=True) for hot in sel_hot]
    pad_i = [jnp.zeros((1, tm), I32)] * (8 - TOP_K)
    pad_f = [jnp.zeros((1, tm), F32)] * (8 - TOP_K)
    idx_ref[...] = jnp.concatenate(sel_idx + pad_i, axis=0)
    wts_ref[...] = jnp.concatenate([w / wsum * ROUTED_SCALE for w in sel_w] + pad_f, axis=0)
    rank_ref[...] = jnp.concatenate([r.astype(I32) for r in ranks] + pad_i, axis=0)
    base_ref[...] = base_ref[...] + jnp.sum(chosen, axis=1, keepdims=True)
    cnt_ref[...] = jnp.broadcast_to(base_ref[...], cnt_ref.shape)


def _outproj(oa, ob, x, mod, p):
    bt, s, d = x.shape
    tm = min(TM_OUT, s)
    n = bt * s
    woa = p["w_o"][:A_WIDTH].astype(BF16)
    wob = p["w_o"][A_WIDTH:].astype(BF16)
    wr_t = p["w_router"].T
    wrh, wrl = _split_bf16(wr_t)
    tri = (jnp.arange(tm)[:, None] < jnp.arange(tm)[None, :]).astype(BF16)
    consts = [woa, wob, p["g_norm2"].reshape(1, -1), wrh, wrl, p["b_router"].reshape(-1, 1), tri]
    tiles = s // tm
    tok = lambda w: pl.BlockSpec((None, tm, w), lambda b, i: (b, i, 0))
    full = lambda a: pl.BlockSpec(a.shape, lambda b, i: (0,) * a.ndim)
    flat = lambda w: pl.BlockSpec((tm, w), lambda b, i: (b * tiles + i, 0))
    slab = pl.BlockSpec((8, tm), lambda b, i: (0, b * tiles + i))
    return pl.pallas_call(
        _outproj_kernel,
        grid=(bt, tiles),
        in_specs=[tok(A_WIDTH), tok(B_WIDTH), tok(d), pl.BlockSpec((None, 6, d), lambda b, i: (b, 0, 0))]
        + [full(a) for a in consts],
        out_specs=[flat(d), pl.BlockSpec((tm * ROW_SUB, LANES), lambda b, i: (b * tiles + i, 0)),
                   slab, slab, slab, pl.BlockSpec((N_EXPERTS, LANES), lambda b, i: (0, 0))],
        out_shape=[jax.ShapeDtypeStruct((n, d), F32), jax.ShapeDtypeStruct((n * ROW_SUB, LANES), U32),
                   jax.ShapeDtypeStruct((8, n), I32), jax.ShapeDtypeStruct((8, n), F32),
                   jax.ShapeDtypeStruct((8, n), I32), jax.ShapeDtypeStruct((N_EXPERTS, LANES), F32)],
        scratch_shapes=[pltpu.VMEM((N_EXPERTS, 1), F32)],
        compiler_params=_cparams("arbitrary", "arbitrary"),
        name="outproj_router",
    )(oa, ob, x, mod, *consts)


def _dest_kernel(pstart_ref, idx_ref, rank_ref, dest_ref):
    idx = idx_ref[...]
    dest = rank_ref[...]
    for e in range(N_EXPERTS):
        dest = dest + jnp.where(idx == e, pstart_ref[e], 0)
    dest_ref[...] = dest * ROW_SUB


def _dest(pstart, idx_t, rank_t):
    n = idx_t.shape[1]
    tn = min(2048, n)
    slab = pl.BlockSpec((8, tn), lambda i, ps: (0, i))
    return pl.pallas_call(
        _dest_kernel,
        grid_spec=pltpu.PrefetchScalarGridSpec(
            num_scalar_prefetch=1, grid=(n // tn,), in_specs=[slab, slab], out_specs=slab),
        out_shape=jax.ShapeDtypeStruct((8, n), I32),
        compiler_params=_cparams("arbitrary"),
        name="moe_dest",
    )(pstart, idx_t, rank_t)


def _dispatch_kernel(pad_start_ref, pad_len_ref, nused_ref, dest_ref, h_ref, xs_ref, zero_ref, sem, pad_sem):
    tm = h_ref.shape[0] // ROW_SUB

    def slab(ref, first_row):
        return ref.at[pl.ds(pl.multiple_of(first_row, ROW_SUB), ROW_SUB), :]

    @pl.when(pl.program_id(0) == 0)
    def _():
        zero_ref[...] = jnp.zeros_like(zero_ref)

        def pad_copy(e, r):
            return pltpu.make_async_copy(slab(zero_ref, 0), slab(xs_ref, pad_start_ref[e] + r * ROW_SUB), pad_sem)

        def per_expert(e, carry):
            lax.fori_loop(0, pad_len_ref[e], lambda r, c: (pad_copy(e, r).start(), c)[1], 0)
            lax.fori_loop(0, pad_len_ref[e], lambda r, c: (pad_copy(e, r).wait(), c)[1], 0)
            return carry

        lax.fori_loop(0, N_EXPERTS, per_expert, 0)

        def block_copy(b):
            rows = pl.ds(pl.multiple_of(b * zero_ref.shape[0], zero_ref.shape[0]), zero_ref.shape[0])
            return pltpu.make_async_copy(zero_ref, xs_ref.at[rows, :], pad_sem)

        n_blocks = xs_ref.shape[0] // zero_ref.shape[0]
        lax.fori_loop(nused_ref[0], n_blocks, lambda b, c: (block_copy(b).start(), c)[1], 0)
        lax.fori_loop(nused_ref[0], n_blocks, lambda b, c: (block_copy(b).wait(), c)[1], 0)

    def issue(j, carry):
        for k in range(TOP_K):
            pltpu.make_async_copy(slab(h_ref, j * ROW_SUB), slab(xs_ref, dest_ref[k, j]), sem).start()
        return carry

    lax.fori_loop(0, tm, issue, 0, unroll=ISSUE_UNROLL)
    for k in range(TOP_K):
        pltpu.make_async_copy(h_ref, xs_ref.at[pl.ds(0, tm * ROW_SUB), :], sem).wait()


def _dispatch(pad_start, pad_len, n_used, dest_t, hp, n_slots):
    n = hp.shape[0] // ROW_SUB
    tm = min(TM_ROW, n)
    return pl.pallas_call(
        _dispatch_kernel,
        grid_spec=pltpu.PrefetchScalarGridSpec(
            num_scalar_prefetch=3, grid=(n // tm,),
            in_specs=[
                pl.BlockSpec((8, tm), lambda i, ps, pn, nu: (0, i), memory_space=pltpu.SMEM),
                pl.BlockSpec((tm * ROW_SUB, LANES), lambda i, ps, pn, nu: (i, 0)),
            ],
            out_specs=pl.BlockSpec(memory_space=pl.ANY),
            scratch_shapes=[pltpu.VMEM((EXPERT_BLOCK * ROW_SUB, LANES), U32), pltpu.SemaphoreType.DMA(()),
                            pltpu.SemaphoreType.DMA(())]),
        out_shape=jax.ShapeDtypeStruct((n_slots * ROW_SUB, LANES), U32),
        compiler_params=_cparams("arbitrary"),
        name="moe_dispatch",
    )(pad_start, pad_len, n_used, dest_t, hp)


def _expert_kernel(be_ref, nused_ref, xs_ref, wg_ref, wu_ref, wd_ref, ys_ref):
    i = pl.program_id(0)

    @pl.when(i < nused_ref[0])
    def _():
        xb = _load_token_rows(xs_ref).astype(BF16)
        a = _dot(xb, wg_ref[...])
        u = _dot(xb, wu_ref[...])
        _store_token_rows(ys_ref, _pack_rows(_dot((_silu(a) * u).astype(BF16), wd_ref[...])))

    @pl.when(i >= nused_ref[0])
    def _():
        ys_ref[...] = jnp.zeros_like(ys_ref)


def _experts(block_e, n_used, xs, wg, wu, wd):
    n_slots = xs.shape[0] // ROW_SUB
    n_blocks = n_slots // EXPERT_BLOCK
    d, de = wg.shape[1], wg.shape[2]
    rows = pl.BlockSpec((EXPERT_BLOCK * ROW_SUB, LANES), lambda i, be, nu: (i, 0))
    used_rows = pl.BlockSpec((EXPERT_BLOCK * ROW_SUB, LANES), lambda i, be, nu: (jnp.minimum(i, nu[0] - 1), 0))
    return pl.pallas_call(
        _expert_kernel,
        grid_spec=pltpu.PrefetchScalarGridSpec(
            num_scalar_prefetch=2, grid=(n_blocks,),
            in_specs=[used_rows,
                      pl.BlockSpec((None, d, de), lambda i, be, nu: (be[i], 0, 0)),
                      pl.BlockSpec((None, d, de), lambda i, be, nu: (be[i], 0, 0)),
                      pl.BlockSpec((None, de, d), lambda i, be, nu: (be[i], 0, 0))],
            out_specs=rows),
        out_shape=jax.ShapeDtypeStruct((n_slots * ROW_SUB, LANES), U32),
        compiler_params=_cparams("arbitrary"),
        name="moe_experts",
    )(block_e, n_used, xs, wg, wu, wd)


def _combine_kernel(dest_ref, h_ref, x1_ref, mod_ref, wts_ref, wsg_ref, wsu_ref, wsd_ref, ys_ref,
                    out_ref, buf_ref, sem):
    tm = h_ref.shape[0] // ROW_SUB

    def issue(j, carry):
        for k in range(TOP_K):
            src = ys_ref.at[pl.ds(pl.multiple_of(dest_ref[k, j], ROW_SUB), ROW_SUB), :]
            dst = buf_ref.at[k, pl.ds(pl.multiple_of(j * ROW_SUB, ROW_SUB), ROW_SUB), :]
            pltpu.make_async_copy(src, dst, sem).start()
        return carry

    lax.fori_loop(0, tm, issue, 0, unroll=ISSUE_UNROLL)
    hb = _load_token_rows(h_ref).astype(BF16)
    act = _silu(_dot(hb, wsg_ref[...])) * _dot(hb, wsu_ref[...])
    y = _dot(act.astype(BF16), wsd_ref[...])
    for k in range(TOP_K):
        pltpu.make_async_copy(ys_ref.at[pl.ds(0, tm * ROW_SUB), :], buf_ref.at[k], sem).wait()
    wts = wts_ref[...]
    for k in range(TOP_K):
        y = y + wts[:, k:k + 1] * _load_token_rows(buf_ref.at[k])
    out_ref[...] = x1_ref[...] + mod_ref[5:6, :] * y


def _combine(dest_t, hp, x1, mod, wts, ys, p, bt, s):
    n, d = x1.shape
    tm = min(TM_ROW, s)
    tiles = s // tm
    wsg, wsu, wsd = (p[k].astype(BF16) for k in ("w_s_gate", "w_s_up", "w_s_down"))
    full = lambda a: pl.BlockSpec(a.shape, lambda i: (0,) * a.ndim)
    rows = pl.BlockSpec((tm, d), lambda i: (i, 0))
    out = pl.pallas_call(
        _combine_kernel,
        grid=(n // tm,),
        in_specs=[
            pl.BlockSpec((8, tm), lambda i: (0, i), memory_space=pltpu.SMEM),
            pl.BlockSpec((tm * ROW_SUB, LANES), lambda i: (i, 0)),
            rows,
            pl.BlockSpec((None, 6, d), lambda i: (i // tiles, 0, 0)),
            pl.BlockSpec((tm, 8), lambda i: (i, 0)),
            full(wsg), full(wsu), full(wsd),
            pl.BlockSpec(memory_space=pl.ANY),
        ],
        out_specs=rows,
        out_shape=jax.ShapeDtypeStruct((n, d), F32),
        scratch_shapes=[pltpu.VMEM((TOP_K, tm * ROW_SUB, LANES), U32), pltpu.SemaphoreType.DMA(())],
        compiler_params=_cparams("arbitrary"),
        name="moe_combine",
    )(dest_t, hp, x1, mod, wts, wsg, wsu, wsd, ys)
    return out.reshape(bt, s, d)


def _layer(x, c, p):
    bt, s, d = x.shape
    n = bt * s
    mod = _adaln(c, p["w_ada"], p["b_ada"]).reshape(bt, 6, d)
    qa, ka, va, qb, kb, vb = _inproj(x, mod, p)
    oa = _dilated_attention(qa, ka, va, p)
    ob = _latent_attention(qb, kb, vb, p)
    x1, h2, idx_t, wts_t, rank_t, cnt = _outproj(oa, ob, x, mod, p)

    counts = cnt[:, 0].astype(I32)
    padded = (counts + EXPERT_BLOCK - 1) // EXPERT_BLOCK * EXPERT_BLOCK
    pend = jnp.cumsum(padded)
    pstart = pend - padded
    n_blocks = (n * TOP_K + N_EXPERTS * (EXPERT_BLOCK - 1)) // EXPERT_BLOCK
    block_first = jnp.arange(n_blocks, dtype=I32) * EXPERT_BLOCK
    block_e = jnp.minimum(jnp.sum((pend[None, :] <= block_first[:, None]).astype(I32), axis=1), N_EXPERTS - 1)
    n_used = (pend[-1:] // EXPERT_BLOCK).astype(I32)

    dest_t = _dest(pstart.astype(I32), idx_t, rank_t)
    xs = _dispatch(((pstart + counts) * ROW_SUB).astype(I32), (padded - counts).astype(I32), n_used, dest_t, h2,
                   n_blocks * EXPERT_BLOCK)
    ys = _experts(block_e, n_used, xs, p["w_e_gate"].astype(BF16), p["w_e_up"].astype(BF16),
                  p["w_e_down"].astype(BF16))
    return _combine(dest_t, h2, x1, mod, wts_t.T, ys, p, bt, s)


_PARAM_NAMES = ("w_ada", "b_ada", "g_norm1", "w_in", "g_qa", "g_ka", "g_qlat", "w_qb", "g_kvlat", "w_kvb",
                "g_qb", "g_kb", "w_o", "g_norm2", "w_router", "b_router", "w_e_gate", "w_e_up", "w_e_down",
                "w_s_gate", "w_s_up", "w_s_down")


def kernel(x_prompt, x_sample, c_prompt, c_sample, w_ada, b_ada, g_norm1, w_in, g_qa, g_ka, g_qlat, w_qb,
           g_kvlat, w_kvb, g_qb, g_kb, w_o, g_norm2, w_router, b_router, w_e_gate, w_e_up, w_e_down,
           w_s_gate, w_s_up, w_s_down):
    stacked = (w_ada, b_ada, g_norm1, w_in, g_qa, g_ka, g_qlat, w_qb, g_kvlat, w_kvb, g_qb, g_kb, w_o,
               g_norm2, w_router, b_router, w_e_gate, w_e_up, w_e_down, w_s_gate, w_s_up, w_s_down)
    y_prompt, y_sample = x_prompt, x_sample
    for layer in range(w_ada.shape[0]):
        p = {name: w[layer] for name, w in zip(_PARAM_NAMES, stacked)}
        y_prompt = _layer(y_prompt, c_prompt, p)
        y_sample = _layer(y_sample, c_sample, p)
    return (y_prompt, y_sample)
```

```python
import functools

import jax
import jax.numpy as jnp
from jax import lax
from jax.experimental import pallas as pl
from jax.experimental.pallas import tpu as pltpu

F32 = jnp.float32
BF16 = jnp.bfloat16
I32 = jnp.int32
U32 = jnp.uint32

HEAD_DIM = 64
A_HEADS = 12
A_WIDTH = A_HEADS * HEAD_DIM
DILATED_BRANCHES = ((128, 1), (512, 4), (2048, 16))
ROT_DIMS_A = HEAD_DIM // 4
B_HEADS = 4
QK_NOPE = 64
QK_ROPE = 32
QK_HEAD = QK_NOPE + QK_ROPE
V_HEAD = 64
Q_LORA = 256
KV_LORA = 256
B_WIDTH = B_HEADS * V_HEAD
ROPE_THETA = 500000.0
N_EXPERTS = 64
TOP_K = 6
N_GROUPS = 8
TOPK_GROUPS = 4
PER_GROUP = N_EXPERTS // N_GROUPS
ROUTED_SCALE = 2.5
EXPERT_BLOCK = 512
EPS = 1e-6

LOG2E = 1.4426950408889634
BOUND_SLACK = 1.02
EXP2_SAFE_RANGE = 120.0

LANES = 128
PAD_HEAD = 128
NEG_BIG = -1e30
VMEM_LIMIT = 56 * 1024 * 1024

TM_IN = 512
TM_OUT = 512
TQ_A = 128
UNROLL_A = 4
NORM_ROWS_A = 512
TQ_B = 256
KC_B = 128
KC_FAST_B = 256
KV_TILES_B = 2
TM_ROW = 256
ISSUE_UNROLL = 4
ROW_SUB = 4


def _nt_dot(a, b):
    return lax.dot_general(a, b, (((1,), (1,)), ((), ())), preferred_element_type=F32)


def _dot(a, b):
    return jnp.dot(a, b, preferred_element_type=F32)


def _split_bf16(x):
    hi = x.astype(BF16)
    lo = (x - hi.astype(F32)).astype(BF16)
    return hi, lo


def _sigmoid(x):
    return 1.0 / (1.0 + jnp.exp(-x))


def _silu(x):
    return x * _sigmoid(x)


def _pack_rows(x):
    w = x.shape[1] // 2
    lo = lax.bitcast_convert_type(x[:, :w].astype(BF16).astype(F32), U32)
    hi = lax.bitcast_convert_type(x[:, w:].astype(BF16).astype(F32), U32)
    return (lo >> 16) | (hi & jnp.uint32(0xFFFF0000))


def _unpack_words(words):
    lo = lax.bitcast_convert_type(words << 16, F32)
    hi = lax.bitcast_convert_type(words & jnp.uint32(0xFFFF0000), F32)
    return lo, hi


def _store_token_rows(ref, words):
    m = ref.shape[0] // ROW_SUB
    for c in range(ROW_SUB):
        ref[pl.ds(c, m, stride=ROW_SUB), :] = words[:, c * LANES:(c + 1) * LANES]


def _load_token_rows(ref):
    m = ref.shape[0] // ROW_SUB
    halves = [_unpack_words(ref[pl.ds(c, m, stride=ROW_SUB), :]) for c in range(ROW_SUB)]
    return jnp.concatenate([lo for lo, _ in halves] + [hi for _, hi in halves], axis=1)


def _zero_unrouted_slots(pad_start_ref, pad_len_ref, nused_ref, xs_ref, zero_ref, pad_sem, n_blocks):
    zero_ref[...] = jnp.zeros_like(zero_ref)
    block_rows = zero_ref.shape[0]

    def slab(ref, first_row):
        return ref.at[pl.ds(pl.multiple_of(first_row, ROW_SUB), ROW_SUB), :]

    def pad_copy(e, r):
        return pltpu.make_async_copy(slab(zero_ref, 0), slab(xs_ref, pad_start_ref[e] + r * ROW_SUB), pad_sem)

    def per_expert(e, carry):
        lax.fori_loop(0, pad_len_ref[e], lambda r, c: (pad_copy(e, r).start(), c)[1], 0)
        lax.fori_loop(0, pad_len_ref[e], lambda r, c: (pad_copy(e, r).wait(), c)[1], 0)
        return carry

    lax.fori_loop(0, N_EXPERTS, per_expert, 0)

    def block_copy(b):
        rows = pl.ds(pl.multiple_of(b * block_rows, block_rows), block_rows)
        return pltpu.make_async_copy(zero_ref, xs_ref.at[rows, :], pad_sem)

    lax.fori_loop(nused_ref[0], n_blocks, lambda b, c: (block_copy(b).start(), c)[1], 0)
    lax.fori_loop(nused_ref[0], n_blocks, lambda b, c: (block_copy(b).wait(), c)[1], 0)


def _cparams(*sem):
    return pltpu.CompilerParams(dimension_semantics=sem, vmem_limit_bytes=VMEM_LIMIT)


def _adaln_kernel(c_ref, w_ref, b_ref, o_ref):
    s = _silu(c_ref[...])
    o_ref[...] = _dot(s.astype(BF16), w_ref[...].astype(BF16)) + b_ref[...]


def _adaln(c, w_ada, b_ada):
    bt, d = c.shape
    n_chunks = w_ada.shape[1] // d
    return pl.pallas_call(
        _adaln_kernel,
        grid=(n_chunks,),
        in_specs=[
            pl.BlockSpec((bt, d), lambda j: (0, 0)),
            pl.BlockSpec((d, d), lambda j: (0, j)),
            pl.BlockSpec((1, d), lambda j: (0, j)),
        ],
        out_specs=pl.BlockSpec((bt, d), lambda j: (0, j)),
        out_shape=jax.ShapeDtypeStruct((bt, w_ada.shape[1]), F32),
        compiler_params=_cparams("arbitrary"),
        name="adaln",
    )(c, w_ada, b_ada.reshape(1, -1))


def _seg_sumsq(x, seg_ref):
    hi, lo = _split_bf16(x * x)
    width = seg_ref.shape[0]
    outs = []
    for c in range(x.shape[1] // width):
        sl = slice(c * width, (c + 1) * width)
        outs.append(_dot(hi[:, sl], seg_ref[...]) + _dot(lo[:, sl], seg_ref[...]))
    return outs[0] if len(outs) == 1 else jnp.concatenate(outs, axis=1)


def _tile_lanes(t, reps):
    return t if reps == 1 else jnp.concatenate([t] * reps, axis=1)


def _rope_lanes(x, tab_ref, shift, reps):
    width = x.shape[1]
    cos = _tile_lanes(tab_ref[:, 0:LANES], reps)
    sin_p = _tile_lanes(tab_ref[:, LANES:2 * LANES], reps)
    sin_m = _tile_lanes(tab_ref[:, 2 * LANES:3 * LANES], reps)
    return x * cos + pltpu.roll(x, shift, 1) * sin_p + pltpu.roll(x, width - shift, 1) * sin_m


def _inproj_kernel(x_ref, mod_ref, g1_ref, wqkv_ref, wlat_ref, wkr_ref, wqb_ref, wkn_ref, wv_ref,
                   gqa_ref, gka_ref, gql_ref, gkvl_ref, gqb_ref, gkb_ref, seg64_ref, seg128_ref, qpad_ref, kpad_ref,
                   ropea_ref, ropeb_ref,
                   qa_ref, ka_ref, va_ref, qb_ref, kb_ref, vb_ref):
    x = x_ref[...]
    d = x.shape[1]
    h = x * lax.rsqrt(jnp.sum(x * x, axis=-1, keepdims=True) * (1.0 / d) + EPS) * g1_ref[...]
    h = h * (1.0 + mod_ref[1:2, :]) + mod_ref[0:1, :]
    hb = h.astype(BF16)

    z = _dot(hb, wqkv_ref[...])
    q = z[:, 0:A_WIDTH]
    k = z[:, A_WIDTH:2 * A_WIDTH]
    q = q * lax.rsqrt(_seg_sumsq(q, seg64_ref) * (1.0 / HEAD_DIM) + EPS) * gqa_ref[...]
    k = k * lax.rsqrt(_seg_sumsq(k, seg64_ref) * (1.0 / HEAD_DIM) + EPS) * gka_ref[...]
    reps_a = A_WIDTH // LANES
    q = _rope_lanes(q, ropea_ref, ROT_DIMS_A // 2, reps_a) * (HEAD_DIM ** -0.5 * LOG2E)
    k = _rope_lanes(k, ropea_ref, ROT_DIMS_A // 2, reps_a)
    qa_ref[...] = q
    ka_ref[...] = k
    va_ref[...] = z[:, 2 * A_WIDTH:3 * A_WIDTH]

    zl = _dot(hb, wlat_ref[...])
    ql = zl[:, 0:Q_LORA]
    kvl = zl[:, Q_LORA:Q_LORA + KV_LORA]
    ql = ql * lax.rsqrt(jnp.sum(ql * ql, axis=-1, keepdims=True) * (1.0 / Q_LORA) + EPS) * gql_ref[...]
    kvl = kvl * lax.rsqrt(jnp.sum(kvl * kvl, axis=-1, keepdims=True) * (1.0 / KV_LORA) + EPS) * gkvl_ref[...]
    qlb = ql.astype(BF16)
    kvlb = kvl.astype(BF16)
    reps_b = B_HEADS
    qb = _dot(qlb, wqb_ref[...])
    qb = qb * lax.rsqrt(_seg_sumsq(qb, seg128_ref) * (1.0 / QK_HEAD) + EPS) * gqb_ref[...]
    qb = _rope_lanes(qb, ropeb_ref, QK_ROPE // 2, reps_b) * (QK_HEAD ** -0.5 * LOG2E) + qpad_ref[...]
    qb_ref[...] = qb.T.astype(BF16)
    kr = _dot(hb, wkr_ref[...])
    kb = _dot(kvlb, wkn_ref[...]) + _tile_lanes(kr, reps_b)
    kb = kb * lax.rsqrt(_seg_sumsq(kb, seg128_ref) * (1.0 / QK_HEAD) + EPS) * gkb_ref[...]
    kb = _rope_lanes(kb, ropeb_ref, QK_ROPE // 2, reps_b) + kpad_ref[...]
    kb_ref[...] = kb.astype(BF16)
    vb_ref[...] = _dot(kvlb, wv_ref[...]).T.astype(BF16)


def _score_bound(g_q, g_k, head_dim):
    bound = LOG2E * head_dim ** 0.5 * jnp.max(jnp.abs(g_q)) * jnp.max(jnp.abs(g_k))
    return bound * BOUND_SLACK


def _bound_is_safe(bound):
    return 2.0 * bound < EXP2_SAFE_RANGE


def _rope_table(seq, rot, lane_of_first, period):
    half = rot // 2
    inv = ROPE_THETA ** (-jnp.arange(half, dtype=F32) * 2.0 / rot)
    ang = jnp.arange(seq).astype(F32)[:, None] * inv[None, :]
    cos, sin = jnp.cos(ang), jnp.sin(ang)
    one = jnp.ones((seq, period), F32)
    zero = jnp.zeros((seq, period), F32)
    a, b = lane_of_first, lane_of_first + half
    cos_t = one.at[:, a:a + half].set(cos).at[:, b:b + half].set(cos)
    sin_p = zero.at[:, b:b + half].set(sin)
    sin_m = zero.at[:, a:a + half].set(-sin)
    reps = LANES // period
    return jnp.concatenate([jnp.tile(t, (1, reps)) for t in (cos_t, sin_p, sin_m)], axis=1)


def _pad_heads(w, heads, width, offset=0):
    lead = w.shape[:-1]
    w = w.reshape(lead + (heads, width))
    pad = [(0, 0)] * len(lead) + [(0, 0), (offset, PAD_HEAD - width - offset)]
    return jnp.pad(w, pad).reshape(lead + (heads * PAD_HEAD,))


def _inproj(x, mod, p):
    bt, s, d = x.shape
    tm = min(TM_IN, s)
    ropea = _rope_table(s, ROT_DIMS_A, 0, HEAD_DIM)
    ropeb = _rope_table(s, QK_ROPE, QK_NOPE, PAD_HEAD)
    w_in = p["w_in"]
    wqkv = w_in[:, 0:3 * A_WIDTH].astype(BF16)
    wlat = w_in[:, 3 * A_WIDTH:3 * A_WIDTH + Q_LORA + KV_LORA].astype(BF16)
    wkr = _pad_heads(w_in[:, 3 * A_WIDTH + Q_LORA + KV_LORA:], 1, QK_ROPE, QK_NOPE).astype(BF16)
    wqb = _pad_heads(p["w_qb"], B_HEADS, QK_HEAD).astype(BF16)
    wkv = p["w_kvb"].reshape(KV_LORA, B_HEADS, QK_NOPE + V_HEAD)
    wkn = _pad_heads(wkv[:, :, :QK_NOPE].reshape(KV_LORA, B_HEADS * QK_NOPE), B_HEADS, QK_NOPE).astype(BF16)
    wv = wkv[:, :, QK_NOPE:].reshape(KV_LORA, B_WIDTH).astype(BF16)
    gqa = jnp.tile(p["g_qa"], A_HEADS).reshape(1, -1)
    gka = jnp.tile(p["g_ka"], A_HEADS).reshape(1, -1)
    gqb = _pad_heads(jnp.tile(p["g_qb"], B_HEADS), B_HEADS, QK_HEAD).reshape(1, -1)
    gkb = _pad_heads(jnp.tile(p["g_kb"], B_HEADS), B_HEADS, QK_HEAD).reshape(1, -1)
    seg64 = jnp.kron(jnp.eye(256 // HEAD_DIM, dtype=F32), jnp.ones((HEAD_DIM, HEAD_DIM), F32)).astype(BF16)
    seg128 = jnp.kron(jnp.eye(2, dtype=F32), jnp.ones((PAD_HEAD, PAD_HEAD), F32)).astype(BF16)

    def full(a):
        return pl.BlockSpec(a.shape, lambda i, b: (0,) * a.ndim)

    spare = jnp.zeros((B_HEADS, PAD_HEAD), F32).at[:, QK_HEAD].set(1.0).reshape(1, -1)
    qpad = -_score_bound(p["g_qb"], p["g_kb"], QK_HEAD) * spare
    consts = [p["g_norm1"].reshape(1, -1), wqkv, wlat, wkr, wqb, wkn, wv, gqa, gka,
              p["g_qlat"].reshape(1, -1), p["g_kvlat"].reshape(1, -1), gqb, gkb, seg64, seg128, qpad, spare]
    tok = lambda w: pl.BlockSpec((None, tm, w), lambda i, b: (b, i, 0))
    qw = B_HEADS * PAD_HEAD
    tok_shape = lambda w, dt=F32: jax.ShapeDtypeStruct((bt, s, w), dt)
    return pl.pallas_call(
        _inproj_kernel,
        grid=(s // tm, bt),
        in_specs=[tok(d), pl.BlockSpec((None, 6, d), lambda i, b: (b, 0, 0))]
        + [full(a) for a in consts]
        + [pl.BlockSpec((tm, 3 * LANES), lambda i, b: (i, 0))] * 2,
        out_specs=[tok(A_WIDTH), tok(A_WIDTH), tok(A_WIDTH),
                   pl.BlockSpec((None, qw, tm), lambda i, b: (b, 0, i)),
                   tok(qw),
                   pl.BlockSpec((None, None, B_WIDTH, tm), lambda i, b: (b, i, 0, 0))],
        out_shape=[tok_shape(A_WIDTH), tok_shape(A_WIDTH), tok_shape(A_WIDTH),
                   jax.ShapeDtypeStruct((bt, qw, s), BF16),
                   tok_shape(qw, BF16),
                   jax.ShapeDtypeStruct((bt, s // tm, B_WIDTH, tm), BF16)],
        compiler_params=_cparams("arbitrary", "arbitrary"),
        name="inproj",
    )(x, mod, *consts, ropea, ropeb)


def _rows(start, size, stride):
    return pl.ds(start, size) if stride == 1 else pl.ds(start, size, stride=stride)


def _dilated_kernel(bound_ref, q_ref, k_ref, v_ref, o_ref, stat_ref, *, bounded):
    s_len = q_ref.shape[0]
    first = lax.broadcasted_iota(I32, (TQ_A, LANES), 1) < HEAD_DIM
    for branch, (window, dil) in enumerate(DILATED_BRANCHES):
        radius = window // (2 * dil)
        win = TQ_A + 2 * radius
        length = s_len // dil
        nq = length // TQ_A
        rel = lax.broadcasted_iota(I32, (TQ_A, win), 0) - lax.broadcasted_iota(I32, (TQ_A, win), 1)

        def group(g, carry, branch=branch, dil=dil, radius=radius, win=win, length=length, nq=nq, rel=rel):
            done = []
            for u in range(UNROLL_A):
                idx = g * UNROLL_A + u
                r = idx // nq
                q0 = (idx % nq) * TQ_A
                w0 = jnp.clip(q0 - radius, 0, length - win)
                qrows = _rows(r + dil * q0, TQ_A, dil)
                krows = _rows(r + dil * w0, win, dil)
                q = q_ref[qrows, :]
                kw = k_ref[krows, :].astype(BF16)
                vw = v_ref[krows, :].astype(BF16)
                valid = jnp.abs(rel + (q0 - w0)) <= radius
                outs, stats = [], []
                for keep in (first, jnp.logical_not(first)):
                    s = _nt_dot(jnp.where(keep, q, 0.0).astype(BF16), kw)
                    if bounded:
                        pexp = jnp.exp2(jnp.where(valid, s - bound_ref[0], NEG_BIG))
                        outs.append(_dot(pexp.astype(BF16), vw))
                        stats.append(jnp.sum(pexp, axis=-1, keepdims=True))
                    else:
                        s = jnp.where(valid, s, NEG_BIG)
                        m = jnp.max(s, axis=-1, keepdims=True)
                        pexp = jnp.exp2(s - m)
                        den = jnp.sum(pexp, axis=-1, keepdims=True)
                        outs.append(_dot(pexp.astype(BF16), vw) / den)
                        stats.append(m + jnp.log2(den))
                o_new = jnp.where(first, outs[0], outs[1])
                stat_new = jnp.where(first, stats[0], stats[1])
                if branch > 0:
                    o_old, stat_old = o_ref[qrows, :], stat_ref[qrows, :]
                    if bounded:
                        o_new, stat_new = o_old + o_new, stat_old + stat_new
                    else:
                        mx = jnp.maximum(stat_old, stat_new)
                        w_old, w_new = jnp.exp2(stat_old - mx), jnp.exp2(stat_new - mx)
                        den = w_old + w_new
                        o_new = (w_old * o_old + w_new * o_new) / den
                        stat_new = mx + jnp.log2(den)
                done.append((qrows, o_new, stat_new))
            for qrows, o_new, stat_new in done:
                o_ref[qrows, :] = o_new
                stat_ref[qrows, :] = stat_new
            return carry

        lax.fori_loop(0, dil * nq // UNROLL_A, group, 0)

    if bounded:
        def normalise(i, carry):
            rows = pl.ds(pl.multiple_of(i * NORM_ROWS_A, NORM_ROWS_A), NORM_ROWS_A)
            o_ref[rows, :] = o_ref[rows, :] / stat_ref[rows, :]
            return carry

        lax.fori_loop(0, s_len // NORM_ROWS_A, normalise, 0)


def _dilated_call(bounded, bound, qa, ka, va):
    bt, s, width = qa.shape
    spec = pl.BlockSpec((None, s, LANES), lambda b, j: (b, 0, j))
    return pl.pallas_call(
        functools.partial(_dilated_kernel, bounded=bounded),
        grid=(bt, width // LANES),
        in_specs=[pl.BlockSpec(memory_space=pltpu.SMEM), spec, spec, spec],
        out_specs=spec,
        out_shape=jax.ShapeDtypeStruct((bt, s, width), F32),
        scratch_shapes=[pltpu.VMEM((s, LANES), F32)],
        compiler_params=_cparams("arbitrary", "arbitrary"),
        name="dilated_bounded" if bounded else "dilated_online",
    )(bound.reshape(1), qa, ka, va)


def _dilated_attention(qa, ka, va, p):
    bound = _score_bound(p["g_qa"], p["g_ka"], HEAD_DIM)
    return lax.cond(_bound_is_safe(bound),
                    functools.partial(_dilated_call, True), functools.partial(_dilated_call, False),
                    bound, qa, ka, va)


def _latent_online_kernel(qt_ref, k_ref, vt_ref, o_ref):
    n_kv, _, tk = vt_ref.shape
    tq = qt_ref.shape[1]
    n_chunks = tk // KC_B
    qts = [qt_ref[h * PAD_HEAD:(h + 1) * PAD_HEAD, :] for h in range(2)]

    def scores(j, h):
        k0 = pl.multiple_of(j * tk, tk)
        return tuple(_dot(k_ref[pl.ds(k0 + c * KC_B, KC_B), h * PAD_HEAD:(h + 1) * PAD_HEAD], qts[h])
                     for c in range(n_chunks))

    def body(j, carry):
        nxt = jnp.minimum(j + 1, n_kv - 1)
        new = []
        for h in range(2):
            m, l, acc, sts = carry[h]
            sts_next = scores(nxt, h)
            for c in range(n_chunks):
                m_new = jnp.maximum(m, jnp.max(sts[c], axis=0, keepdims=True))
                alpha = jnp.exp2(m - m_new)
                pexp = jnp.exp2(sts[c] - m_new)
                l = alpha * l + jnp.sum(pexp, axis=0, keepdims=True)
                vt = vt_ref[j, h * V_HEAD:(h + 1) * V_HEAD, c * KC_B:(c + 1) * KC_B]
                acc = alpha * acc + _dot(vt, pexp.astype(BF16))
                m = m_new
            new.append((m, l, acc, sts_next))
        return tuple(new)

    init = tuple((jnp.full((1, tq), NEG_BIG, F32), jnp.zeros((1, tq), F32), jnp.zeros((V_HEAD, tq), F32),
                  scores(0, h)) for h in range(2))
    (_, l0, acc0, _), (_, l1, acc1, _) = lax.fori_loop(0, n_kv, body, init)
    ot = jnp.concatenate([acc0 / l0, acc1 / l1], axis=0)
    o_ref[...] = ot.T.astype(o_ref.dtype)


def _latent_bounded_kernel(qt_ref, k_ref, vt_ref, o_ref):
    n_kv, _, tk = vt_ref.shape
    tq = qt_ref.shape[1]
    n_chunks = tk // KC_FAST_B
    qts = [qt_ref[h * PAD_HEAD:(h + 1) * PAD_HEAD, :] for h in range(2)]
    group = min(KV_TILES_B, n_kv)

    def probs(jj, h):
        out = []
        for t in range(group):
            k0 = pl.multiple_of((jj * group + t) * tk, tk)
            for c in range(n_chunks):
                kt = k_ref[pl.ds(k0 + c * KC_FAST_B, KC_FAST_B), h * PAD_HEAD:(h + 1) * PAD_HEAD]
                pexp = jnp.exp2(_dot(kt, qts[h]))
                out.append((pexp.astype(BF16), jnp.sum(pexp, axis=0, keepdims=True)))
        return tuple(out)

    def body(jj, carry):
        nxt = jnp.minimum(jj + 1, n_kv // group - 1)
        new = []
        for h in range(2):
            l, acc, cur = carry[h]
            nxt_probs = probs(nxt, h)
            for t in range(group):
                for c in range(n_chunks):
                    pexp, psum = cur[t * n_chunks + c]
                    vt = vt_ref[jj * group + t, h * V_HEAD:(h + 1) * V_HEAD, c * KC_FAST_B:(c + 1) * KC_FAST_B]
                    acc = acc + _dot(vt, pexp)
                    l = l + psum
            new.append((l, acc, nxt_probs))
        return tuple(new)

    init = tuple((jnp.zeros((1, tq), F32), jnp.zeros((V_HEAD, tq), F32), probs(0, h)) for h in range(2))
    (l0, acc0, _), (l1, acc1, _) = lax.fori_loop(0, n_kv // group, body, init)
    ot = jnp.concatenate([acc0 / l0, acc1 / l1], axis=0)
    o_ref[...] = ot.T.astype(o_ref.dtype)


def _latent_call(bounded, qbt, kb, vbt):
    bt, s, _ = kb.shape
    pairs = B_HEADS // 2
    n_kv, tk = vbt.shape[1], vbt.shape[3]
    tq = min(TQ_B, s)
    return pl.pallas_call(
        _latent_bounded_kernel if bounded else _latent_online_kernel,
        grid=(bt, pairs, s // tq),
        in_specs=[
            pl.BlockSpec((None, 2 * PAD_HEAD, tq), lambda b, hp, i: (b, hp, i)),
            pl.BlockSpec((None, s, 2 * PAD_HEAD), lambda b, hp, i: (b, 0, hp)),
            pl.BlockSpec((None, n_kv, 2 * V_HEAD, tk), lambda b, hp, i: (b, 0, hp, 0)),
        ],
        out_specs=pl.BlockSpec((None, tq, 2 * V_HEAD), lambda b, hp, i: (b, i, hp)),
        out_shape=jax.ShapeDtypeStruct((bt, s, B_WIDTH), BF16),
        compiler_params=_cparams("arbitrary", "arbitrary", "arbitrary"),
        name="latent_bounded" if bounded else "latent_online",
    )(qbt, kb, vbt)


def _latent_attention(qbt, kb, vbt, p):
    safe = _bound_is_safe(_score_bound(p["g_qb"], p["g_kb"], QK_HEAD))
    return lax.cond(safe, functools.partial(_latent_call, True), functools.partial(_latent_call, False),
                    qbt, kb, vbt)


def _outproj_kernel(oa_ref, ob_ref, x_ref, mod_ref,
                    woa_ref, wob_ref, g2_ref, wrh_ref, wrl_ref, br_ref, tri_ref,
                    x1_ref, h2_ref, idx_ref, wts_ref, rank_ref, cnt_ref, base_ref):
    first_step = jnp.logical_and(pl.program_id(0) == 0, pl.program_id(1) == 0)

    @pl.when(first_step)
    def _():
        base_ref[...] = jnp.zeros_like(base_ref)

    mixed = _dot(oa_ref[...].astype(BF16), woa_ref[...]) + _dot(ob_ref[...], wob_ref[...])
    x1 = x_ref[...] + mod_ref[2:3, :] * mixed
    x1_ref[...] = x1
    d = x1.shape[1]
    h2 = x1 * lax.rsqrt(jnp.sum(x1 * x1, axis=-1, keepdims=True) * (1.0 / d) + EPS) * g2_ref[...]
    h2 = h2 * (1.0 + mod_ref[4:5, :]) + mod_ref[3:4, :]
    _store_token_rows(h2_ref, _pack_rows(h2))

    hh, hl = _split_bf16(h2)
    logits = _nt_dot(wrh_ref[...], hh) + _nt_dot(wrh_ref[...], hl) + _nt_dot(wrl_ref[...], hh)
    scores = _sigmoid(logits)
    biased = scores + br_ref[...]
    tm = biased.shape[1]
    sub = lax.broadcasted_iota(I32, (PER_GROUP, tm), 0)
    groups, gscore = [], []
    for g in range(N_GROUPS):
        bg = biased[g * PER_GROUP:(g + 1) * PER_GROUP, :]
        m1 = jnp.max(bg, axis=0, keepdims=True)
        i1 = jnp.min(jnp.where(bg == m1, sub, PER_GROUP), axis=0, keepdims=True)
        m2 = jnp.max(jnp.where(sub == i1, -jnp.inf, bg), axis=0, keepdims=True)
        groups.append(bg)
        gscore.append(m1 + m2)
    masked = []
    for g in range(N_GROUPS):
        ahead = jnp.zeros((1, tm), I32)
        for g2 in range(N_GROUPS):
            if g2 == g:
                continue
            beats = (gscore[g2] > gscore[g]) if g2 > g else (gscore[g2] >= gscore[g])
            ahead = ahead + beats.astype(I32)
        masked.append(jnp.where(ahead < TOPK_GROUPS, groups[g], -jnp.inf))
    cand = jnp.concatenate(masked, axis=0)
    eid = lax.broadcasted_iota(I32, (N_EXPERTS, tm), 0)
    chosen = jnp.zeros((N_EXPERTS, tm), F32)
    sel_idx, sel_w, sel_hot = [], [], []
    for _ in range(TOP_K):
        mv = jnp.max(cand, axis=0, keepdims=True)
        ie = jnp.min(jnp.where(cand == mv, eid, N_EXPERTS), axis=0, keepdims=True)
        hot = eid == ie
        sel_idx.append(ie)
        sel_hot.append(hot)
        sel_w.append(jnp.sum(jnp.where(hot, scores, 0.0), axis=0, keepdims=True))
        cand = jnp.where(hot, -jnp.inf, cand)
        chosen = chosen + hot.astype(F32)
    wsum = sel_w[0]
    for w in sel_w[1:]:
        wsum = wsum + w
    within = _dot(chosen.astype(BF16), tri_ref[...]) + base_ref[...]
    ranks = [jnp.sum(jnp.where(hot, within, 0.0), axis=0, keepdims=True) for hot in sel_hot]
    pad_i = [jnp.zeros((1, tm), I32)] * (8 - TOP_K)
    pad_f = [jnp.zeros((1, tm), F32)] * (8 - TOP_K)
    idx_ref[...] = jnp.concatenate(sel_idx + pad_i, axis=0)
    wts_ref[...] = jnp.concatenate([w / wsum * ROUTED_SCALE for w in sel_w] + pad_f, axis=0)
    rank_ref[...] = jnp.concatenate([r.astype(I32) for r in ranks] + pad_i, axis=0)
    base_ref[...] = base_ref[...] + jnp.sum(chosen, axis=1, keepdims=True)
    cnt_ref[...] = jnp.broadcast_to(base_ref[...], cnt_ref.shape)


def _outproj(oa, ob, x, mod, p):
    bt, s, d = x.shape
    tm = min(TM_OUT, s)
    n = bt * s
    woa = p["w_o"][:A_WIDTH].astype(BF16)
    wob = p["w_o"][A_WIDTH:].astype(BF16)
    wr_t = p["w_router"].T
    wrh, wrl = _split_bf16(wr_t)
    tri = (jnp.arange(tm)[:, None] < jnp.arange(tm)[None, :]).astype(BF16)
    consts = [woa, wob, p["g_norm2"].reshape(1, -1), wrh, wrl, p["b_router"].reshape(-1, 1), tri]
    tiles = s // tm
    tok = lambda w: pl.BlockSpec((None, tm, w), lambda b, i: (b, i, 0))
    full = lambda a: pl.BlockSpec(a.shape, lambda b, i: (0,) * a.ndim)
    flat = lambda w: pl.BlockSpec((tm, w), lambda b, i: (b * tiles + i, 0))
    slab = pl.BlockSpec((8, tm), lambda b, i: (0, b * tiles + i))
    return pl.pallas_call(
        _outproj_kernel,
        grid=(bt, tiles),
        in_specs=[tok(A_WIDTH), tok(B_WIDTH), tok(d), pl.BlockSpec((None, 6, d), lambda b, i: (b, 0, 0))]
        + [full(a) for a in consts],
        out_specs=[flat(d), pl.BlockSpec((tm * ROW_SUB, LANES), lambda b, i: (b * tiles + i, 0)),
                   slab, slab, slab, pl.BlockSpec((N_EXPERTS, LANES), lambda b, i: (0, 0))],
        out_shape=[jax.ShapeDtypeStruct((n, d), F32), jax.ShapeDtypeStruct((n * ROW_SUB, LANES), U32),
                   jax.ShapeDtypeStruct((8, n), I32), jax.ShapeDtypeStruct((8, n), F32),
                   jax.ShapeDtypeStruct((8, n), I32), jax.ShapeDtypeStruct((N_EXPERTS, LANES), F32)],
        scratch_shapes=[pltpu.VMEM((N_EXPERTS, 1), F32)],
        compiler_params=_cparams("arbitrary", "arbitrary"),
        name="outproj_router",
    )(oa, ob, x, mod, *consts)


def _dest_kernel(pstart_ref, idx_ref, rank_ref, dest_ref):
    idx = idx_ref[...]
    dest = rank_ref[...]
    for e in range(N_EXPERTS):
        dest = dest + jnp.where(idx == e, pstart_ref[e], 0)
    dest_ref[...] = dest * ROW_SUB


def _dest(pstart, idx_t, rank_t):
    n = idx_t.shape[1]
    tn = min(2048, n)
    slab = pl.BlockSpec((8, tn), lambda i, ps: (0, i))
    return pl.pallas_call(
        _dest_kernel,
        grid_spec=pltpu.PrefetchScalarGridSpec(
            num_scalar_prefetch=1, grid=(n // tn,), in_specs=[slab, slab], out_specs=slab),
        out_shape=jax.ShapeDtypeStruct((8, n), I32),
        compiler_params=_cparams("arbitrary"),
        name="moe_dest",
    )(pstart, idx_t, rank_t)


def _dispatch_kernel(pad_start_ref, pad_len_ref, nused_ref, dest_ref, h_ref, xs_ref, zero_ref, sem, pad_sem):
    tm = h_ref.shape[0] // ROW_SUB

    @pl.when(pl.program_id(0) == 0)
    def _():
        _zero_unrouted_slots(pad_start_ref, pad_len_ref, nused_ref, xs_ref, zero_ref, pad_sem,
                             xs_ref.shape[0] // zero_ref.shape[0])

    def slab(ref, first_row):
        return ref.at[pl.ds(pl.multiple_of(first_row, ROW_SUB), ROW_SUB), :]

    def issue(j, carry):
        for k in range(TOP_K):
            pltpu.make_async_copy(slab(h_ref, j * ROW_SUB), slab(xs_ref, dest_ref[k, j]), sem).start()
        return carry

    lax.fori_loop(0, tm, issue, 0, unroll=ISSUE_UNROLL)
    for k in range(TOP_K):
        pltpu.make_async_copy(h_ref, xs_ref.at[pl.ds(0, tm * ROW_SUB), :], sem).wait()


def _dispatch(pad_start, pad_len, n_used, dest_t, hp, n_slots):
    n = hp.shape[0] // ROW_SUB
    tm = min(TM_ROW, n)
    return pl.pallas_call(
        _dispatch_kernel,
        grid_spec=pltpu.PrefetchScalarGridSpec(
            num_scalar_prefetch=3, grid=(n // tm,),
            in_specs=[
                pl.BlockSpec((8, tm), lambda i, ps, pn, nu: (0, i), memory_space=pltpu.SMEM),
                pl.BlockSpec((tm * ROW_SUB, LANES), lambda i, ps, pn, nu: (i, 0)),
            ],
            out_specs=pl.BlockSpec(memory_space=pl.ANY),
            scratch_shapes=[pltpu.VMEM((EXPERT_BLOCK * ROW_SUB, LANES), U32), pltpu.SemaphoreType.DMA(()),
                            pltpu.SemaphoreType.DMA(())]),
        out_shape=jax.ShapeDtypeStruct((n_slots * ROW_SUB, LANES), U32),
        compiler_params=_cparams("arbitrary"),
        name="moe_dispatch",
    )(pad_start, pad_len, n_used, dest_t, hp)


def _expert_kernel(be_ref, nused_ref, xs_ref, wg_ref, wu_ref, wd_ref, ys_ref):
    i = pl.program_id(0)

    @pl.when(i < nused_ref[0])
    def _():
        xb = _load_token_rows(xs_ref).astype(BF16)
        a = _dot(xb, wg_ref[...])
        u = _dot(xb, wu_ref[...])
        _store_token_rows(ys_ref, _pack_rows(_dot((_silu(a) * u).astype(BF16), wd_ref[...])))

    @pl.when(i >= nused_ref[0])
    def _():
        ys_ref[...] = jnp.zeros_like(ys_ref)


def _experts(block_e, n_used, xs, wg, wu, wd):
    n_slots = xs.shape[0] // ROW_SUB
    n_blocks = n_slots // EXPERT_BLOCK
    d, de = wg.shape[1], wg.shape[2]
    rows = pl.BlockSpec((EXPERT_BLOCK * ROW_SUB, LANES), lambda i, be, nu: (i, 0))
    used_rows = pl.BlockSpec((EXPERT_BLOCK * ROW_SUB, LANES), lambda i, be, nu: (jnp.minimum(i, nu[0] - 1), 0))
    return pl.pallas_call(
        _expert_kernel,
        grid_spec=pltpu.PrefetchScalarGridSpec(
            num_scalar_prefetch=2, grid=(n_blocks,),
            in_specs=[used_rows,
                      pl.BlockSpec((None, d, de), lambda i, be, nu: (be[i], 0, 0)),
                      pl.BlockSpec((None, d, de), lambda i, be, nu: (be[i], 0, 0)),
                      pl.BlockSpec((None, de, d), lambda i, be, nu: (be[i], 0, 0))],
            out_specs=rows),
        out_shape=jax.ShapeDtypeStruct((n_slots * ROW_SUB, LANES), U32),
        compiler_params=_cparams("arbitrary"),
        name="moe_experts",
    )(block_e, n_used, xs, wg, wu, wd)


def _combine_kernel(dest_ref, dest_next_ref, h_ref, x1_ref, mod_ref, wts_ref, wsg_ref, wsu_ref, wsd_ref, ys_ref,
                    out_ref, buf_ref, sems):
    i = pl.program_id(0)
    slot = lax.rem(i, 2)
    tm = h_ref.shape[0] // ROW_SUB

    def gather(idx_ref, into):
        def issue(j, carry):
            for k in range(TOP_K):
                src = ys_ref.at[pl.ds(pl.multiple_of(idx_ref[k, j], ROW_SUB), ROW_SUB), :]
                dst = buf_ref.at[into, k, pl.ds(pl.multiple_of(j * ROW_SUB, ROW_SUB), ROW_SUB), :]
                pltpu.make_async_copy(src, dst, sems.at[into]).start()
            return carry

        lax.fori_loop(0, tm, issue, 0, unroll=ISSUE_UNROLL)

    @pl.when(i == 0)
    def _():
        gather(dest_ref, 0)

    @pl.when(i + 1 < pl.num_programs(0))
    def _():
        gather(dest_next_ref, 1 - slot)

    hb = _load_token_rows(h_ref).astype(BF16)
    act = _silu(_dot(hb, wsg_ref[...])) * _dot(hb, wsu_ref[...])
    y = _dot(act.astype(BF16), wsd_ref[...])
    for k in range(TOP_K):
        pltpu.make_async_copy(ys_ref.at[pl.ds(0, tm * ROW_SUB), :], buf_ref.at[slot, k], sems.at[slot]).wait()
    wts = wts_ref[...]
    for k in range(TOP_K):
        y = y + wts[:, k:k + 1] * _load_token_rows(buf_ref.at[slot, k])
    out_ref[...] = x1_ref[...] + mod_ref[5:6, :] * y


def _combine(dest_t, hp, x1, mod, wts, ys, p, bt, s):
    n, d = x1.shape
    tm = min(TM_ROW, s)
    tiles = s // tm
    steps = n // tm
    wsg, wsu, wsd = (p[k].astype(BF16) for k in ("w_s_gate", "w_s_up", "w_s_down"))
    full = lambda a: pl.BlockSpec(a.shape, lambda i: (0,) * a.ndim)
    rows = pl.BlockSpec((tm, d), lambda i: (i, 0))
    out = pl.pallas_call(
        _combine_kernel,
        grid=(steps,),
        in_specs=[
            pl.BlockSpec((8, tm), lambda i: (0, i), memory_space=pltpu.SMEM),
            pl.BlockSpec((8, tm), lambda i: (0, jnp.minimum(i + 1, steps - 1)), memory_space=pltpu.SMEM),
            pl.BlockSpec((tm * ROW_SUB, LANES), lambda i: (i, 0)),
            rows,
            pl.BlockSpec((None, 6, d), lambda i: (i // tiles, 0, 0)),
            pl.BlockSpec((tm, 8), lambda i: (i, 0)),
            full(wsg), full(wsu), full(wsd),
            pl.BlockSpec(memory_space=pl.ANY),
        ],
        out_specs=rows,
        out_shape=jax.ShapeDtypeStruct((n, d), F32),
        scratch_shapes=[pltpu.VMEM((2, TOP_K, tm * ROW_SUB, LANES), U32), pltpu.SemaphoreType.DMA((2,))],
        compiler_params=_cparams("arbitrary"),
        name="moe_combine",
    )(dest_t, dest_t, hp, x1, mod, wts, wsg, wsu, wsd, ys)
    return out.reshape(bt, s, d)


def _layer(x, c, p):
    bt, s, d = x.shape
    n = bt * s
    mod = _adaln(c, p["w_ada"], p["b_ada"]).reshape(bt, 6, d)
    qa, ka, va, qb, kb, vb = _inproj(x, mod, p)
    oa = _dilated_attention(qa, ka, va, p)
    ob = _latent_attention(qb, kb, vb, p)
    x1, hp, idx_t, wts_t, rank_t, cnt = _outproj(oa, ob, x, mod, p)

    counts = cnt[:, 0].astype(I32)
    padded = (counts + EXPERT_BLOCK - 1) // EXPERT_BLOCK * EXPERT_BLOCK
    pend = jnp.cumsum(padded)
    pstart = pend - padded
    n_blocks = (n * TOP_K + N_EXPERTS * (EXPERT_BLOCK - 1)) // EXPERT_BLOCK
    block_first = jnp.arange(n_blocks, dtype=I32) * EXPERT_BLOCK
    block_e = jnp.minimum(jnp.sum((pend[None, :] <= block_first[:, None]).astype(I32), axis=1), N_EXPERTS - 1)
    n_used = (pend[-1:] // EXPERT_BLOCK).astype(I32)

    dest_t = _dest(pstart.astype(I32), idx_t, rank_t)
    xs = _dispatch(((pstart + counts) * ROW_SUB).astype(I32), (padded - counts).astype(I32), n_used, dest_t, hp,
                   n_blocks * EXPERT_BLOCK)
    ys = _experts(block_e, n_used, xs, p["w_e_gate"].astype(BF16), p["w_e_up"].astype(BF16),
                  p["w_e_down"].astype(BF16))
    return _combine(dest_t, hp, x1, mod, wts_t.T, ys, p, bt, s)


_PARAM_NAMES = ("w_ada", "b_ada", "g_norm1", "w_in", "g_qa", "g_ka", "g_qlat", "w_qb", "g_kvlat", "w_kvb",
                "g_qb", "g_kb", "w_o", "g_norm2", "w_router", "b_router", "w_e_gate", "w_e_up", "w_e_down",
                "w_s_gate", "w_s_up", "w_s_down")


def kernel(x_prompt, x_sample, c_prompt, c_sample, w_ada, b_ada, g_norm1, w_in, g_qa, g_ka, g_qlat, w_qb,
           g_kvlat, w_kvb, g_qb, g_kb, w_o, g_norm2, w_router, b_router, w_e_gate, w_e_up, w_e_down,
           w_s_gate, w_s_up, w_s_down):
    stacked = (w_ada, b_ada, g_norm1, w_in, g_qa, g_ka, g_qlat, w_qb, g_kvlat, w_kvb, g_qb, g_kb, w_o,
               g_norm2, w_router, b_router, w_e_gate, w_e_up, w_e_down, w_s_gate, w_s_up, w_s_down)
    y_prompt, y_sample = x_prompt, x_sample
    for layer in range(w_ada.shape[0]):
        p = {name: w[layer] for name, w in zip(_PARAM_NAMES, stacked)}
        y_prompt = _layer(y_prompt, c_prompt, p)
        y_sample = _layer(y_sample, c_sample, p)
    return (y_prompt, y_sample)
```

```python
import functools

import jax
import jax.numpy as jnp
from jax import lax
from jax.experimental import pallas as pl
from jax.experimental.pallas import tpu as pltpu

F32 = jnp.float32
BF16 = jnp.bfloat16
I32 = jnp.int32
U32 = jnp.uint32

HEAD_DIM = 64
A_HEADS = 12
A_WIDTH = A_HEADS * HEAD_DIM
DILATED_BRANCHES = ((128, 1), (512, 4), (2048, 16))
ROT_DIMS_A = HEAD_DIM // 4
B_HEADS = 4
QK_NOPE = 64
QK_ROPE = 32
QK_HEAD = QK_NOPE + QK_ROPE
V_HEAD = 64
Q_LORA = 256
KV_LORA = 256
B_WIDTH = B_HEADS * V_HEAD
ROPE_THETA = 500000.0
N_EXPERTS = 64
TOP_K = 6
N_GROUPS = 8
TOPK_GROUPS = 4
PER_GROUP = N_EXPERTS // N_GROUPS
ROUTED_SCALE = 2.5
EXPERT_BLOCK = 512
EPS = 1e-6

LOG2E = 1.4426950408889634
BOUND_SLACK = 1.02
EXP2_SAFE_RANGE = 120.0

LANES = 128
PAD_HEAD = 128
NEG_BIG = -1e30
VMEM_LIMIT = 56 * 1024 * 1024

TM_IN = 512
TM_OUT = 512
TQ_A = 128
UNROLL_A = 8
NORM_ROWS_A = 512
TQ_B = 256
KC_B = 128
KC_FAST_B = 256
KV_TILES_B = 2
TM_ROW = 512
ISSUE_UNROLL = 4
ROW_SUB = 4


def _nt_dot(a, b):
    return lax.dot_general(a, b, (((1,), (1,)), ((), ())), preferred_element_type=F32)


def _dot(a, b):
    return jnp.dot(a, b, preferred_element_type=F32)


def _split_bf16(x):
    hi = x.astype(BF16)
    lo = (x - hi.astype(F32)).astype(BF16)
    return hi, lo


def _sigmoid(x):
    return 1.0 / (1.0 + jnp.exp(-x))


def _silu(x):
    return x * _sigmoid(x)


def _pack_rows(x):
    w = x.shape[1] // 2
    lo = lax.bitcast_convert_type(x[:, :w].astype(BF16).astype(F32), U32)
    hi = lax.bitcast_convert_type(x[:, w:].astype(BF16).astype(F32), U32)
    return (lo >> 16) | (hi & jnp.uint32(0xFFFF0000))


def _unpack_words(words):
    lo = lax.bitcast_convert_type(words << 16, F32)
    hi = lax.bitcast_convert_type(words & jnp.uint32(0xFFFF0000), F32)
    return lo, hi


def _store_token_rows(ref, words):
    m = ref.shape[0] // ROW_SUB
    for c in range(ROW_SUB):
        ref[pl.ds(c, m, stride=ROW_SUB), :] = words[:, c * LANES:(c + 1) * LANES]


def _load_token_rows(ref):
    m = ref.shape[0] // ROW_SUB
    halves = [_unpack_words(ref[pl.ds(c, m, stride=ROW_SUB), :]) for c in range(ROW_SUB)]
    return jnp.concatenate([lo for lo, _ in halves] + [hi for _, hi in halves], axis=1)


def _zero_unrouted_slots(pad_start_ref, pad_len_ref, nused_ref, xs_ref, zero_ref, pad_sem, n_blocks):
    zero_ref[...] = jnp.zeros_like(zero_ref)
    block_rows = zero_ref.shape[0]

    def slab(ref, first_row):
        return ref.at[pl.ds(pl.multiple_of(first_row, ROW_SUB), ROW_SUB), :]

    def pad_copy(e, r):
        return pltpu.make_async_copy(slab(zero_ref, 0), slab(xs_ref, pad_start_ref[e] + r * ROW_SUB), pad_sem)

    def per_expert(e, carry):
        lax.fori_loop(0, pad_len_ref[e], lambda r, c: (pad_copy(e, r).start(), c)[1], 0)
        lax.fori_loop(0, pad_len_ref[e], lambda r, c: (pad_copy(e, r).wait(), c)[1], 0)
        return carry

    lax.fori_loop(0, N_EXPERTS, per_expert, 0)

    def block_copy(b):
        rows = pl.ds(pl.multiple_of(b * block_rows, block_rows), block_rows)
        return pltpu.make_async_copy(zero_ref, xs_ref.at[rows, :], pad_sem)

    lax.fori_loop(nused_ref[0], n_blocks, lambda b, c: (block_copy(b).start(), c)[1], 0)
    lax.fori_loop(nused_ref[0], n_blocks, lambda b, c: (block_copy(b).wait(), c)[1], 0)


def _cparams(*sem):
    return pltpu.CompilerParams(dimension_semantics=sem, vmem_limit_bytes=VMEM_LIMIT)


def _adaln_kernel(c_ref, w_ref, b_ref, o_ref):
    s = _silu(c_ref[...])
    o_ref[...] = _dot(s.astype(BF16), w_ref[...].astype(BF16)) + b_ref[...]


def _adaln(c, w_ada, b_ada):
    bt, d = c.shape
    n_chunks = w_ada.shape[1] // d
    return pl.pallas_call(
        _adaln_kernel,
        grid=(n_chunks,),
        in_specs=[
            pl.BlockSpec((bt, d), lambda j: (0, 0)),
            pl.BlockSpec((d, d), lambda j: (0, j)),
            pl.BlockSpec((1, d), lambda j: (0, j)),
        ],
        out_specs=pl.BlockSpec((bt, d), lambda j: (0, j)),
        out_shape=jax.ShapeDtypeStruct((bt, w_ada.shape[1]), F32),
        compiler_params=_cparams("arbitrary"),
        name="adaln",
    )(c, w_ada, b_ada.reshape(1, -1))


def _seg_sumsq(x, seg_ref):
    sq = (x * x).astype(BF16)
    width = seg_ref.shape[0]
    outs = [_dot(sq[:, c * width:(c + 1) * width], seg_ref[...]) for c in range(x.shape[1] // width)]
    return outs[0] if len(outs) == 1 else jnp.concatenate(outs, axis=1)


def _tile_lanes(t, reps):
    return t if reps == 1 else jnp.concatenate([t] * reps, axis=1)


def _rope_lanes(x, tab_ref, shift, reps):
    width = x.shape[1]
    cos = _tile_lanes(tab_ref[:, 0:LANES], reps)
    sin_p = _tile_lanes(tab_ref[:, LANES:2 * LANES], reps)
    sin_m = _tile_lanes(tab_ref[:, 2 * LANES:3 * LANES], reps)
    return x * cos + pltpu.roll(x, shift, 1) * sin_p + pltpu.roll(x, width - shift, 1) * sin_m


def _inproj_kernel(x_ref, mod_ref, g1_ref, wqkv_ref, wlat_ref, wkr_ref, wqb_ref, wkn_ref, wv_ref,
                   gqa_ref, gka_ref, gql_ref, gkvl_ref, gqb_ref, gkb_ref, seg64_ref, seg128_ref, qpad_ref, kpad_ref,
                   ropea_ref, ropeb_ref,
                   qa_ref, ka_ref, va_ref, qb_ref, kb_ref, vb_ref):
    x = x_ref[...]
    d = x.shape[1]
    h = x * lax.rsqrt(jnp.sum(x * x, axis=-1, keepdims=True) * (1.0 / d) + EPS) * g1_ref[...]
    h = h * (1.0 + mod_ref[1:2, :]) + mod_ref[0:1, :]
    hb = h.astype(BF16)

    z = _dot(hb, wqkv_ref[...])
    q = z[:, 0:A_WIDTH]
    k = z[:, A_WIDTH:2 * A_WIDTH]
    q = q * lax.rsqrt(_seg_sumsq(q, seg64_ref) * (1.0 / HEAD_DIM) + EPS) * gqa_ref[...]
    k = k * lax.rsqrt(_seg_sumsq(k, seg64_ref) * (1.0 / HEAD_DIM) + EPS) * gka_ref[...]
    reps_a = A_WIDTH // LANES
    q = _rope_lanes(q, ropea_ref, ROT_DIMS_A // 2, reps_a) * (HEAD_DIM ** -0.5 * LOG2E)
    k = _rope_lanes(k, ropea_ref, ROT_DIMS_A // 2, reps_a)
    qa_ref[...] = q
    ka_ref[...] = k
    va_ref[...] = z[:, 2 * A_WIDTH:3 * A_WIDTH]

    zl = _dot(hb, wlat_ref[...])
    ql = zl[:, 0:Q_LORA]
    kvl = zl[:, Q_LORA:Q_LORA + KV_LORA]
    ql = ql * lax.rsqrt(jnp.sum(ql * ql, axis=-1, keepdims=True) * (1.0 / Q_LORA) + EPS) * gql_ref[...]
    kvl = kvl * lax.rsqrt(jnp.sum(kvl * kvl, axis=-1, keepdims=True) * (1.0 / KV_LORA) + EPS) * gkvl_ref[...]
    qlb = ql.astype(BF16)
    kvlb = kvl.astype(BF16)
    reps_b = B_HEADS
    qb = _dot(qlb, wqb_ref[...])
    qb = qb * lax.rsqrt(_seg_sumsq(qb, seg128_ref) * (1.0 / QK_HEAD) + EPS) * gqb_ref[...]
    qb = _rope_lanes(qb, ropeb_ref, QK_ROPE // 2, reps_b) * (QK_HEAD ** -0.5 * LOG2E) + qpad_ref[...]
    qb_ref[...] = qb.T.astype(BF16)
    kr = _dot(hb, wkr_ref[...])
    kb = _dot(kvlb, wkn_ref[...]) + _tile_lanes(kr, reps_b)
    kb = kb * lax.rsqrt(_seg_sumsq(kb, seg128_ref) * (1.0 / QK_HEAD) + EPS) * gkb_ref[...]
    kb = _rope_lanes(kb, ropeb_ref, QK_ROPE // 2, reps_b) + kpad_ref[...]
    kb_ref[...] = kb.astype(BF16)
    vb_ref[...] = _dot(kvlb, wv_ref[...]).T.astype(BF16)


def _score_bound(g_q, g_k, head_dim):
    bound = LOG2E * head_dim ** 0.5 * jnp.max(jnp.abs(g_q)) * jnp.max(jnp.abs(g_k))
    return bound * BOUND_SLACK


def _bound_is_safe(bound):
    return 2.0 * bound < EXP2_SAFE_RANGE


def _rope_table(seq, rot, lane_of_first, period):
    half = rot // 2
    inv = ROPE_THETA ** (-jnp.arange(half, dtype=F32) * 2.0 / rot)
    ang = jnp.arange(seq).astype(F32)[:, None] * inv[None, :]
    cos, sin = jnp.cos(ang), jnp.sin(ang)
    one = jnp.ones((seq, period), F32)
    zero = jnp.zeros((seq, period), F32)
    a, b = lane_of_first, lane_of_first + half
    cos_t = one.at[:, a:a + half].set(cos).at[:, b:b + half].set(cos)
    sin_p = zero.at[:, b:b + half].set(sin)
    sin_m = zero.at[:, a:a + half].set(-sin)
    reps = LANES // period
    return jnp.concatenate([jnp.tile(t, (1, reps)) for t in (cos_t, sin_p, sin_m)], axis=1)


def _pad_heads(w, heads, width, offset=0):
    lead = w.shape[:-1]
    w = w.reshape(lead + (heads, width))
    pad = [(0, 0)] * len(lead) + [(0, 0), (offset, PAD_HEAD - width - offset)]
    return jnp.pad(w, pad).reshape(lead + (heads * PAD_HEAD,))


def _inproj(x, mod, p):
    bt, s, d = x.shape
    tm = min(TM_IN, s)
    ropea = _rope_table(s, ROT_DIMS_A, 0, HEAD_DIM)
    ropeb = _rope_table(s, QK_ROPE, QK_NOPE, PAD_HEAD)
    w_in = p["w_in"]
    wqkv = w_in[:, 0:3 * A_WIDTH].astype(BF16)
    wlat = w_in[:, 3 * A_WIDTH:3 * A_WIDTH + Q_LORA + KV_LORA].astype(BF16)
    wkr = _pad_heads(w_in[:, 3 * A_WIDTH + Q_LORA + KV_LORA:], 1, QK_ROPE, QK_NOPE).astype(BF16)
    wqb = _pad_heads(p["w_qb"], B_HEADS, QK_HEAD).astype(BF16)
    wkv = p["w_kvb"].reshape(KV_LORA, B_HEADS, QK_NOPE + V_HEAD)
    wkn = _pad_heads(wkv[:, :, :QK_NOPE].reshape(KV_LORA, B_HEADS * QK_NOPE), B_HEADS, QK_NOPE).astype(BF16)
    wv = wkv[:, :, QK_NOPE:].reshape(KV_LORA, B_WIDTH).astype(BF16)
    gqa = jnp.tile(p["g_qa"], A_HEADS).reshape(1, -1)
    gka = jnp.tile(p["g_ka"], A_HEADS).reshape(1, -1)
    gqb = _pad_heads(jnp.tile(p["g_qb"], B_HEADS), B_HEADS, QK_HEAD).reshape(1, -1)
    gkb = _pad_heads(jnp.tile(p["g_kb"], B_HEADS), B_HEADS, QK_HEAD).reshape(1, -1)
    seg64 = jnp.kron(jnp.eye(256 // HEAD_DIM, dtype=F32), jnp.ones((HEAD_DIM, HEAD_DIM), F32)).astype(BF16)
    seg128 = jnp.kron(jnp.eye(2, dtype=F32), jnp.ones((PAD_HEAD, PAD_HEAD), F32)).astype(BF16)

    def full(a):
        return pl.BlockSpec(a.shape, lambda i, b: (0,) * a.ndim)

    spare = jnp.zeros((B_HEADS, PAD_HEAD), F32).at[:, QK_HEAD].set(1.0).reshape(1, -1)
    qpad = -_score_bound(p["g_qb"], p["g_kb"], QK_HEAD) * spare
    consts = [p["g_norm1"].reshape(1, -1), wqkv, wlat, wkr, wqb, wkn, wv, gqa, gka,
              p["g_qlat"].reshape(1, -1), p["g_kvlat"].reshape(1, -1), gqb, gkb, seg64, seg128, qpad, spare]
    tok = lambda w: pl.BlockSpec((None, tm, w), lambda i, b: (b, i, 0))
    qw = B_HEADS * PAD_HEAD
    tok_shape = lambda w, dt=F32: jax.ShapeDtypeStruct((bt, s, w), dt)
    return pl.pallas_call(
        _inproj_kernel,
        grid=(s // tm, bt),
        in_specs=[tok(d), pl.BlockSpec((None, 6, d), lambda i, b: (b, 0, 0))]
        + [full(a) for a in consts]
        + [pl.BlockSpec((tm, 3 * LANES), lambda i, b: (i, 0))] * 2,
        out_specs=[tok(A_WIDTH), tok(A_WIDTH), tok(A_WIDTH),
                   pl.BlockSpec((None, qw, tm), lambda i, b: (b, 0, i)),
                   tok(qw),
                   pl.BlockSpec((None, None, B_WIDTH, tm), lambda i, b: (b, i, 0, 0))],
        out_shape=[tok_shape(A_WIDTH), tok_shape(A_WIDTH), tok_shape(A_WIDTH),
                   jax.ShapeDtypeStruct((bt, qw, s), BF16),
                   tok_shape(qw, BF16),
                   jax.ShapeDtypeStruct((bt, s // tm, B_WIDTH, tm), BF16)],
        compiler_params=_cparams("arbitrary", "arbitrary"),
        name="inproj",
    )(x, mod, *consts, ropea, ropeb)


def _rows(start, size, stride):
    return pl.ds(start, size) if stride == 1 else pl.ds(start, size, stride=stride)


def _dilated_kernel(bound_ref, q_ref, k_ref, v_ref, o_ref, stat_ref, *, bounded):
    s_len = q_ref.shape[0]
    first = lax.broadcasted_iota(I32, (TQ_A, LANES), 1) < HEAD_DIM
    for branch, (window, dil) in enumerate(DILATED_BRANCHES):
        radius = window // (2 * dil)
        win = TQ_A + 2 * radius
        length = s_len // dil
        nq = length // TQ_A
        rel = lax.broadcasted_iota(I32, (TQ_A, win), 0) - lax.broadcasted_iota(I32, (TQ_A, win), 1)

        def group(g, carry, branch=branch, dil=dil, radius=radius, win=win, length=length, nq=nq, rel=rel):
            done = []
            for u in range(UNROLL_A):
                idx = g * UNROLL_A + u
                r = idx // nq
                q0 = (idx % nq) * TQ_A
                w0 = jnp.clip(q0 - radius, 0, length - win)
                qrows = _rows(r + dil * q0, TQ_A, dil)
                krows = _rows(r + dil * w0, win, dil)
                q = q_ref[qrows, :]
                kw = k_ref[krows, :].astype(BF16)
                vw = v_ref[krows, :].astype(BF16)
                valid = jnp.abs(rel + (q0 - w0)) <= radius
                outs, stats = [], []
                for keep in (first, jnp.logical_not(first)):
                    s = _nt_dot(jnp.where(keep, q, 0.0).astype(BF16), kw)
                    if bounded:
                        pexp = jnp.exp2(jnp.where(valid, s - bound_ref[0], NEG_BIG))
                        outs.append(_dot(pexp.astype(BF16), vw))
                        stats.append(jnp.sum(pexp, axis=-1, keepdims=True))
                    else:
                        s = jnp.where(valid, s, NEG_BIG)
                        m = jnp.max(s, axis=-1, keepdims=True)
                        pexp = jnp.exp2(s - m)
                        den = jnp.sum(pexp, axis=-1, keepdims=True)
                        outs.append(_dot(pexp.astype(BF16), vw) / den)
                        stats.append(m + jnp.log2(den))
                o_new = jnp.where(first, outs[0], outs[1])
                stat_new = jnp.where(first, stats[0], stats[1])
                if branch > 0:
                    o_old, stat_old = o_ref[qrows, :], stat_ref[qrows, :]
                    if bounded:
                        o_new, stat_new = o_old + o_new, stat_old + stat_new
                    else:
                        mx = jnp.maximum(stat_old, stat_new)
                        w_old, w_new = jnp.exp2(stat_old - mx), jnp.exp2(stat_new - mx)
                        den = w_old + w_new
                        o_new = (w_old * o_old + w_new * o_new) / den
                        stat_new = mx + jnp.log2(den)
                done.append((qrows, o_new, stat_new))
            for qrows, o_new, stat_new in done:
                o_ref[qrows, :] = o_new
                stat_ref[qrows, :] = stat_new
            return carry

        lax.fori_loop(0, dil * nq // UNROLL_A, group, 0)

    if bounded:
        def normalise(i, carry):
            rows = pl.ds(pl.multiple_of(i * NORM_ROWS_A, NORM_ROWS_A), NORM_ROWS_A)
            o_ref[rows, :] = o_ref[rows, :] / stat_ref[rows, :]
            return carry

        lax.fori_loop(0, s_len // NORM_ROWS_A, normalise, 0)


def _dilated_call(bounded, bound, qa, ka, va):
    bt, s, width = qa.shape
    spec = pl.BlockSpec((None, s, LANES), lambda b, j: (b, 0, j))
    return pl.pallas_call(
        functools.partial(_dilated_kernel, bounded=bounded),
        grid=(bt, width // LANES),
        in_specs=[pl.BlockSpec(memory_space=pltpu.SMEM), spec, spec, spec],
        out_specs=spec,
        out_shape=jax.ShapeDtypeStruct((bt, s, width), F32),
        scratch_shapes=[pltpu.VMEM((s, LANES), F32)],
        compiler_params=_cparams("arbitrary", "arbitrary"),
        name="dilated_bounded" if bounded else "dilated_online",
    )(bound.reshape(1), qa, ka, va)


def _dilated_attention(qa, ka, va, p):
    bound = _score_bound(p["g_qa"], p["g_ka"], HEAD_DIM)
    return lax.cond(_bound_is_safe(bound),
                    functools.partial(_dilated_call, True), functools.partial(_dilated_call, False),
                    bound, qa, ka, va)


def _latent_online_kernel(qt_ref, k_ref, vt_ref, o_ref):
    n_kv, _, tk = vt_ref.shape
    tq = qt_ref.shape[1]
    n_chunks = tk // KC_B
    qts = [qt_ref[h * PAD_HEAD:(h + 1) * PAD_HEAD, :] for h in range(2)]

    def scores(j, h):
        k0 = pl.multiple_of(j * tk, tk)
        return tuple(_dot(k_ref[pl.ds(k0 + c * KC_B, KC_B), h * PAD_HEAD:(h + 1) * PAD_HEAD], qts[h])
                     for c in range(n_chunks))

    def body(j, carry):
        nxt = jnp.minimum(j + 1, n_kv - 1)
        new = []
        for h in range(2):
            m, l, acc, sts = carry[h]
            sts_next = scores(nxt, h)
            for c in range(n_chunks):
                m_new = jnp.maximum(m, jnp.max(sts[c], axis=0, keepdims=True))
                alpha = jnp.exp2(m - m_new)
                pexp = jnp.exp2(sts[c] - m_new)
                l = alpha * l + jnp.sum(pexp, axis=0, keepdims=True)
                vt = vt_ref[j, h * V_HEAD:(h + 1) * V_HEAD, c * KC_B:(c + 1) * KC_B]
                acc = alpha * acc + _dot(vt, pexp.astype(BF16))
                m = m_new
            new.append((m, l, acc, sts_next))
        return tuple(new)

    init = tuple((jnp.full((1, tq), NEG_BIG, F32), jnp.zeros((1, tq), F32), jnp.zeros((V_HEAD, tq), F32),
                  scores(0, h)) for h in range(2))
    (_, l0, acc0, _), (_, l1, acc1, _) = lax.fori_loop(0, n_kv, body, init)
    ot = jnp.concatenate([acc0 / l0, acc1 / l1], axis=0)
    o_ref[...] = ot.T.astype(o_ref.dtype)


def _latent_bounded_kernel(qt_ref, k_ref, vt_ref, o_ref):
    n_kv, _, tk = vt_ref.shape
    tq = qt_ref.shape[1]
    n_chunks = tk // KC_FAST_B
    qts = [qt_ref[h * PAD_HEAD:(h + 1) * PAD_HEAD, :] for h in range(2)]
    group = min(KV_TILES_B, n_kv)

    def probs(jj, h):
        out = []
        for t in range(group):
            k0 = pl.multiple_of((jj * group + t) * tk, tk)
            for c in range(n_chunks):
                kt = k_ref[pl.ds(k0 + c * KC_FAST_B, KC_FAST_B), h * PAD_HEAD:(h + 1) * PAD_HEAD]
                pexp = jnp.exp2(_dot(kt, qts[h]))
                out.append((pexp.astype(BF16), jnp.sum(pexp, axis=0, keepdims=True)))
        return tuple(out)

    def body(jj, carry):
        nxt = jnp.minimum(jj + 1, n_kv // group - 1)
        new = []
        for h in range(2):
            l, acc, cur = carry[h]
            nxt_probs = probs(nxt, h)
            for t in range(group):
                for c in range(n_chunks):
                    pexp, psum = cur[t * n_chunks + c]
                    vt = vt_ref[jj * group + t, h * V_HEAD:(h + 1) * V_HEAD, c * KC_FAST_B:(c + 1) * KC_FAST_B]
                    acc = acc + _dot(vt, pexp)
                    l = l + psum
            new.append((l, acc, nxt_probs))
        return tuple(new)

    init = tuple((jnp.zeros((1, tq), F32), jnp.zeros((V_HEAD, tq), F32), probs(0, h)) for h in range(2))
    (l0, acc0, _), (l1, acc1, _) = lax.fori_loop(0, n_kv // group, body, init)
    ot = jnp.concatenate([acc0 / l0, acc1 / l1], axis=0)
    o_ref[...] = ot.T.astype(o_ref.dtype)


def _latent_call(bounded, qbt, kb, vbt):
    bt, s, _ = kb.shape
    pairs = B_HEADS // 2
    n_kv, tk = vbt.shape[1], vbt.shape[3]
    tq = min(TQ_B, s)
    return pl.pallas_call(
        _latent_bounded_kernel if bounded else _latent_online_kernel,
        grid=(bt, pairs, s // tq),
        in_specs=[
            pl.BlockSpec((None, 2 * PAD_HEAD, tq), lambda b, hp, i: (b, hp, i)),
            pl.BlockSpec((None, s, 2 * PAD_HEAD), lambda b, hp, i: (b, 0, hp)),
            pl.BlockSpec((None, n_kv, 2 * V_HEAD, tk), lambda b, hp, i: (b, 0, hp, 0)),
        ],
        out_specs=pl.BlockSpec((None, tq, 2 * V_HEAD), lambda b, hp, i: (b, i, hp)),
        out_shape=jax.ShapeDtypeStruct((bt, s, B_WIDTH), BF16),
        compiler_params=_cparams("arbitrary", "arbitrary", "arbitrary"),
        name="latent_bounded" if bounded else "latent_online",
    )(qbt, kb, vbt)


def _latent_attention(qbt, kb, vbt, p):
    safe = _bound_is_safe(_score_bound(p["g_qb"], p["g_kb"], QK_HEAD))
    return lax.cond(safe, functools.partial(_latent_call, True), functools.partial(_latent_call, False),
                    qbt, kb, vbt)


def _outproj_kernel(oa_ref, ob_ref, x_ref, mod_ref,
                    woa_ref, wob_ref, g2_ref, wrh_ref, wrl_ref, br_ref, tri_ref,
                    x1_ref, h2_ref, idx_ref, wts_ref, rank_ref, cnt_ref, base_ref):
    first_step = jnp.logical_and(pl.program_id(0) == 0, pl.program_id(1) == 0)

    @pl.when(first_step)
    def _():
        base_ref[...] = jnp.zeros_like(base_ref)

    mixed = _dot(oa_ref[...].astype(BF16), woa_ref[...]) + _dot(ob_ref[...], wob_ref[...])
    x1 = x_ref[...] + mod_ref[2:3, :] * mixed
    x1_ref[...] = x1
    d = x1.shape[1]
    h2 = x1 * lax.rsqrt(jnp.sum(x1 * x1, axis=-1, keepdims=True) * (1.0 / d) + EPS) * g2_ref[...]
    h2 = h2 * (1.0 + mod_ref[4:5, :]) + mod_ref[3:4, :]
    _store_token_rows(h2_ref, _pack_rows(h2))

    hh, hl = _split_bf16(h2)
    logits = _nt_dot(wrh_ref[...], hh) + _nt_dot(wrh_ref[...], hl) + _nt_dot(wrl_ref[...], hh)
    scores = _sigmoid(logits)
    biased = scores + br_ref[...]
    tm = biased.shape[1]
    sub = lax.broadcasted_iota(I32, (PER_GROUP, tm), 0)
    groups, gscore = [], []
    for g in range(N_GROUPS):
        bg = biased[g * PER_GROUP:(g + 1) * PER_GROUP, :]
        m1 = jnp.max(bg, axis=0, keepdims=True)
        i1 = jnp.min(jnp.where(bg == m1, sub, PER_GROUP), axis=0, keepdims=True)
        m2 = jnp.max(jnp.where(sub == i1, -jnp.inf, bg), axis=0, keepdims=True)
        groups.append(bg)
        gscore.append(m1 + m2)
    masked = []
    for g in range(N_GROUPS):
        ahead = jnp.zeros((1, tm), I32)
        for g2 in range(N_GROUPS):
            if g2 == g:
                continue
            beats = (gscore[g2] > gscore[g]) if g2 > g else (gscore[g2] >= gscore[g])
            ahead = ahead + beats.astype(I32)
        masked.append(jnp.where(ahead < TOPK_GROUPS, groups[g], -jnp.inf))
    cand = jnp.concatenate(masked, axis=0)
    eid = lax.broadcasted_iota(I32, (N_EXPERTS, tm), 0)
    chosen = jnp.zeros((N_EXPERTS, tm), F32)
    sel_idx, sel_w, sel_hot = [], [], []
    for _ in range(TOP_K):
        mv = jnp.max(cand, axis=0, keepdims=True)
        ie = jnp.min(jnp.where(cand == mv, eid, N_EXPERTS), axis=0, keepdims=True)
        hot = eid == ie
        sel_idx.append(ie)
        sel_hot.append(hot)
        sel_w.append(jnp.sum(jnp.where(hot, scores, 0.0), axis=0, keepdims=True))
        cand = jnp.where(hot, -jnp.inf, cand)
        chosen = chosen + hot.astype(F32)
    wsum = sel_w[0]
    for w in sel_w[1:]:
        wsum = wsum + w
    within = _dot(chosen.astype(BF16), tri_ref[...]) + base_ref[...]
    ranks = [jnp.sum(jnp.where(hot, within, 0.0), axis=0, keepdims=True) for hot in sel_hot]
    pad_i = [jnp.zeros((1, tm), I32)] * (8 - TOP_K)
    pad_f = [jnp.zeros((1, tm), F32)] * (8 - TOP_K)
    idx_ref[...] = jnp.concatenate(sel_idx + pad_i, axis=0)
    wts_ref[...] = jnp.concatenate([w / wsum * ROUTED_SCALE for w in sel_w] + pad_f, axis=0)
    rank_ref[...] = jnp.concatenate([r.astype(I32) for r in ranks] + pad_i, axis=0)
    base_ref[...] = base_ref[...] + jnp.sum(chosen, axis=1, keepdims=True)
    cnt_ref[...] = jnp.broadcast_to(base_ref[...], cnt_ref.shape)


def _outproj(oa, ob, x, mod, p):
    bt, s, d = x.shape
    tm = min(TM_OUT, s)
    n = bt * s
    woa = p["w_o"][:A_WIDTH].astype(BF16)
    wob = p["w_o"][A_WIDTH:].astype(BF16)
    wr_t = p["w_router"].T
    wrh, wrl = _split_bf16(wr_t)
    tri = (jnp.arange(tm)[:, None] < jnp.arange(tm)[None, :]).astype(BF16)
    consts = [woa, wob, p["g_norm2"].reshape(1, -1), wrh, wrl, p["b_router"].reshape(-1, 1), tri]
    tiles = s // tm
    tok = lambda w: pl.BlockSpec((None, tm, w), lambda b, i: (b, i, 0))
    full = lambda a: pl.BlockSpec(a.shape, lambda b, i: (0,) * a.ndim)
    flat = lambda w: pl.BlockSpec((tm, w), lambda b, i: (b * tiles + i, 0))
    slab = pl.BlockSpec((8, tm), lambda b, i: (0, b * tiles + i))
    return pl.pallas_call(
        _outproj_kernel,
        grid=(bt, tiles),
        in_specs=[tok(A_WIDTH), tok(B_WIDTH), tok(d), pl.BlockSpec((None, 6, d), lambda b, i: (b, 0, 0))]
        + [full(a) for a in consts],
        out_specs=[flat(d), pl.BlockSpec((tm * ROW_SUB, LANES), lambda b, i: (b * tiles + i, 0)),
                   slab, slab, slab, pl.BlockSpec((N_EXPERTS, LANES), lambda b, i: (0, 0))],
        out_shape=[jax.ShapeDtypeStruct((n, d), F32), jax.ShapeDtypeStruct((n * ROW_SUB, LANES), U32),
                   jax.ShapeDtypeStruct((8, n), I32), jax.ShapeDtypeStruct((8, n), F32),
                   jax.ShapeDtypeStruct((8, n), I32), jax.ShapeDtypeStruct((N_EXPERTS, LANES), F32)],
        scratch_shapes=[pltpu.VMEM((N_EXPERTS, 1), F32)],
        compiler_params=_cparams("arbitrary", "arbitrary"),
        name="outproj_router",
    )(oa, ob, x, mod, *consts)


def _dest_kernel(pstart_ref, idx_ref, rank_ref, dest_ref):
    idx = idx_ref[...]
    dest = rank_ref[...]
    for e in range(N_EXPERTS):
        dest = dest + jnp.where(idx == e, pstart_ref[e], 0)
    dest_ref[...] = dest * ROW_SUB


def _dest(pstart, idx_t, rank_t):
    n = idx_t.shape[1]
    tn = min(2048, n)
    slab = pl.BlockSpec((8, tn), lambda i, ps: (0, i))
    return pl.pallas_call(
        _dest_kernel,
        grid_spec=pltpu.PrefetchScalarGridSpec(
            num_scalar_prefetch=1, grid=(n // tn,), in_specs=[slab, slab], out_specs=slab),
        out_shape=jax.ShapeDtypeStruct((8, n), I32),
        compiler_params=_cparams("arbitrary"),
        name="moe_dest",
    )(pstart, idx_t, rank_t)


def _dispatch_kernel(pad_start_ref, pad_len_ref, nused_ref, dest_ref, h_ref, xs_ref, zero_ref, sem, pad_sem):
    tm = h_ref.shape[0] // ROW_SUB

    @pl.when(pl.program_id(0) == 0)
    def _():
        _zero_unrouted_slots(pad_start_ref, pad_len_ref, nused_ref, xs_ref, zero_ref, pad_sem,
                             xs_ref.shape[0] // zero_ref.shape[0])

    def slab(ref, first_row):
        return ref.at[pl.ds(pl.multiple_of(first_row, ROW_SUB), ROW_SUB), :]

    def issue(j, carry):
        for k in range(TOP_K):
            pltpu.make_async_copy(slab(h_ref, j * ROW_SUB), slab(xs_ref, dest_ref[k, j]), sem).start()
        return carry

    lax.fori_loop(0, tm, issue, 0, unroll=ISSUE_UNROLL)
    for k in range(TOP_K):
        pltpu.make_async_copy(h_ref, xs_ref.at[pl.ds(0, tm * ROW_SUB), :], sem).wait()


def _dispatch(pad_start, pad_len, n_used, dest_t, hp, n_slots):
    n = hp.shape[0] // ROW_SUB
    tm = min(TM_ROW, n)
    return pl.pallas_call(
        _dispatch_kernel,
        grid_spec=pltpu.PrefetchScalarGridSpec(
            num_scalar_prefetch=3, grid=(n // tm,),
            in_specs=[
                pl.BlockSpec((8, tm), lambda i, ps, pn, nu: (0, i), memory_space=pltpu.SMEM),
                pl.BlockSpec((tm * ROW_SUB, LANES), lambda i, ps, pn, nu: (i, 0)),
            ],
            out_specs=pl.BlockSpec(memory_space=pl.ANY),
            scratch_shapes=[pltpu.VMEM((EXPERT_BLOCK * ROW_SUB, LANES), U32), pltpu.SemaphoreType.DMA(()),
                            pltpu.SemaphoreType.DMA(())]),
        out_shape=jax.ShapeDtypeStruct((n_slots * ROW_SUB, LANES), U32),
        compiler_params=_cparams("arbitrary"),
        name="moe_dispatch",
    )(pad_start, pad_len, n_used, dest_t, hp)


def _expert_kernel(be_ref, nused_ref, xs_ref, wg_ref, wu_ref, wd_ref, ys_ref):
    i = pl.program_id(0)

    @pl.when(i < nused_ref[0])
    def _():
        xb = _load_token_rows(xs_ref).astype(BF16)
        a = _dot(xb, wg_ref[...])
        u = _dot(xb, wu_ref[...])
        _store_token_rows(ys_ref, _pack_rows(_dot((_silu(a) * u).astype(BF16), wd_ref[...])))

    @pl.when(i >= nused_ref[0])
    def _():
        ys_ref[...] = jnp.zeros_like(ys_ref)


def _experts(block_e, n_used, xs, wg, wu, wd):
    n_slots = xs.shape[0] // ROW_SUB
    n_blocks = n_slots // EXPERT_BLOCK
    d, de = wg.shape[1], wg.shape[2]
    rows = pl.BlockSpec((EXPERT_BLOCK * ROW_SUB, LANES), lambda i, be, nu: (i, 0))
    used_rows = pl.BlockSpec((EXPERT_BLOCK * ROW_SUB, LANES), lambda i, be, nu: (jnp.minimum(i, nu[0] - 1), 0))
    return pl.pallas_call(
        _expert_kernel,
        grid_spec=pltpu.PrefetchScalarGridSpec(
            num_scalar_prefetch=2, grid=(n_blocks,),
            in_specs=[used_rows,
                      pl.BlockSpec((None, d, de), lambda i, be, nu: (be[i], 0, 0)),
                      pl.BlockSpec((None, d, de), lambda i, be, nu: (be[i], 0, 0)),
                      pl.BlockSpec((None, de, d), lambda i, be, nu: (be[i], 0, 0))],
            out_specs=rows),
        out_shape=jax.ShapeDtypeStruct((n_slots * ROW_SUB, LANES), U32),
        compiler_params=_cparams("arbitrary"),
        name="moe_experts",
    )(block_e, n_used, xs, wg, wu, wd)


def _combine_kernel(dest_ref, dest_next_ref, h_ref, x1_ref, mod_ref, wts_ref, wsg_ref, wsu_ref, wsd_ref, ys_ref,
                    out_ref, buf_ref, sems):
    i = pl.program_id(0)
    slot = lax.rem(i, 2)
    tm = h_ref.shape[0] // ROW_SUB

    def gather(idx_ref, into):
        def issue(j, carry):
            for k in range(TOP_K):
                src = ys_ref.at[pl.ds(pl.multiple_of(idx_ref[k, j], ROW_SUB), ROW_SUB), :]
                dst = buf_ref.at[into, k, pl.ds(pl.multiple_of(j * ROW_SUB, ROW_SUB), ROW_SUB), :]
                pltpu.make_async_copy(src, dst, sems.at[into]).start()
            return carry

        lax.fori_loop(0, tm, issue, 0, unroll=ISSUE_UNROLL)

    @pl.when(i == 0)
    def _():
        gather(dest_ref, 0)

    @pl.when(i + 1 < pl.num_programs(0))
    def _():
        gather(dest_next_ref, 1 - slot)

    hb = _load_token_rows(h_ref).astype(BF16)
    act = _silu(_dot(hb, wsg_ref[...])) * _dot(hb, wsu_ref[...])
    y = _dot(act.astype(BF16), wsd_ref[...])
    for k in range(TOP_K):
        pltpu.make_async_copy(ys_ref.at[pl.ds(0, tm * ROW_SUB), :], buf_ref.at[slot, k], sems.at[slot]).wait()
    wts = wts_ref[...]
    for k in range(TOP_K):
        y = y + wts[:, k:k + 1] * _load_token_rows(buf_ref.at[slot, k])
    out_ref[...] = x1_ref[...] + mod_ref[5:6, :] * y


def _combine(dest_t, hp, x1, mod, wts, ys, p, bt, s):
    n, d = x1.shape
    tm = min(TM_ROW, s)
    tiles = s // tm
    steps = n // tm
    wsg, wsu, wsd = (p[k].astype(BF16) for k in ("w_s_gate", "w_s_up", "w_s_down"))
    full = lambda a: pl.BlockSpec(a.shape, lambda i: (0,) * a.ndim)
    rows = pl.BlockSpec((tm, d), lambda i: (i, 0))
    out = pl.pallas_call(
        _combine_kernel,
        grid=(steps,),
        in_specs=[
            pl.BlockSpec((8, tm), lambda i: (0, i), memory_space=pltpu.SMEM),
            pl.BlockSpec((8, tm), lambda i: (0, jnp.minimum(i + 1, steps - 1)), memory_space=pltpu.SMEM),
            pl.BlockSpec((tm * ROW_SUB, LANES), lambda i: (i, 0)),
            rows,
            pl.BlockSpec((None, 6, d), lambda i: (i // tiles, 0, 0)),
            pl.BlockSpec((tm, 8), lambda i: (i, 0)),
            full(wsg), full(wsu), full(wsd),
            pl.BlockSpec(memory_space=pl.ANY),
        ],
        out_specs=rows,
        out_shape=jax.ShapeDtypeStruct((n, d), F32),
        scratch_shapes=[pltpu.VMEM((2, TOP_K, tm * ROW_SUB, LANES), U32), pltpu.SemaphoreType.DMA((2,))],
        compiler_params=_cparams("arbitrary"),
        name="moe_combine",
    )(dest_t, dest_t, hp, x1, mod, wts, wsg, wsu, wsd, ys)
    return out.reshape(bt, s, d)


def _layer(x, c, p):
    bt, s, d = x.shape
    n = bt * s
    mod = _adaln(c, p["w_ada"], p["b_ada"]).reshape(bt, 6, d)
    qa, ka, va, qb, kb, vb = _inproj(x, mod, p)
    oa = _dilated_attention(qa, ka, va, p)
    ob = _latent_attention(qb, kb, vb, p)
    x1, hp, idx_t, wts_t, rank_t, cnt = _outproj(oa, ob, x, mod, p)

    counts = cnt[:, 0].astype(I32)
    padded = (counts + EXPERT_BLOCK - 1) // EXPERT_BLOCK * EXPERT_BLOCK
    pend = jnp.cumsum(padded)
    pstart = pend - padded
    n_blocks = (n * TOP_K + N_EXPERTS * (EXPERT_BLOCK - 1)) // EXPERT_BLOCK
    block_first = jnp.arange(n_blocks, dtype=I32) * EXPERT_BLOCK
    block_e = jnp.minimum(jnp.sum((pend[None, :] <= block_first[:, None]).astype(I32), axis=1), N_EXPERTS - 1)
    n_used = (pend[-1:] // EXPERT_BLOCK).astype(I32)

    dest_t = _dest(pstart.astype(I32), idx_t, rank_t)
    xs = _dispatch(((pstart + counts) * ROW_SUB).astype(I32), (padded - counts).astype(I32), n_used, dest_t, hp,
                   n_blocks * EXPERT_BLOCK)
    ys = _experts(block_e, n_used, xs, p["w_e_gate"].astype(BF16), p["w_e_up"].astype(BF16),
                  p["w_e_down"].astype(BF16))
    return _combine(dest_t, hp, x1, mod, wts_t.T, ys, p, bt, s)


_PARAM_NAMES = ("w_ada", "b_ada", "g_norm1", "w_in", "g_qa", "g_ka", "g_qlat", "w_qb", "g_kvlat", "w_kvb",
                "g_qb", "g_kb", "w_o", "g_norm2", "w_router", "b_router", "w_e_gate", "w_e_up", "w_e_down",
                "w_s_gate", "w_s_up", "w_s_down")


def kernel(x_prompt, x_sample, c_prompt, c_sample, w_ada, b_ada, g_norm1, w_in, g_qa, g_ka, g_qlat, w_qb,
           g_kvlat, w_kvb, g_qb, g_kb, w_o, g_norm2, w_router, b_router, w_e_gate, w_e_up, w_e_down,
           w_s_gate, w_s_up, w_s_down):
    stacked = (w_ada, b_ada, g_norm1, w_in, g_qa, g_ka, g_qlat, w_qb, g_kvlat, w_kvb, g_qb, g_kb, w_o,
               g_norm2, w_router, b_router, w_e_gate, w_e_up, w_e_down, w_s_gate, w_s_up, w_s_down)
    y_prompt, y_sample = x_prompt, x_sample
    for layer in range(w_ada.shape[0]):
        p = {name: w[layer] for name, w in zip(_PARAM_NAMES, stacked)}
        y_prompt = _layer(y_prompt, c_prompt, p)
        y_sample = _layer(y_sample, c_sample, p)
    return (y_prompt, y_sample)
```

```python
import functools

import jax
import jax.numpy as jnp
from jax import lax
from jax.experimental import pallas as pl
from jax.experimental.pallas import tpu as pltpu

F32 = jnp.float32
BF16 = jnp.bfloat16
I32 = jnp.int32
U32 = jnp.uint32

HEAD_DIM = 64
A_HEADS = 12
A_WIDTH = A_HEADS * HEAD_DIM
DILATED_BRANCHES = ((128, 1), (512, 4), (2048, 16))
ROT_DIMS_A = HEAD_DIM // 4
B_HEADS = 4
QK_NOPE = 64
QK_ROPE = 32
QK_HEAD = QK_NOPE + QK_ROPE
V_HEAD = 64
Q_LORA = 256
KV_LORA = 256
B_WIDTH = B_HEADS * V_HEAD
ROPE_THETA = 500000.0
N_EXPERTS = 64
TOP_K = 6
N_GROUPS = 8
TOPK_GROUPS = 4
PER_GROUP = N_EXPERTS // N_GROUPS
ROUTED_SCALE = 2.5
EXPERT_BLOCK = 512
EPS = 1e-6

LOG2E = 1.4426950408889634
BOUND_SLACK = 1.02
EXP2_SAFE_RANGE = 120.0

LANES = 128
PAD_HEAD = 128
NEG_BIG = -1e30
VMEM_LIMIT = 56 * 1024 * 1024

TM_IN = 512
TM_OUT = 512
TQ_A = 128
UNROLL_A = 16
NORM_ROWS_A = 512
TQ_B = 512
KC_B = 128
KC_FAST_B = 256
KV_TILES_B = 2
TM_DISPATCH = 512
TM_COMBINE = 256
ISSUE_UNROLL = 4
ROW_SUB = 4


def _nt_dot(a, b):
    return lax.dot_general(a, b, (((1,), (1,)), ((), ())), preferred_element_type=F32)


def _dot(a, b):
    return jnp.dot(a, b, preferred_element_type=F32)


def _split_bf16(x):
    hi = x.astype(BF16)
    lo = (x - hi.astype(F32)).astype(BF16)
    return hi, lo


def _sigmoid(x):
    return 1.0 / (1.0 + jnp.exp(-x))


def _silu(x):
    return x * _sigmoid(x)


def _pack_rows(x):
    w = x.shape[1] // 2
    lo = lax.bitcast_convert_type(x[:, :w].astype(BF16).astype(F32), U32)
    hi = lax.bitcast_convert_type(x[:, w:].astype(BF16).astype(F32), U32)
    return (lo >> 16) | (hi & jnp.uint32(0xFFFF0000))


def _unpack_words(words):
    lo = lax.bitcast_convert_type(words << 16, F32)
    hi = lax.bitcast_convert_type(words & jnp.uint32(0xFFFF0000), F32)
    return lo, hi


def _store_token_rows(ref, words):
    m = ref.shape[0] // ROW_SUB
    for c in range(ROW_SUB):
        ref[pl.ds(c, m, stride=ROW_SUB), :] = words[:, c * LANES:(c + 1) * LANES]


def _load_token_rows(ref):
    m = ref.shape[0] // ROW_SUB
    halves = [_unpack_words(ref[pl.ds(c, m, stride=ROW_SUB), :]) for c in range(ROW_SUB)]
    return jnp.concatenate([lo for lo, _ in halves] + [hi for _, hi in halves], axis=1)


def _zero_unrouted_slots(pad_start_ref, pad_len_ref, nused_ref, xs_ref, zero_ref, pad_sem, n_blocks):
    zero_ref[...] = jnp.zeros_like(zero_ref)
    block_rows = zero_ref.shape[0]

    def slab(ref, first_row):
        return ref.at[pl.ds(pl.multiple_of(first_row, ROW_SUB), ROW_SUB), :]

    def pad_copy(e, r):
        return pltpu.make_async_copy(slab(zero_ref, 0), slab(xs_ref, pad_start_ref[e] + r * ROW_SUB), pad_sem)

    def per_expert(e, carry):
        lax.fori_loop(0, pad_len_ref[e], lambda r, c: (pad_copy(e, r).start(), c)[1], 0)
        lax.fori_loop(0, pad_len_ref[e], lambda r, c: (pad_copy(e, r).wait(), c)[1], 0)
        return carry

    lax.fori_loop(0, N_EXPERTS, per_expert, 0)

    def block_copy(b):
        rows = pl.ds(pl.multiple_of(b * block_rows, block_rows), block_rows)
        return pltpu.make_async_copy(zero_ref, xs_ref.at[rows, :], pad_sem)

    lax.fori_loop(nused_ref[0], n_blocks, lambda b, c: (block_copy(b).start(), c)[1], 0)
    lax.fori_loop(nused_ref[0], n_blocks, lambda b, c: (block_copy(b).wait(), c)[1], 0)


def _cparams(*sem):
    return pltpu.CompilerParams(dimension_semantics=sem, vmem_limit_bytes=VMEM_LIMIT)


def _adaln_kernel(c_ref, w_ref, b_ref, o_ref):
    s = _silu(c_ref[...])
    o_ref[...] = _dot(s.astype(BF16), w_ref[...].astype(BF16)) + b_ref[...]


def _adaln(c, w_ada, b_ada):
    bt, d = c.shape
    n_chunks = w_ada.shape[1] // d
    return pl.pallas_call(
        _adaln_kernel,
        grid=(n_chunks,),
        in_specs=[
            pl.BlockSpec((bt, d), lambda j: (0, 0)),
            pl.BlockSpec((d, d), lambda j: (0, j)),
            pl.BlockSpec((1, d), lambda j: (0, j)),
        ],
        out_specs=pl.BlockSpec((bt, d), lambda j: (0, j)),
        out_shape=jax.ShapeDtypeStruct((bt, w_ada.shape[1]), F32),
        compiler_params=_cparams("arbitrary"),
        name="adaln",
    )(c, w_ada, b_ada.reshape(1, -1))


def _seg_sumsq(x, seg_ref):
    sq = (x * x).astype(BF16)
    width = seg_ref.shape[0]
    outs = [_dot(sq[:, c * width:(c + 1) * width], seg_ref[...]) for c in range(x.shape[1] // width)]
    return outs[0] if len(outs) == 1 else jnp.concatenate(outs, axis=1)


def _tile_lanes(t, reps):
    return t if reps == 1 else jnp.concatenate([t] * reps, axis=1)


def _rope_lanes(x, tab_ref, shift, reps):
    width = x.shape[1]
    cos = _tile_lanes(tab_ref[:, 0:LANES], reps)
    sin_p = _tile_lanes(tab_ref[:, LANES:2 * LANES], reps)
    sin_m = _tile_lanes(tab_ref[:, 2 * LANES:3 * LANES], reps)
    return x * cos + pltpu.roll(x, shift, 1) * sin_p + pltpu.roll(x, width - shift, 1) * sin_m


def _inproj_kernel(x_ref, mod_ref, g1_ref, wqkv_ref, wlat_ref, wkr_ref, wqb_ref, wkn_ref, wv_ref,
                   gqa_ref, gka_ref, gql_ref, gkvl_ref, gqb_ref, gkb_ref, seg64_ref, seg128_ref, qpad_ref, kpad_ref,
                   ropea_ref, ropeb_ref,
                   qa_ref, ka_ref, va_ref, qb_ref, kb_ref, vb_ref):
    x = x_ref[...]
    d = x.shape[1]
    h = x * lax.rsqrt(jnp.sum(x * x, axis=-1, keepdims=True) * (1.0 / d) + EPS) * g1_ref[...]
    h = h * (1.0 + mod_ref[1:2, :]) + mod_ref[0:1, :]
    hb = h.astype(BF16)

    z = _dot(hb, wqkv_ref[...])
    q = z[:, 0:A_WIDTH]
    k = z[:, A_WIDTH:2 * A_WIDTH]
    q = q * lax.rsqrt(_seg_sumsq(q, seg64_ref) * (1.0 / HEAD_DIM) + EPS) * gqa_ref[...]
    k = k * lax.rsqrt(_seg_sumsq(k, seg64_ref) * (1.0 / HEAD_DIM) + EPS) * gka_ref[...]
    reps_a = A_WIDTH // LANES
    q = _rope_lanes(q, ropea_ref, ROT_DIMS_A // 2, reps_a) * (HEAD_DIM ** -0.5 * LOG2E)
    k = _rope_lanes(k, ropea_ref, ROT_DIMS_A // 2, reps_a)
    qa_ref[...] = q
    ka_ref[...] = k
    va_ref[...] = z[:, 2 * A_WIDTH:3 * A_WIDTH]

    zl = _dot(hb, wlat_ref[...])
    ql = zl[:, 0:Q_LORA]
    kvl = zl[:, Q_LORA:Q_LORA + KV_LORA]
    ql = ql * lax.rsqrt(jnp.sum(ql * ql, axis=-1, keepdims=True) * (1.0 / Q_LORA) + EPS) * gql_ref[...]
    kvl = kvl * lax.rsqrt(jnp.sum(kvl * kvl, axis=-1, keepdims=True) * (1.0 / KV_LORA) + EPS) * gkvl_ref[...]
    qlb = ql.astype(BF16)
    kvlb = kvl.astype(BF16)
    reps_b = B_HEADS
    qb = _dot(qlb, wqb_ref[...])
    qb = qb * lax.rsqrt(_seg_sumsq(qb, seg128_ref) * (1.0 / QK_HEAD) + EPS) * gqb_ref[...]
    qb = _rope_lanes(qb, ropeb_ref, QK_ROPE // 2, reps_b) * (QK_HEAD ** -0.5 * LOG2E) + qpad_ref[...]
    qb_ref[...] = qb.T.astype(BF16)
    kr = _dot(hb, wkr_ref[...])
    kb = _dot(kvlb, wkn_ref[...]) + _tile_lanes(kr, reps_b)
    kb = kb * lax.rsqrt(_seg_sumsq(kb, seg128_ref) * (1.0 / QK_HEAD) + EPS) * gkb_ref[...]
    kb = _rope_lanes(kb, ropeb_ref, QK_ROPE // 2, reps_b) + kpad_ref[...]
    kb_ref[...] = kb.astype(BF16)
    vb_ref[...] = _dot(kvlb, wv_ref[...]).T.astype(BF16)


def _score_bound(g_q, g_k, head_dim):
    bound = LOG2E * head_dim ** 0.5 * jnp.max(jnp.abs(g_q)) * jnp.max(jnp.abs(g_k))
    return bound * BOUND_SLACK


def _bound_is_safe(bound):
    return 2.0 * bound < EXP2_SAFE_RANGE


def _rope_table(seq, rot, lane_of_first, period):
    half = rot // 2
    inv = ROPE_THETA ** (-jnp.arange(half, dtype=F32) * 2.0 / rot)
    ang = jnp.arange(seq).astype(F32)[:, None] * inv[None, :]
    cos, sin = jnp.cos(ang), jnp.sin(ang)
    one = jnp.ones((seq, period), F32)
    zero = jnp.zeros((seq, period), F32)
    a, b = lane_of_first, lane_of_first + half
    cos_t = one.at[:, a:a + half].set(cos).at[:, b:b + half].set(cos)
    sin_p = zero.at[:, b:b + half].set(sin)
    sin_m = zero.at[:, a:a + half].set(-sin)
    reps = LANES // period
    return jnp.concatenate([jnp.tile(t, (1, reps)) for t in (cos_t, sin_p, sin_m)], axis=1)


def _pad_heads(w, heads, width, offset=0):
    lead = w.shape[:-1]
    w = w.reshape(lead + (heads, width))
    pad = [(0, 0)] * len(lead) + [(0, 0), (offset, PAD_HEAD - width - offset)]
    return jnp.pad(w, pad).reshape(lead + (heads * PAD_HEAD,))


def _inproj(x, mod, p):
    bt, s, d = x.shape
    tm = min(TM_IN, s)
    ropea = _rope_table(s, ROT_DIMS_A, 0, HEAD_DIM)
    ropeb = _rope_table(s, QK_ROPE, QK_NOPE, PAD_HEAD)
    w_in = p["w_in"]
    wqkv = w_in[:, 0:3 * A_WIDTH].astype(BF16)
    wlat = w_in[:, 3 * A_WIDTH:3 * A_WIDTH + Q_LORA + KV_LORA].astype(BF16)
    wkr = _pad_heads(w_in[:, 3 * A_WIDTH + Q_LORA + KV_LORA:], 1, QK_ROPE, QK_NOPE).astype(BF16)
    wqb = _pad_heads(p["w_qb"], B_HEADS, QK_HEAD).astype(BF16)
    wkv = p["w_kvb"].reshape(KV_LORA, B_HEADS, QK_NOPE + V_HEAD)
    wkn = _pad_heads(wkv[:, :, :QK_NOPE].reshape(KV_LORA, B_HEADS * QK_NOPE), B_HEADS, QK_NOPE).astype(BF16)
    wv = wkv[:, :, QK_NOPE:].reshape(KV_LORA, B_WIDTH).astype(BF16)
    gqa = jnp.tile(p["g_qa"], A_HEADS).reshape(1, -1)
    gka = jnp.tile(p["g_ka"], A_HEADS).reshape(1, -1)
    gqb = _pad_heads(jnp.tile(p["g_qb"], B_HEADS), B_HEADS, QK_HEAD).reshape(1, -1)
    gkb = _pad_heads(jnp.tile(p["g_kb"], B_HEADS), B_HEADS, QK_HEAD).reshape(1, -1)
    seg64 = jnp.kron(jnp.eye(256 // HEAD_DIM, dtype=F32), jnp.ones((HEAD_DIM, HEAD_DIM), F32)).astype(BF16)
    seg128 = jnp.kron(jnp.eye(2, dtype=F32), jnp.ones((PAD_HEAD, PAD_HEAD), F32)).astype(BF16)

    def full(a):
        return pl.BlockSpec(a.shape, lambda i, b: (0,) * a.ndim)

    spare = jnp.zeros((B_HEADS, PAD_HEAD), F32).at[:, QK_HEAD].set(1.0).reshape(1, -1)
    qpad = -_score_bound(p["g_qb"], p["g_kb"], QK_HEAD) * spare
    consts = [p["g_norm1"].reshape(1, -1), wqkv, wlat, wkr, wqb, wkn, wv, gqa, gka,
              p["g_qlat"].reshape(1, -1), p["g_kvlat"].reshape(1, -1), gqb, gkb, seg64, seg128, qpad, spare]
    tok = lambda w: pl.BlockSpec((None, tm, w), lambda i, b: (b, i, 0))
    qw = B_HEADS * PAD_HEAD
    tok_shape = lambda w, dt=F32: jax.ShapeDtypeStruct((bt, s, w), dt)
    return pl.pallas_call(
        _inproj_kernel,
        grid=(s // tm, bt),
        in_specs=[tok(d), pl.BlockSpec((None, 6, d), lambda i, b: (b, 0, 0))]
        + [full(a) for a in consts]
        + [pl.BlockSpec((tm, 3 * LANES), lambda i, b: (i, 0))] * 2,
        out_specs=[tok(A_WIDTH), tok(A_WIDTH), tok(A_WIDTH),
                   pl.BlockSpec((None, qw, tm), lambda i, b: (b, 0, i)),
                   tok(qw),
                   pl.BlockSpec((None, None, B_WIDTH, tm), lambda i, b: (b, i, 0, 0))],
        out_shape=[tok_shape(A_WIDTH), tok_shape(A_WIDTH), tok_shape(A_WIDTH),
                   jax.ShapeDtypeStruct((bt, qw, s), BF16),
                   tok_shape(qw, BF16),
                   jax.ShapeDtypeStruct((bt, s // tm, B_WIDTH, tm), BF16)],
        compiler_params=_cparams("arbitrary", "arbitrary"),
        name="inproj",
    )(x, mod, *consts, ropea, ropeb)


def _rows(start, size, stride):
    return pl.ds(start, size) if stride == 1 else pl.ds(start, size, stride=stride)


def _dilated_kernel(bound_ref, q_ref, k_ref, v_ref, o_ref, stat_ref, *, bounded):
    s_len = q_ref.shape[0]
    first = lax.broadcasted_iota(I32, (TQ_A, LANES), 1) < HEAD_DIM
    for branch, (window, dil) in enumerate(sorted(DILATED_BRANCHES, key=lambda wd: -wd[1])):
        radius = window // (2 * dil)
        win = TQ_A + 2 * radius
        length = s_len // dil
        nq = length // TQ_A
        rel = lax.broadcasted_iota(I32, (TQ_A, win), 0) - lax.broadcasted_iota(I32, (TQ_A, win), 1)

        def group(g, carry, branch=branch, dil=dil, radius=radius, win=win, length=length, nq=nq, rel=rel):
            done = []
            for u in range(UNROLL_A):
                idx = g * UNROLL_A + u
                r = idx // nq
                q0 = (idx % nq) * TQ_A
                w0 = jnp.clip(q0 - radius, 0, length - win)
                qrows = _rows(r + dil * q0, TQ_A, dil)
                krows = _rows(r + dil * w0, win, dil)
                q = q_ref[qrows, :]
                kw = k_ref[krows, :].astype(BF16)
                vw = v_ref[krows, :].astype(BF16)
                valid = jnp.abs(rel + (q0 - w0)) <= radius
                outs, stats = [], []
                for keep in (first, jnp.logical_not(first)):
                    s = _nt_dot(jnp.where(keep, q, 0.0).astype(BF16), kw)
                    if bounded:
                        pexp = jnp.exp2(jnp.where(valid, s - bound_ref[0], NEG_BIG))
                        outs.append(_dot(pexp.astype(BF16), vw))
                        stats.append(jnp.sum(pexp, axis=-1, keepdims=True))
                    else:
                        s = jnp.where(valid, s, NEG_BIG)
                        m = jnp.max(s, axis=-1, keepdims=True)
                        pexp = jnp.exp2(s - m)
                        den = jnp.sum(pexp, axis=-1, keepdims=True)
                        outs.append(_dot(pexp.astype(BF16), vw) / den)
                        stats.append(m + jnp.log2(den))
                o_new = jnp.where(first, outs[0], outs[1])
                stat_new = jnp.where(first, stats[0], stats[1])
                if branch > 0:
                    o_old, stat_old = o_ref[qrows, :], stat_ref[qrows, :]
                    if bounded:
                        o_new, stat_new = o_old + o_new, stat_old + stat_new
                    else:
                        mx = jnp.maximum(stat_old, stat_new)
                        w_old, w_new = jnp.exp2(stat_old - mx), jnp.exp2(stat_new - mx)
                        den = w_old + w_new
                        o_new = (w_old * o_old + w_new * o_new) / den
                        stat_new = mx + jnp.log2(den)
                done.append((qrows, o_new, stat_new))
            for qrows, o_new, stat_new in done:
                o_ref[qrows, :] = o_new
                stat_ref[qrows, :] = stat_new
            return carry

        lax.fori_loop(0, dil * nq // UNROLL_A, group, 0)

    if bounded:
        def normalise(i, carry):
            rows = pl.ds(pl.multiple_of(i * NORM_ROWS_A, NORM_ROWS_A), NORM_ROWS_A)
            o_ref[rows, :] = o_ref[rows, :] / stat_ref[rows, :]
            return carry

        lax.fori_loop(0, s_len // NORM_ROWS_A, normalise, 0)


def _dilated_call(bounded, bound, qa, ka, va):
    bt, s, width = qa.shape
    spec = pl.BlockSpec((None, s, LANES), lambda b, j: (b, 0, j))
    return pl.pallas_call(
        functools.partial(_dilated_kernel, bounded=bounded),
        grid=(bt, width // LANES),
        in_specs=[pl.BlockSpec(memory_space=pltpu.SMEM), spec, spec, spec],
        out_specs=spec,
        out_shape=jax.ShapeDtypeStruct((bt, s, width), F32),
        scratch_shapes=[pltpu.VMEM((s, LANES), F32)],
        compiler_params=_cparams("arbitrary", "arbitrary"),
        name="dilated_bounded" if bounded else "dilated_online",
    )(bound.reshape(1), qa, ka, va)


def _dilated_attention(qa, ka, va, p):
    bound = _score_bound(p["g_qa"], p["g_ka"], HEAD_DIM)
    return lax.cond(_bound_is_safe(bound),
                    functools.partial(_dilated_call, True), functools.partial(_dilated_call, False),
                    bound, qa, ka, va)


def _latent_online_kernel(qt_ref, k_ref, vt_ref, o_ref):
    n_kv, _, tk = vt_ref.shape
    tq = qt_ref.shape[1]
    n_chunks = tk // KC_B
    qts = [qt_ref[h * PAD_HEAD:(h + 1) * PAD_HEAD, :] for h in range(2)]

    def scores(j, h):
        k0 = pl.multiple_of(j * tk, tk)
        return tuple(_dot(k_ref[pl.ds(k0 + c * KC_B, KC_B), h * PAD_HEAD:(h + 1) * PAD_HEAD], qts[h])
                     for c in range(n_chunks))

    def body(j, carry):
        nxt = jnp.minimum(j + 1, n_kv - 1)
        new = []
        for h in range(2):
            m, l, acc, sts = carry[h]
            sts_next = scores(nxt, h)
            for c in range(n_chunks):
                m_new = jnp.maximum(m, jnp.max(sts[c], axis=0, keepdims=True))
                alpha = jnp.exp2(m - m_new)
                pexp = jnp.exp2(sts[c] - m_new)
                l = alpha * l + jnp.sum(pexp, axis=0, keepdims=True)
                vt = vt_ref[j, h * V_HEAD:(h + 1) * V_HEAD, c * KC_B:(c + 1) * KC_B]
                acc = alpha * acc + _dot(vt, pexp.astype(BF16))
                m = m_new
            new.append((m, l, acc, sts_next))
        return tuple(new)

    init = tuple((jnp.full((1, tq), NEG_BIG, F32), jnp.zeros((1, tq), F32), jnp.zeros((V_HEAD, tq), F32),
                  scores(0, h)) for h in range(2))
    (_, l0, acc0, _), (_, l1, acc1, _) = lax.fori_loop(0, n_kv, body, init)
    ot = jnp.concatenate([acc0 / l0, acc1 / l1], axis=0)
    o_ref[...] = ot.T.astype(o_ref.dtype)


def _latent_bounded_kernel(qt_ref, k_ref, vt_ref, o_ref):
    n_kv, _, tk = vt_ref.shape
    tq = qt_ref.shape[1]
    n_chunks = tk // KC_FAST_B
    qts = [qt_ref[h * PAD_HEAD:(h + 1) * PAD_HEAD, :] for h in range(2)]
    group = min(KV_TILES_B, n_kv)

    def probs(jj, h):
        out = []
        for t in range(group):
            k0 = pl.multiple_of((jj * group + t) * tk, tk)
            for c in range(n_chunks):
                kt = k_ref[pl.ds(k0 + c * KC_FAST_B, KC_FAST_B), h * PAD_HEAD:(h + 1) * PAD_HEAD]
                pexp = jnp.exp2(_dot(kt, qts[h]))
                out.append((pexp.astype(BF16), jnp.sum(pexp, axis=0, keepdims=True)))
        return tuple(out)

    def body(jj, carry):
        nxt = jnp.minimum(jj + 1, n_kv // group - 1)
        new = []
        for h in range(2):
            l, acc, cur = carry[h]
            nxt_probs = probs(nxt, h)
            for t in range(group):
                for c in range(n_chunks):
                    pexp, psum = cur[t * n_chunks + c]
                    vt = vt_ref[jj * group + t, h * V_HEAD:(h + 1) * V_HEAD, c * KC_FAST_B:(c + 1) * KC_FAST_B]
                    acc = acc + _dot(vt, pexp)
                    l = l + psum
            new.append((l, acc, nxt_probs))
        return tuple(new)

    init = tuple((jnp.zeros((1, tq), F32), jnp.zeros((V_HEAD, tq), F32), probs(0, h)) for h in range(2))
    (l0, acc0, _), (l1, acc1, _) = lax.fori_loop(0, n_kv // group, body, init)
    ot = jnp.concatenate([acc0 / l0, acc1 / l1], axis=0)
    o_ref[...] = ot.T.astype(o_ref.dtype)


def _latent_call(bounded, qbt, kb, vbt):
    bt, s, _ = kb.shape
    pairs = B_HEADS // 2
    n_kv, tk = vbt.shape[1], vbt.shape[3]
    tq = min(TQ_B, s)
    return pl.pallas_call(
        _latent_bounded_kernel if bounded else _latent_online_kernel,
        grid=(bt, pairs, s // tq),
        in_specs=[
            pl.BlockSpec((None, 2 * PAD_HEAD, tq), lambda b, hp, i: (b, hp, i)),
            pl.BlockSpec((None, s, 2 * PAD_HEAD), lambda b, hp, i: (b, 0, hp)),
            pl.BlockSpec((None, n_kv, 2 * V_HEAD, tk), lambda b, hp, i: (b, 0, hp, 0)),
        ],
        out_specs=pl.BlockSpec((None, tq, 2 * V_HEAD), lambda b, hp, i: (b, i, hp)),
        out_shape=jax.ShapeDtypeStruct((bt, s, B_WIDTH), BF16),
        compiler_params=_cparams("arbitrary", "arbitrary", "arbitrary"),
        name="latent_bounded" if bounded else "latent_online",
    )(qbt, kb, vbt)


def _latent_attention(qbt, kb, vbt, p):
    safe = _bound_is_safe(_score_bound(p["g_qb"], p["g_kb"], QK_HEAD))
    return lax.cond(safe, functools.partial(_latent_call, True), functools.partial(_latent_call, False),
                    qbt, kb, vbt)


def _outproj_kernel(oa_ref, ob_ref, x_ref, mod_ref,
                    woa_ref, wob_ref, g2_ref, wrh_ref, wrl_ref, br_ref, tri_ref,
                    x1_ref, h2_ref, idx_ref, wts_ref, rank_ref, cnt_ref, base_ref):
    first_step = jnp.logical_and(pl.program_id(0) == 0, pl.program_id(1) == 0)

    @pl.when(first_step)
    def _():
        base_ref[...] = jnp.zeros_like(base_ref)

    mixed = _dot(oa_ref[...].astype(BF16), woa_ref[...]) + _dot(ob_ref[...], wob_ref[...])
    x1 = x_ref[...] + mod_ref[2:3, :] * mixed
    x1_ref[...] = x1
    d = x1.shape[1]
    h2 = x1 * lax.rsqrt(jnp.sum(x1 * x1, axis=-1, keepdims=True) * (1.0 / d) + EPS) * g2_ref[...]
    h2 = h2 * (1.0 + mod_ref[4:5, :]) + mod_ref[3:4, :]
    _store_token_rows(h2_ref, _pack_rows(h2))

    hh, hl = _split_bf16(h2)
    logits = _nt_dot(wrh_ref[...], hh) + _nt_dot(wrh_ref[...], hl) + _nt_dot(wrl_ref[...], hh)
    scores = _sigmoid(logits)
    biased = scores + br_ref[...]
    tm = biased.shape[1]
    sub = lax.broadcasted_iota(I32, (PER_GROUP, tm), 0)
    groups, gscore = [], []
    for g in range(N_GROUPS):
        bg = biased[g * PER_GROUP:(g + 1) * PER_GROUP, :]
        m1 = jnp.max(bg, axis=0, keepdims=True)
        i1 = jnp.min(jnp.where(bg == m1, sub, PER_GROUP), axis=0, keepdims=True)
        m2 = jnp.max(jnp.where(sub == i1, -jnp.inf, bg), axis=0, keepdims=True)
        groups.append(bg)
        gscore.append(m1 + m2)
    masked = []
    for g in range(N_GROUPS):
        ahead = jnp.zeros((1, tm), I32)
        for g2 in range(N_GROUPS):
            if g2 == g:
                continue
            beats = (gscore[g2] > gscore[g]) if g2 > g else (gscore[g2] >= gscore[g])
            ahead = ahead + beats.astype(I32)
        masked.append(jnp.where(ahead < TOPK_GROUPS, groups[g], -jnp.inf))
    cand = jnp.concatenate(masked, axis=0)
    eid = lax.broadcasted_iota(I32, (N_EXPERTS, tm), 0)
    chosen = jnp.zeros((N_EXPERTS, tm), F32)
    sel_idx, sel_w, sel_hot = [], [], []
    for _ in range(TOP_K):
        mv = jnp.max(cand, axis=0, keepdims=True)
        ie = jnp.min(jnp.where(cand == mv, eid, N_EXPERTS), axis=0, keepdims=True)
        hot = eid == ie
        sel_idx.append(ie)
        sel_hot.append(hot)
        sel_w.append(jnp.sum(jnp.where(hot, scores, 0.0), axis=0, keepdims=True))
        cand = jnp.where(hot, -jnp.inf, cand)
        chosen = chosen + hot.astype(F32)
    wsum = sel_w[0]
    for w in sel_w[1:]:
        wsum = wsum + w
    within = _dot(chosen.astype(BF16), tri_ref[...]) + base_ref[...]
    ranks = [jnp.sum(jnp.where(hot, within, 0.0), axis=0, keepdims=True) for hot in sel_hot]
    pad_i = [jnp.zeros((1, tm), I32)] * (8 - TOP_K)
    pad_f = [jnp.zeros((1, tm), F32)] * (8 - TOP_K)
    idx_ref[...] = jnp.concatenate(sel_idx + pad_i, axis=0)
    wts_ref[...] = jnp.concatenate([w / wsum * ROUTED_SCALE for w in sel_w] + pad_f, axis=0)
    rank_ref[...] = jnp.concatenate([r.astype(I32) for r in ranks] + pad_i, axis=0)
    base_ref[...] = base_ref[...] + jnp.sum(chosen, axis=1, keepdims=True)
    cnt_ref[...] = jnp.broadcast_to(base_ref[...], cnt_ref.shape)


def _outproj(oa, ob, x, mod, p):
    bt, s, d = x.shape
    tm = min(TM_OUT, s)
    n = bt * s
    woa = p["w_o"][:A_WIDTH].astype(BF16)
    wob = p["w_o"][A_WIDTH:].astype(BF16)
    wr_t = p["w_router"].T
    wrh, wrl = _split_bf16(wr_t)
    tri = (jnp.arange(tm)[:, None] < jnp.arange(tm)[None, :]).astype(BF16)
    consts = [woa, wob, p["g_norm2"].reshape(1, -1), wrh, wrl, p["b_router"].reshape(-1, 1), tri]
    tiles = s // tm
    tok = lambda w: pl.BlockSpec((None, tm, w), lambda b, i: (b, i, 0))
    full = lambda a: pl.BlockSpec(a.shape, lambda b, i: (0,) * a.ndim)
    flat = lambda w: pl.BlockSpec((tm, w), lambda b, i: (b * tiles + i, 0))
    slab = pl.BlockSpec((8, tm), lambda b, i: (0, b * tiles + i))
    return pl.pallas_call(
        _outproj_kernel,
        grid=(bt, tiles),
        in_specs=[tok(A_WIDTH), tok(B_WIDTH), tok(d), pl.BlockSpec((None, 6, d), lambda b, i: (b, 0, 0))]
        + [full(a) for a in consts],
        out_specs=[flat(d), pl.BlockSpec((tm * ROW_SUB, LANES), lambda b, i: (b * tiles + i, 0)),
                   slab, slab, slab, pl.BlockSpec((N_EXPERTS, LANES), lambda b, i: (0, 0))],
        out_shape=[jax.ShapeDtypeStruct((n, d), F32), jax.ShapeDtypeStruct((n * ROW_SUB, LANES), U32),
                   jax.ShapeDtypeStruct((8, n), I32), jax.ShapeDtypeStruct((8, n), F32),
                   jax.ShapeDtypeStruct((8, n), I32), jax.ShapeDtypeStruct((N_EXPERTS, LANES), F32)],
        scratch_shapes=[pltpu.VMEM((N_EXPERTS, 1), F32)],
        compiler_params=_cparams("arbitrary", "arbitrary"),
        name="outproj_router",
    )(oa, ob, x, mod, *consts)


def _dest_kernel(pstart_ref, idx_ref, rank_ref, dest_ref):
    idx = idx_ref[...]
    dest = rank_ref[...]
    for e in range(N_EXPERTS):
        dest = dest + jnp.where(idx == e, pstart_ref[e], 0)
    dest_ref[...] = dest * ROW_SUB


def _dest(pstart, idx_t, rank_t):
    n = idx_t.shape[1]
    tn = min(2048, n)
    slab = pl.BlockSpec((8, tn), lambda i, ps: (0, i))
    return pl.pallas_call(
        _dest_kernel,
        grid_spec=pltpu.PrefetchScalarGridSpec(
            num_scalar_prefetch=1, grid=(n // tn,), in_specs=[slab, slab], out_specs=slab),
        out_shape=jax.ShapeDtypeStruct((8, n), I32),
        compiler_params=_cparams("arbitrary"),
        name="moe_dest",
    )(pstart, idx_t, rank_t)


def _dispatch_kernel(pad_start_ref, pad_len_ref, nused_ref, dest_ref, h_ref, xs_ref, zero_ref, sem, pad_sem):
    tm = h_ref.shape[0] // ROW_SUB

    @pl.when(pl.program_id(0) == 0)
    def _():
        _zero_unrouted_slots(pad_start_ref, pad_len_ref, nused_ref, xs_ref, zero_ref, pad_sem,
                             xs_ref.shape[0] // zero_ref.shape[0])

    def slab(ref, first_row):
        return ref.at[pl.ds(pl.multiple_of(first_row, ROW_SUB), ROW_SUB), :]

    def issue(j, carry):
        for k in range(TOP_K):
            pltpu.make_async_copy(slab(h_ref, j * ROW_SUB), slab(xs_ref, dest_ref[k, j]), sem).start()
        return carry

    lax.fori_loop(0, tm, issue, 0, unroll=ISSUE_UNROLL)
    for k in range(TOP_K):
        pltpu.make_async_copy(h_ref, xs_ref.at[pl.ds(0, tm * ROW_SUB), :], sem).wait()


def _dispatch(pad_start, pad_len, n_used, dest_t, hp, n_slots):
    n = hp.shape[0] // ROW_SUB
    tm = min(TM_DISPATCH, n)
    return pl.pallas_call(
        _dispatch_kernel,
        grid_spec=pltpu.PrefetchScalarGridSpec(
            num_scalar_prefetch=3, grid=(n // tm,),
            in_specs=[
                pl.BlockSpec((8, tm), lambda i, ps, pn, nu: (0, i), memory_space=pltpu.SMEM),
                pl.BlockSpec((tm * ROW_SUB, LANES), lambda i, ps, pn, nu: (i, 0)),
            ],
            out_specs=pl.BlockSpec(memory_space=pl.ANY),
            scratch_shapes=[pltpu.VMEM((EXPERT_BLOCK * ROW_SUB, LANES), U32), pltpu.SemaphoreType.DMA(()),
                            pltpu.SemaphoreType.DMA(())]),
        out_shape=jax.ShapeDtypeStruct((n_slots * ROW_SUB, LANES), U32),
        compiler_params=_cparams("arbitrary"),
        name="moe_dispatch",
    )(pad_start, pad_len, n_used, dest_t, hp)


def _expert_kernel(be_ref, nused_ref, xs_ref, wg_ref, wu_ref, wd_ref, ys_ref):
    i = pl.program_id(0)

    @pl.when(i < nused_ref[0])
    def _():
        xb = _load_token_rows(xs_ref).astype(BF16)
        a = _dot(xb, wg_ref[...])
        u = _dot(xb, wu_ref[...])
        _store_token_rows(ys_ref, _pack_rows(_dot((_silu(a) * u).astype(BF16), wd_ref[...])))

    @pl.when(i >= nused_ref[0])
    def _():
        ys_ref[...] = jnp.zeros_like(ys_ref)


def _experts(block_e, n_used, xs, wg, wu, wd):
    n_slots = xs.shape[0] // ROW_SUB
    n_blocks = n_slots // EXPERT_BLOCK
    d, de = wg.shape[1], wg.shape[2]
    rows = pl.BlockSpec((EXPERT_BLOCK * ROW_SUB, LANES), lambda i, be, nu: (i, 0))
    used_rows = pl.BlockSpec((EXPERT_BLOCK * ROW_SUB, LANES), lambda i, be, nu: (jnp.minimum(i, nu[0] - 1), 0))
    return pl.pallas_call(
        _expert_kernel,
        grid_spec=pltpu.PrefetchScalarGridSpec(
            num_scalar_prefetch=2, grid=(n_blocks,),
            in_specs=[used_rows,
                      pl.BlockSpec((None, d, de), lambda i, be, nu: (be[i], 0, 0)),
                      pl.BlockSpec((None, d, de), lambda i, be, nu: (be[i], 0, 0)),
                      pl.BlockSpec((None, de, d), lambda i, be, nu: (be[i], 0, 0))],
            out_specs=rows),
        out_shape=jax.ShapeDtypeStruct((n_slots * ROW_SUB, LANES), U32),
        compiler_params=_cparams("arbitrary"),
        name="moe_experts",
    )(block_e, n_used, xs, wg, wu, wd)


def _combine_kernel(dest_ref, dest_next_ref, h_ref, x1_ref, mod_ref, wts_ref, wsg_ref, wsu_ref, wsd_ref, ys_ref,
                    out_ref, buf_ref, sems):
    i = pl.program_id(0)
    slot = lax.rem(i, 2)
    tm = h_ref.shape[0] // ROW_SUB

    def gather(idx_ref, into):
        def issue(j, carry):
            for k in range(TOP_K):
                src = ys_ref.at[pl.ds(pl.multiple_of(idx_ref[k, j], ROW_SUB), ROW_SUB), :]
                dst = buf_ref.at[into, k, pl.ds(pl.multiple_of(j * ROW_SUB, ROW_SUB), ROW_SUB), :]
                pltpu.make_async_copy(src, dst, sems.at[into]).start()
            return carry

        lax.fori_loop(0, tm, issue, 0, unroll=ISSUE_UNROLL)

    @pl.when(i == 0)
    def _():
        gather(dest_ref, 0)

    @pl.when(i + 1 < pl.num_programs(0))
    def _():
        gather(dest_next_ref, 1 - slot)

    hb = _load_token_rows(h_ref).astype(BF16)
    act = _silu(_dot(hb, wsg_ref[...])) * _dot(hb, wsu_ref[...])
    y = _dot(act.astype(BF16), wsd_ref[...])
    for k in range(TOP_K):
        pltpu.make_async_copy(ys_ref.at[pl.ds(0, tm * ROW_SUB), :], buf_ref.at[slot, k], sems.at[slot]).wait()
    wts = wts_ref[...]
    for k in range(TOP_K):
        y = y + wts[:, k:k + 1] * _load_token_rows(buf_ref.at[slot, k])
    out_ref[...] = x1_ref[...] + mod_ref[5:6, :] * y


def _combine(dest_t, hp, x1, mod, wts, ys, p, bt, s):
    n, d = x1.shape
    tm = min(TM_COMBINE, s)
    tiles = s // tm
    steps = n // tm
    wsg, wsu, wsd = (p[k].astype(BF16) for k in ("w_s_gate", "w_s_up", "w_s_down"))
    full = lambda a: pl.BlockSpec(a.shape, lambda i: (0,) * a.ndim)
    rows = pl.BlockSpec((tm, d), lambda i: (i, 0))
    out = pl.pallas_call(
        _combine_kernel,
        grid=(steps,),
        in_specs=[
            pl.BlockSpec((8, tm), lambda i: (0, i), memory_space=pltpu.SMEM),
            pl.BlockSpec((8, tm), lambda i: (0, jnp.minimum(i + 1, steps - 1)), memory_space=pltpu.SMEM),
            pl.BlockSpec((tm * ROW_SUB, LANES), lambda i: (i, 0)),
            rows,
            pl.BlockSpec((None, 6, d), lambda i: (i // tiles, 0, 0)),
            pl.BlockSpec((tm, 8), lambda i: (i, 0)),
            full(wsg), full(wsu), full(wsd),
            pl.BlockSpec(memory_space=pl.ANY),
        ],
        out_specs=rows,
        out_shape=jax.ShapeDtypeStruct((n, d), F32),
        scratch_shapes=[pltpu.VMEM((2, TOP_K, tm * ROW_SUB, LANES), U32), pltpu.SemaphoreType.DMA((2,))],
        compiler_params=_cparams("arbitrary"),
        name="moe_combine",
    )(dest_t, dest_t, hp, x1, mod, wts, wsg, wsu, wsd, ys)
    return out.reshape(bt, s, d)


def _layer(x, c, p):
    bt, s, d = x.shape
    n = bt * s
    mod = _adaln(c, p["w_ada"], p["b_ada"]).reshape(bt, 6, d)
    qa, ka, va, qb, kb, vb = _inproj(x, mod, p)
    oa = _dilated_attention(qa, ka, va, p)
    ob = _latent_attention(qb, kb, vb, p)
    x1, hp, idx_t, wts_t, rank_t, cnt = _outproj(oa, ob, x, mod, p)

    counts = cnt[:, 0].astype(I32)
    padded = (counts + EXPERT_BLOCK - 1) // EXPERT_BLOCK * EXPERT_BLOCK
    pend = jnp.cumsum(padded)
    pstart = pend - padded
    n_blocks = (n * TOP_K + N_EXPERTS * (EXPERT_BLOCK - 1)) // EXPERT_BLOCK
    block_first = jnp.arange(n_blocks, dtype=I32) * EXPERT_BLOCK
    block_e = jnp.minimum(jnp.sum((pend[None, :] <= block_first[:, None]).astype(I32), axis=1), N_EXPERTS - 1)
    n_used = (pend[-1:] // EXPERT_BLOCK).astype(I32)

    dest_t = _dest(pstart.astype(I32), idx_t, rank_t)
    xs = _dispatch(((pstart + counts) * ROW_SUB).astype(I32), (padded - counts).astype(I32), n_used, dest_t, hp,
                   n_blocks * EXPERT_BLOCK)
    ys = _experts(block_e, n_used, xs, p["w_e_gate"].astype(BF16), p["w_e_up"].astype(BF16),
                  p["w_e_down"].astype(BF16))
    return _combine(dest_t, hp, x1, mod, wts_t.T, ys, p, bt, s)


_PARAM_NAMES = ("w_ada", "b_ada", "g_norm1", "w_in", "g_qa", "g_ka", "g_qlat", "w_qb", "g_kvlat", "w_kvb",
                "g_qb", "g_kb", "w_o", "g_norm2", "w_router", "b_router", "w_e_gate", "w_e_up", "w_e_down",
                "w_s_gate", "w_s_up", "w_s_down")


def kernel(x_prompt, x_sample, c_prompt, c_sample, w_ada, b_ada, g_norm1, w_in, g_qa, g_ka, g_qlat, w_qb,
           g_kvlat, w_kvb, g_qb, g_kb, w_o, g_norm2, w_router, b_router, w_e_gate, w_e_up, w_e_down,
           w_s_gate, w_s_up, w_s_down):
    stacked = (w_ada, b_ada, g_norm1, w_in, g_qa, g_ka, g_qlat, w_qb, g_kvlat, w_kvb, g_qb, g_kb, w_o,
               g_norm2, w_router, b_router, w_e_gate, w_e_up, w_e_down, w_s_gate, w_s_up, w_s_down)
    y_prompt, y_sample = x_prompt, x_sample
    for layer in range(w_ada.shape[0]):
        p = {name: w[layer] for name, w in zip(_PARAM_NAMES, stacked)}
        y_prompt = _layer(y_prompt, c_prompt, p)
        y_sample = _layer(y_sample, c_sample, p)
    return (y_prompt, y_sample)
```

```python
import functools

import jax
import jax.numpy as jnp
from jax import lax
from jax.experimental import pallas as pl
from jax.experimental.pallas import tpu as pltpu

F32 = jnp.float32
BF16 = jnp.bfloat16
I32 = jnp.int32
U32 = jnp.uint32

HEAD_DIM = 64
A_HEADS = 12
A_WIDTH = A_HEADS * HEAD_DIM
DILATED_BRANCHES = ((128, 1), (512, 4), (2048, 16))
ROT_DIMS_A = HEAD_DIM // 4
B_HEADS = 4
QK_NOPE = 64
QK_ROPE = 32
QK_HEAD = QK_NOPE + QK_ROPE
V_HEAD = 64
Q_LORA = 256
KV_LORA = 256
B_WIDTH = B_HEADS * V_HEAD
ROPE_THETA = 500000.0
N_EXPERTS = 64
TOP_K = 6
N_GROUPS = 8
TOPK_GROUPS = 4
PER_GROUP = N_EXPERTS // N_GROUPS
ROUTED_SCALE = 2.5
EXPERT_BLOCK = 512
EPS = 1e-6

LOG2E = 1.4426950408889634
BOUND_SLACK = 1.02
EXP2_SAFE_RANGE = 120.0

LANES = 128
PAD_HEAD = 128
NEG_BIG = -1e30
VMEM_LIMIT = 56 * 1024 * 1024

TM_IN = 512
TM_OUT = 512
TQ_A = 128
UNROLL_A = 16
NORM_ROWS_A = 512
TQ_B = 512
KC_B = 128
KC_FAST_B = 256
KV_TILES_B = 2
TM_DISPATCH = 512
TM_COMBINE = 256
ISSUE_UNROLL = 4
ROW_SUB = 4


def _nt_dot(a, b):
    return lax.dot_general(a, b, (((1,), (1,)), ((), ())), preferred_element_type=F32)


def _dot(a, b):
    return jnp.dot(a, b, preferred_element_type=F32)


def _split_bf16(x):
    hi = x.astype(BF16)
    lo = (x - hi.astype(F32)).astype(BF16)
    return hi, lo


def _sigmoid(x):
    return 1.0 / (1.0 + jnp.exp(-x))


def _silu(x):
    return x * _sigmoid(x)


def _pack_rows(x):
    w = x.shape[1] // 2
    lo = lax.bitcast_convert_type(x[:, :w].astype(BF16).astype(F32), U32)
    hi = lax.bitcast_convert_type(x[:, w:].astype(BF16).astype(F32), U32)
    return (lo >> 16) | (hi & jnp.uint32(0xFFFF0000))


def _unpack_words(words):
    lo = lax.bitcast_convert_type(words << 16, F32)
    hi = lax.bitcast_convert_type(words & jnp.uint32(0xFFFF0000), F32)
    return lo, hi


def _store_token_rows(ref, words):
    m = ref.shape[0] // ROW_SUB
    for c in range(ROW_SUB):
        ref[pl.ds(c, m, stride=ROW_SUB), :] = words[:, c * LANES:(c + 1) * LANES]


def _load_token_rows(ref):
    m = ref.shape[0] // ROW_SUB
    halves = [_unpack_words(ref[pl.ds(c, m, stride=ROW_SUB), :]) for c in range(ROW_SUB)]
    return jnp.concatenate([lo for lo, _ in halves] + [hi for _, hi in halves], axis=1)


def _zero_unrouted_slots(pad_start_ref, pad_len_ref, nused_ref, xs_ref, zero_ref, pad_sem, n_blocks):
    zero_ref[...] = jnp.zeros_like(zero_ref)
    block_rows = zero_ref.shape[0]

    def slab(ref, first_row):
        return ref.at[pl.ds(pl.multiple_of(first_row, ROW_SUB), ROW_SUB), :]

    def pad_copy(e, r):
        return pltpu.make_async_copy(slab(zero_ref, 0), slab(xs_ref, pad_start_ref[e] + r * ROW_SUB), pad_sem)

    def per_expert(e, carry):
        lax.fori_loop(0, pad_len_ref[e], lambda r, c: (pad_copy(e, r).start(), c)[1], 0)
        lax.fori_loop(0, pad_len_ref[e], lambda r, c: (pad_copy(e, r).wait(), c)[1], 0)
        return carry

    lax.fori_loop(0, N_EXPERTS, per_expert, 0)

    def block_copy(b):
        rows = pl.ds(pl.multiple_of(b * block_rows, block_rows), block_rows)
        return pltpu.make_async_copy(zero_ref, xs_ref.at[rows, :], pad_sem)

    lax.fori_loop(nused_ref[0], n_blocks, lambda b, c: (block_copy(b).start(), c)[1], 0)
    lax.fori_loop(nused_ref[0], n_blocks, lambda b, c: (block_copy(b).wait(), c)[1], 0)


def _cparams(*sem):
    return pltpu.CompilerParams(dimension_semantics=sem, vmem_limit_bytes=VMEM_LIMIT)


def _adaln_kernel(c_ref, w_ref, b_ref, o_ref):
    s = _silu(c_ref[...])
    o_ref[...] = _dot(s.astype(BF16), w_ref[...].astype(BF16)) + b_ref[...]


def _adaln(c, w_ada, b_ada):
    bt, d = c.shape
    n_chunks = w_ada.shape[1] // d
    return pl.pallas_call(
        _adaln_kernel,
        grid=(n_chunks,),
        in_specs=[
            pl.BlockSpec((bt, d), lambda j: (0, 0)),
            pl.BlockSpec((d, d), lambda j: (0, j)),
            pl.BlockSpec((1, d), lambda j: (0, j)),
        ],
        out_specs=pl.BlockSpec((bt, d), lambda j: (0, j)),
        out_shape=jax.ShapeDtypeStruct((bt, w_ada.shape[1]), F32),
        compiler_params=_cparams("arbitrary"),
        name="adaln",
    )(c, w_ada, b_ada.reshape(1, -1))


def _seg_sumsq(x, seg_ref):
    sq = (x * x).astype(BF16)
    width = seg_ref.shape[0]
    outs = [_dot(sq[:, c * width:(c + 1) * width], seg_ref[...]) for c in range(x.shape[1] // width)]
    return outs[0] if len(outs) == 1 else jnp.concatenate(outs, axis=1)


def _tile_lanes(t, reps):
    return t if reps == 1 else jnp.concatenate([t] * reps, axis=1)


def _rope_lanes(x, tab_ref, shift, reps):
    width = x.shape[1]
    cos = _tile_lanes(tab_ref[:, 0:LANES], reps)
    sin_p = _tile_lanes(tab_ref[:, LANES:2 * LANES], reps)
    sin_m = _tile_lanes(tab_ref[:, 2 * LANES:3 * LANES], reps)
    return x * cos + pltpu.roll(x, shift, 1) * sin_p + pltpu.roll(x, width - shift, 1) * sin_m


def _inproj_kernel(x_ref, mod_ref, g1_ref, wqkv_ref, wlat_ref, wkr_ref, wqb_ref, wkn_ref, wv_ref,
                   gqa_ref, gka_ref, gql_ref, gkvl_ref, gqb_ref, gkb_ref, seg64_ref, seg128_ref, qpad_ref, kpad_ref,
                   ropea_ref, ropeb_ref,
                   qa_ref, ka_ref, va_ref, qb_ref, kb_ref, vb_ref):
    x = x_ref[...]
    d = x.shape[1]
    h = x * lax.rsqrt(jnp.sum(x * x, axis=-1, keepdims=True) * (1.0 / d) + EPS) * g1_ref[...]
    h = h * (1.0 + mod_ref[1:2, :]) + mod_ref[0:1, :]
    hb = h.astype(BF16)

    z = _dot(hb, wqkv_ref[...])
    q = z[:, 0:A_WIDTH]
    k = z[:, A_WIDTH:2 * A_WIDTH]
    q = q * lax.rsqrt(_seg_sumsq(q, seg64_ref) * (1.0 / HEAD_DIM) + EPS) * gqa_ref[...]
    k = k * lax.rsqrt(_seg_sumsq(k, seg64_ref) * (1.0 / HEAD_DIM) + EPS) * gka_ref[...]
    reps_a = A_WIDTH // LANES
    q = _rope_lanes(q, ropea_ref, ROT_DIMS_A // 2, reps_a) * (HEAD_DIM ** -0.5 * LOG2E)
    k = _rope_lanes(k, ropea_ref, ROT_DIMS_A // 2, reps_a)
    qa_ref[...] = q
    ka_ref[...] = k
    va_ref[...] = z[:, 2 * A_WIDTH:3 * A_WIDTH]

    zl = _dot(hb, wlat_ref[...])
    ql = zl[:, 0:Q_LORA]
    kvl = zl[:, Q_LORA:Q_LORA + KV_LORA]
    ql = ql * lax.rsqrt(jnp.sum(ql * ql, axis=-1, keepdims=True) * (1.0 / Q_LORA) + EPS) * gql_ref[...]
    kvl = kvl * lax.rsqrt(jnp.sum(kvl * kvl, axis=-1, keepdims=True) * (1.0 / KV_LORA) + EPS) * gkvl_ref[...]
    qlb = ql.astype(BF16)
    kvlb = kvl.astype(BF16)
    reps_b = B_HEADS
    qb = _dot(qlb, wqb_ref[...])
    qb = qb * lax.rsqrt(_seg_sumsq(qb, seg128_ref) * (1.0 / QK_HEAD) + EPS) * gqb_ref[...]
    qb = _rope_lanes(qb, ropeb_ref, QK_ROPE // 2, reps_b) * (QK_HEAD ** -0.5 * LOG2E) + qpad_ref[...]
    qb_ref[...] = qb.T.astype(BF16)
    kr = _dot(hb, wkr_ref[...])
    kb = _dot(kvlb, wkn_ref[...]) + _tile_lanes(kr, reps_b)
    kb = kb * lax.rsqrt(_seg_sumsq(kb, seg128_ref) * (1.0 / QK_HEAD) + EPS) * gkb_ref[...]
    kb = _rope_lanes(kb, ropeb_ref, QK_ROPE // 2, reps_b) + kpad_ref[...]
    kb_ref[...] = kb.astype(BF16)
    vb_ref[...] = _dot(kvlb, wv_ref[...]).T.astype(BF16)


def _score_bound(g_q, g_k, head_dim):
    bound = LOG2E * head_dim ** 0.5 * jnp.max(jnp.abs(g_q)) * jnp.max(jnp.abs(g_k))
    return bound * BOUND_SLACK


def _bound_is_safe(bound):
    return 2.0 * bound < EXP2_SAFE_RANGE


def _rope_table(seq, rot, lane_of_first, period):
    half = rot // 2
    inv = ROPE_THETA ** (-jnp.arange(half, dtype=F32) * 2.0 / rot)
    ang = jnp.arange(seq).astype(F32)[:, None] * inv[None, :]
    cos, sin = jnp.cos(ang), jnp.sin(ang)
    one = jnp.ones((seq, period), F32)
    zero = jnp.zeros((seq, period), F32)
    a, b = lane_of_first, lane_of_first + half
    cos_t = one.at[:, a:a + half].set(cos).at[:, b:b + half].set(cos)
    sin_p = zero.at[:, b:b + half].set(sin)
    sin_m = zero.at[:, a:a + half].set(-sin)
    reps = LANES // period
    return jnp.concatenate([jnp.tile(t, (1, reps)) for t in (cos_t, sin_p, sin_m)], axis=1)


def _pad_heads(w, heads, width, offset=0):
    lead = w.shape[:-1]
    w = w.reshape(lead + (heads, width))
    pad = [(0, 0)] * len(lead) + [(0, 0), (offset, PAD_HEAD - width - offset)]
    return jnp.pad(w, pad).reshape(lead + (heads * PAD_HEAD,))


def _inproj(x, mod, p):
    bt, s, d = x.shape
    tm = min(TM_IN, s)
    ropea = _rope_table(s, ROT_DIMS_A, 0, HEAD_DIM)
    ropeb = _rope_table(s, QK_ROPE, QK_NOPE, PAD_HEAD)
    w_in = p["w_in"]
    wqkv = w_in[:, 0:3 * A_WIDTH].astype(BF16)
    wlat = w_in[:, 3 * A_WIDTH:3 * A_WIDTH + Q_LORA + KV_LORA].astype(BF16)
    wkr = _pad_heads(w_in[:, 3 * A_WIDTH + Q_LORA + KV_LORA:], 1, QK_ROPE, QK_NOPE).astype(BF16)
    wqb = _pad_heads(p["w_qb"], B_HEADS, QK_HEAD).astype(BF16)
    wkv = p["w_kvb"].reshape(KV_LORA, B_HEADS, QK_NOPE + V_HEAD)
    wkn = _pad_heads(wkv[:, :, :QK_NOPE].reshape(KV_LORA, B_HEADS * QK_NOPE), B_HEADS, QK_NOPE).astype(BF16)
    wv = wkv[:, :, QK_NOPE:].reshape(KV_LORA, B_WIDTH).astype(BF16)
    gqa = jnp.tile(p["g_qa"], A_HEADS).reshape(1, -1)
    gka = jnp.tile(p["g_ka"], A_HEADS).reshape(1, -1)
    gqb = _pad_heads(jnp.tile(p["g_qb"], B_HEADS), B_HEADS, QK_HEAD).reshape(1, -1)
    gkb = _pad_heads(jnp.tile(p["g_kb"], B_HEADS), B_HEADS, QK_HEAD).reshape(1, -1)
    seg64 = jnp.kron(jnp.eye(256 // HEAD_DIM, dtype=F32), jnp.ones((HEAD_DIM, HEAD_DIM), F32)).astype(BF16)
    seg128 = jnp.kron(jnp.eye(2, dtype=F32), jnp.ones((PAD_HEAD, PAD_HEAD), F32)).astype(BF16)

    def full(a):
        return pl.BlockSpec(a.shape, lambda i, b: (0,) * a.ndim)

    spare = jnp.zeros((B_HEADS, PAD_HEAD), F32).at[:, QK_HEAD].set(1.0).reshape(1, -1)
    qpad = -_score_bound(p["g_qb"], p["g_kb"], QK_HEAD) * spare
    consts = [p["g_norm1"].reshape(1, -1), wqkv, wlat, wkr, wqb, wkn, wv, gqa, gka,
              p["g_qlat"].reshape(1, -1), p["g_kvlat"].reshape(1, -1), gqb, gkb, seg64, seg128, qpad, spare]
    tok = lambda w: pl.BlockSpec((None, tm, w), lambda i, b: (b, i, 0))
    qw = B_HEADS * PAD_HEAD
    tok_shape = lambda w, dt=F32: jax.ShapeDtypeStruct((bt, s, w), dt)
    return pl.pallas_call(
        _inproj_kernel,
        grid=(s // tm, bt),
        in_specs=[tok(d), pl.BlockSpec((None, 6, d), lambda i, b: (b, 0, 0))]
        + [full(a) for a in consts]
        + [pl.BlockSpec((tm, 3 * LANES), lambda i, b: (i, 0))] * 2,
        out_specs=[tok(A_WIDTH), tok(A_WIDTH), tok(A_WIDTH),
                   pl.BlockSpec((None, qw, tm), lambda i, b: (b, 0, i)),
                   tok(qw),
                   pl.BlockSpec((None, None, B_WIDTH, tm), lambda i, b: (b, i, 0, 0))],
        out_shape=[tok_shape(A_WIDTH), tok_shape(A_WIDTH), tok_shape(A_WIDTH),
                   jax.ShapeDtypeStruct((bt, qw, s), BF16),
                   tok_shape(qw, BF16),
                   jax.ShapeDtypeStruct((bt, s // tm, B_WIDTH, tm), BF16)],
        compiler_params=_cparams("arbitrary", "arbitrary"),
        name="inproj",
    )(x, mod, *consts, ropea, ropeb)


def _rows(start, size, stride):
    return pl.ds(start, size) if stride == 1 else pl.ds(start, size, stride=stride)


def _dilated_kernel(bound_ref, q_ref, k_ref, v_ref, o_ref, stat_ref, bias_ref, *, bounded):
    s_len = q_ref.shape[0]
    first = lax.broadcasted_iota(I32, (TQ_A, LANES), 1) < HEAD_DIM
    for branch, (window, dil) in enumerate(sorted(DILATED_BRANCHES, key=lambda wd: -wd[1])):
        radius = window // (2 * dil)
        win = TQ_A + 2 * radius
        length = s_len // dil
        nq = length // TQ_A
        rel = lax.broadcasted_iota(I32, (TQ_A, win), 0) - lax.broadcasted_iota(I32, (TQ_A, win), 1)
        shift = bound_ref[0] if bounded else 0.0
        for case in range(3):
            bias_ref[case] = jnp.where(jnp.abs(rel + case * radius) <= radius, -shift, NEG_BIG)

        def group(g, carry, branch=branch, dil=dil, radius=radius, win=win, length=length, nq=nq):
            done = []
            for u in range(UNROLL_A):
                idx = g * UNROLL_A + u
                r = idx // nq
                q0 = (idx % nq) * TQ_A
                w0 = jnp.clip(q0 - radius, 0, length - win)
                qrows = _rows(r + dil * q0, TQ_A, dil)
                krows = _rows(r + dil * w0, win, dil)
                q = q_ref[qrows, :]
                kw = k_ref[krows, :].astype(BF16)
                vw = v_ref[krows, :].astype(BF16)
                bias = bias_ref[(q0 - w0) // radius]
                outs, stats = [], []
                for keep in (first, jnp.logical_not(first)):
                    s = _nt_dot(jnp.where(keep, q, 0.0).astype(BF16), kw)
                    if bounded:
                        pexp = jnp.exp2(s + bias)
                        outs.append(_dot(pexp.astype(BF16), vw))
                        stats.append(jnp.sum(pexp, axis=-1, keepdims=True))
                    else:
                        s = s + bias
                        m = jnp.max(s, axis=-1, keepdims=True)
                        pexp = jnp.exp2(s - m)
                        den = jnp.sum(pexp, axis=-1, keepdims=True)
                        outs.append(_dot(pexp.astype(BF16), vw) / den)
                        stats.append(m + jnp.log2(den))
                o_new = jnp.where(first, outs[0], outs[1])
                stat_new = jnp.where(first, stats[0], stats[1])
                if branch > 0:
                    o_old, stat_old = o_ref[qrows, :], stat_ref[qrows, :]
                    if bounded:
                        o_new, stat_new = o_old + o_new, stat_old + stat_new
                    else:
                        mx = jnp.maximum(stat_old, stat_new)
                        w_old, w_new = jnp.exp2(stat_old - mx), jnp.exp2(stat_new - mx)
                        den = w_old + w_new
                        o_new = (w_old * o_old + w_new * o_new) / den
                        stat_new = mx + jnp.log2(den)
                done.append((qrows, o_new, stat_new))
            for qrows, o_new, stat_new in done:
                o_ref[qrows, :] = o_new
                stat_ref[qrows, :] = stat_new
            return carry

        lax.fori_loop(0, dil * nq // UNROLL_A, group, 0)

    if bounded:
        def normalise(i, carry):
            rows = pl.ds(pl.multiple_of(i * NORM_ROWS_A, NORM_ROWS_A), NORM_ROWS_A)
            o_ref[rows, :] = o_ref[rows, :] / stat_ref[rows, :]
            return carry

        lax.fori_loop(0, s_len // NORM_ROWS_A, normalise, 0)


def _dilated_call(bounded, bound, qa, ka, va):
    bt, s, width = qa.shape
    radii = {window // (2 * dil) for window, dil in DILATED_BRANCHES}
    assert len(radii) == 1, "the score-mask scratch assumes one window radius for all branches"
    radius = radii.pop()
    spec = pl.BlockSpec((None, s, LANES), lambda b, j: (b, 0, j))
    return pl.pallas_call(
        functools.partial(_dilated_kernel, bounded=bounded),
        grid=(bt, width // LANES),
        in_specs=[pl.BlockSpec(memory_space=pltpu.SMEM), spec, spec, spec],
        out_specs=spec,
        out_shape=jax.ShapeDtypeStruct((bt, s, width), F32),
        scratch_shapes=[pltpu.VMEM((s, LANES), F32), pltpu.VMEM((3, TQ_A, TQ_A + 2 * radius), F32)],
        compiler_params=_cparams("arbitrary", "arbitrary"),
        name="dilated_bounded" if bounded else "dilated_online",
    )(bound.reshape(1), qa, ka, va)


def _dilated_attention(qa, ka, va, p):
    bound = _score_bound(p["g_qa"], p["g_ka"], HEAD_DIM)
    return lax.cond(_bound_is_safe(bound),
                    functools.partial(_dilated_call, True), functools.partial(_dilated_call, False),
                    bound, qa, ka, va)


def _latent_online_kernel(qt_ref, k_ref, vt_ref, o_ref):
    n_kv, _, tk = vt_ref.shape
    tq = qt_ref.shape[1]
    n_chunks = tk // KC_B
    qts = [qt_ref[h * PAD_HEAD:(h + 1) * PAD_HEAD, :] for h in range(2)]

    def scores(j, h):
        k0 = pl.multiple_of(j * tk, tk)
        return tuple(_dot(k_ref[pl.ds(k0 + c * KC_B, KC_B), h * PAD_HEAD:(h + 1) * PAD_HEAD], qts[h])
                     for c in range(n_chunks))

    def body(j, carry):
        nxt = jnp.minimum(j + 1, n_kv - 1)
        new = []
        for h in range(2):
            m, l, acc, sts = carry[h]
            sts_next = scores(nxt, h)
            for c in range(n_chunks):
                m_new = jnp.maximum(m, jnp.max(sts[c], axis=0, keepdims=True))
                alpha = jnp.exp2(m - m_new)
                pexp = jnp.exp2(sts[c] - m_new)
                l = alpha * l + jnp.sum(pexp, axis=0, keepdims=True)
                vt = vt_ref[j, h * V_HEAD:(h + 1) * V_HEAD, c * KC_B:(c + 1) * KC_B]
                acc = alpha * acc + _dot(vt, pexp.astype(BF16))
                m = m_new
            new.append((m, l, acc, sts_next))
        return tuple(new)

    init = tuple((jnp.full((1, tq), NEG_BIG, F32), jnp.zeros((1, tq), F32), jnp.zeros((V_HEAD, tq), F32),
                  scores(0, h)) for h in range(2))
    (_, l0, acc0, _), (_, l1, acc1, _) = lax.fori_loop(0, n_kv, body, init)
    ot = jnp.concatenate([acc0 / l0, acc1 / l1], axis=0)
    o_ref[...] = ot.T.astype(o_ref.dtype)


def _latent_bounded_kernel(qt_ref, k_ref, vt_ref, o_ref):
    n_kv, _, tk = vt_ref.shape
    tq = qt_ref.shape[1]
    n_chunks = tk // KC_FAST_B
    qts = [qt_ref[h * PAD_HEAD:(h + 1) * PAD_HEAD, :] for h in range(2)]
    group = min(KV_TILES_B, n_kv)

    def probs(jj, h):
        out = []
        for t in range(group):
            k0 = pl.multiple_of((jj * group + t) * tk, tk)
            for c in range(n_chunks):
                kt = k_ref[pl.ds(k0 + c * KC_FAST_B, KC_FAST_B), h * PAD_HEAD:(h + 1) * PAD_HEAD]
                pexp = jnp.exp2(_dot(kt, qts[h]))
                out.append((pexp.astype(BF16), jnp.sum(pexp, axis=0, keepdims=True)))
        return tuple(out)

    def body(jj, carry):
        nxt = jnp.minimum(jj + 1, n_kv // group - 1)
        new = []
        for h in range(2):
            l, acc, cur = carry[h]
            nxt_probs = probs(nxt, h)
            for t in range(group):
                for c in range(n_chunks):
                    pexp, psum = cur[t * n_chunks + c]
                    vt = vt_ref[jj * group + t, h * V_HEAD:(h + 1) * V_HEAD, c * KC_FAST_B:(c + 1) * KC_FAST_B]
                    acc = acc + _dot(vt, pexp)
                    l = l + psum
            new.append((l, acc, nxt_probs))
        return tuple(new)

    init = tuple((jnp.zeros((1, tq), F32), jnp.zeros((V_HEAD, tq), F32), probs(0, h)) for h in range(2))
    (l0, acc0, _), (l1, acc1, _) = lax.fori_loop(0, n_kv // group, body, init)
    ot = jnp.concatenate([acc0 / l0, acc1 / l1], axis=0)
    o_ref[...] = ot.T.astype(o_ref.dtype)


def _latent_call(bounded, qbt, kb, vbt):
    bt, s, _ = kb.shape
    pairs = B_HEADS // 2
    n_kv, tk = vbt.shape[1], vbt.shape[3]
    tq = min(TQ_B, s)
    return pl.pallas_call(
        _latent_bounded_kernel if bounded else _latent_online_kernel,
        grid=(bt, pairs, s // tq),
        in_specs=[
            pl.BlockSpec((None, 2 * PAD_HEAD, tq), lambda b, hp, i: (b, hp, i)),
            pl.BlockSpec((None, s, 2 * PAD_HEAD), lambda b, hp, i: (b, 0, hp)),
            pl.BlockSpec((None, n_kv, 2 * V_HEAD, tk), lambda b, hp, i: (b, 0, hp, 0)),
        ],
        out_specs=pl.BlockSpec((None, tq, 2 * V_HEAD), lambda b, hp, i: (b, i, hp)),
        out_shape=jax.ShapeDtypeStruct((bt, s, B_WIDTH), BF16),
        compiler_params=_cparams("arbitrary", "arbitrary", "arbitrary"),
        name="latent_bounded" if bounded else "latent_online",
    )(qbt, kb, vbt)


def _latent_attention(qbt, kb, vbt, p):
    safe = _bound_is_safe(_score_bound(p["g_qb"], p["g_kb"], QK_HEAD))
    return lax.cond(safe, functools.partial(_latent_call, True), functools.partial(_latent_call, False),
                    qbt, kb, vbt)


def _outproj_kernel(oa_ref, ob_ref, x_ref, mod_ref,
                    woa_ref, wob_ref, g2_ref, wrh_ref, wrl_ref, br_ref, tri_ref,
                    x1_ref, h2_ref, idx_ref, wts_ref, rank_ref, cnt_ref, base_ref):
    first_step = jnp.logical_and(pl.program_id(0) == 0, pl.program_id(1) == 0)

    @pl.when(first_step)
    def _():
        base_ref[...] = jnp.zeros_like(base_ref)

    mixed = _dot(oa_ref[...].astype(BF16), woa_ref[...]) + _dot(ob_ref[...], wob_ref[...])
    x1 = x_ref[...] + mod_ref[2:3, :] * mixed
    x1_ref[...] = x1
    d = x1.shape[1]
    h2 = x1 * lax.rsqrt(jnp.sum(x1 * x1, axis=-1, keepdims=True) * (1.0 / d) + EPS) * g2_ref[...]
    h2 = h2 * (1.0 + mod_ref[4:5, :]) + mod_ref[3:4, :]
    _store_token_rows(h2_ref, _pack_rows(h2))

    hh, hl = _split_bf16(h2)
    logits = _nt_dot(wrh_ref[...], hh) + _nt_dot(wrh_ref[...], hl) + _nt_dot(wrl_ref[...], hh)
    scores = _sigmoid(logits)
    biased = scores + br_ref[...]
    tm = biased.shape[1]
    sub = lax.broadcasted_iota(I32, (PER_GROUP, tm), 0)
    groups, gscore = [], []
    for g in range(N_GROUPS):
        bg = biased[g * PER_GROUP:(g + 1) * PER_GROUP, :]
        m1 = jnp.max(bg, axis=0, keepdims=True)
        i1 = jnp.min(jnp.where(bg == m1, sub, PER_GROUP), axis=0, keepdims=True)
        m2 = jnp.max(jnp.where(sub == i1, -jnp.inf, bg), axis=0, keepdims=True)
        groups.append(bg)
        gscore.append(m1 + m2)
    masked = []
    for g in range(N_GROUPS):
        ahead = jnp.zeros((1, tm), I32)
        for g2 in range(N_GROUPS):
            if g2 == g:
                continue
            beats = (gscore[g2] > gscore[g]) if g2 > g else (gscore[g2] >= gscore[g])
            ahead = ahead + beats.astype(I32)
        masked.append(jnp.where(ahead < TOPK_GROUPS, groups[g], -jnp.inf))
    cand = jnp.concatenate(masked, axis=0)
    eid = lax.broadcasted_iota(I32, (N_EXPERTS, tm), 0)
    chosen = jnp.zeros((N_EXPERTS, tm), F32)
    sel_idx, sel_w, sel_hot = [], [], []
    for _ in range(TOP_K):
        mv = jnp.max(cand, axis=0, keepdims=True)
        ie = jnp.min(jnp.where(cand == mv, eid, N_EXPERTS), axis=0, keepdims=True)
        hot = eid == ie
        sel_idx.append(ie)
        sel_hot.append(hot)
        sel_w.append(jnp.sum(jnp.where(hot, scores, 0.0), axis=0, keepdims=True))
        cand = jnp.where(hot, -jnp.inf, cand)
        chosen = chosen + hot.astype(F32)
    wsum = sel_w[0]
    for w in sel_w[1:]:
        wsum = wsum + w
    within = _dot(chosen.astype(BF16), tri_ref[...]) + base_ref[...]
    ranks = [jnp.sum(jnp.where(hot, within, 0.0), axis=0, keepdims=True) for hot in sel_hot]
    pad_i = [jnp.zeros((1, tm), I32)] * (8 - TOP_K)
    pad_f = [jnp.zeros((1, tm), F32)] * (8 - TOP_K)
    idx_ref[...] = jnp.concatenate(sel_idx + pad_i, axis=0)
    wts_ref[...] = jnp.concatenate([w / wsum * ROUTED_SCALE for w in sel_w] + pad_f, axis=0)
    rank_ref[...] = jnp.concatenate([r.astype(I32) for r in ranks] + pad_i, axis=0)
    base_ref[...] = base_ref[...] + jnp.sum(chosen, axis=1, keepdims=True)
    cnt_ref[...] = jnp.broadcast_to(base_ref[...], cnt_ref.shape)


def _outproj(oa, ob, x, mod, p):
    bt, s, d = x.shape
    tm = min(TM_OUT, s)
    n = bt * s
    woa = p["w_o"][:A_WIDTH].astype(BF16)
    wob = p["w_o"][A_WIDTH:].astype(BF16)
    wr_t = p["w_router"].T
    wrh, wrl = _split_bf16(wr_t)
    tri = (jnp.arange(tm)[:, None] < jnp.arange(tm)[None, :]).astype(BF16)
    consts = [woa, wob, p["g_norm2"].reshape(1, -1), wrh, wrl, p["b_router"].reshape(-1, 1), tri]
    tiles = s // tm
    tok = lambda w: pl.BlockSpec((None, tm, w), lambda b, i: (b, i, 0))
    full = lambda a: pl.BlockSpec(a.shape, lambda b, i: (0,) * a.ndim)
    flat = lambda w: pl.BlockSpec((tm, w), lambda b, i: (b * tiles + i, 0))
    slab = pl.BlockSpec((8, tm), lambda b, i: (0, b * tiles + i))
    return pl.pallas_call(
        _outproj_kernel,
        grid=(bt, tiles),
        in_specs=[tok(A_WIDTH), tok(B_WIDTH), tok(d), pl.BlockSpec((None, 6, d), lambda b, i: (b, 0, 0))]
        + [full(a) for a in consts],
        out_specs=[flat(d), pl.BlockSpec((tm * ROW_SUB, LANES), lambda b, i: (b * tiles + i, 0)),
                   slab, slab, slab, pl.BlockSpec((N_EXPERTS, LANES), lambda b, i: (0, 0))],
        out_shape=[jax.ShapeDtypeStruct((n, d), F32), jax.ShapeDtypeStruct((n * ROW_SUB, LANES), U32),
                   jax.ShapeDtypeStruct((8, n), I32), jax.ShapeDtypeStruct((8, n), F32),
                   jax.ShapeDtypeStruct((8, n), I32), jax.ShapeDtypeStruct((N_EXPERTS, LANES), F32)],
        scratch_shapes=[pltpu.VMEM((N_EXPERTS, 1), F32)],
        compiler_params=_cparams("arbitrary", "arbitrary"),
        name="outproj_router",
    )(oa, ob, x, mod, *consts)


def _dest_kernel(pstart_ref, idx_ref, rank_ref, dest_ref):
    idx = idx_ref[...]
    dest = rank_ref[...]
    for e in range(N_EXPERTS):
        dest = dest + jnp.where(idx == e, pstart_ref[e], 0)
    dest_ref[...] = dest * ROW_SUB


def _dest(pstart, idx_t, rank_t):
    n = idx_t.shape[1]
    tn = min(2048, n)
    slab = pl.BlockSpec((8, tn), lambda i, ps: (0, i))
    return pl.pallas_call(
        _dest_kernel,
        grid_spec=pltpu.PrefetchScalarGridSpec(
            num_scalar_prefetch=1, grid=(n // tn,), in_specs=[slab, slab], out_specs=slab),
        out_shape=jax.ShapeDtypeStruct((8, n), I32),
        compiler_params=_cparams("arbitrary"),
        name="moe_dest",
    )(pstart, idx_t, rank_t)


def _dispatch_kernel(pad_start_ref, pad_len_ref, nused_ref, dest_ref, h_ref, xs_ref, zero_ref, sem, pad_sem):
    tm = h_ref.shape[0] // ROW_SUB

    @pl.when(pl.program_id(0) == 0)
    def _():
        _zero_unrouted_slots(pad_start_ref, pad_len_ref, nused_ref, xs_ref, zero_ref, pad_sem,
                             xs_ref.shape[0] // zero_ref.shape[0])

    def slab(ref, first_row):
        return ref.at[pl.ds(pl.multiple_of(first_row, ROW_SUB), ROW_SUB), :]

    def issue(j, carry):
        for k in range(TOP_K):
            pltpu.make_async_copy(slab(h_ref, j * ROW_SUB), slab(xs_ref, dest_ref[k, j]), sem).start()
        return carry

    lax.fori_loop(0, tm, issue, 0, unroll=ISSUE_UNROLL)
    for k in range(TOP_K):
        pltpu.make_async_copy(h_ref, xs_ref.at[pl.ds(0, tm * ROW_SUB), :], sem).wait()


def _dispatch(pad_start, pad_len, n_used, dest_t, hp, n_slots):
    n = hp.shape[0] // ROW_SUB
    tm = min(TM_DISPATCH, n)
    return pl.pallas_call(
        _dispatch_kernel,
        grid_spec=pltpu.PrefetchScalarGridSpec(
            num_scalar_prefetch=3, grid=(n // tm,),
            in_specs=[
                pl.BlockSpec((8, tm), lambda i, ps, pn, nu: (0, i), memory_space=pltpu.SMEM),
                pl.BlockSpec((tm * ROW_SUB, LANES), lambda i, ps, pn, nu: (i, 0)),
            ],
            out_specs=pl.BlockSpec(memory_space=pl.ANY),
            scratch_shapes=[pltpu.VMEM((EXPERT_BLOCK * ROW_SUB, LANES), U32), pltpu.SemaphoreType.DMA(()),
                            pltpu.SemaphoreType.DMA(())]),
        out_shape=jax.ShapeDtypeStruct((n_slots * ROW_SUB, LANES), U32),
        compiler_params=_cparams("arbitrary"),
        name="moe_dispatch",
    )(pad_start, pad_len, n_used, dest_t, hp)


def _expert_kernel(be_ref, nused_ref, xs_ref, wg_ref, wu_ref, wd_ref, ys_ref):
    i = pl.program_id(0)

    @pl.when(i < nused_ref[0])
    def _():
        xb = _load_token_rows(xs_ref).astype(BF16)
        a = _dot(xb, wg_ref[...])
        u = _dot(xb, wu_ref[...])
        _store_token_rows(ys_ref, _pack_rows(_dot((_silu(a) * u).astype(BF16), wd_ref[...])))

    @pl.when(i >= nused_ref[0])
    def _():
        ys_ref[...] = jnp.zeros_like(ys_ref)


def _experts(block_e, n_used, xs, wg, wu, wd):
    n_slots = xs.shape[0] // ROW_SUB
    n_blocks = n_slots // EXPERT_BLOCK
    d, de = wg.shape[1], wg.shape[2]
    rows = pl.BlockSpec((EXPERT_BLOCK * ROW_SUB, LANES), lambda i, be, nu: (i, 0))
    used_rows = pl.BlockSpec((EXPERT_BLOCK * ROW_SUB, LANES), lambda i, be, nu: (jnp.minimum(i, nu[0] - 1), 0))
    return pl.pallas_call(
        _expert_kernel,
        grid_spec=pltpu.PrefetchScalarGridSpec(
            num_scalar_prefetch=2, grid=(n_blocks,),
            in_specs=[used_rows,
                      pl.BlockSpec((None, d, de), lambda i, be, nu: (be[i], 0, 0)),
                      pl.BlockSpec((None, d, de), lambda i, be, nu: (be[i], 0, 0)),
                      pl.BlockSpec((None, de, d), lambda i, be, nu: (be[i], 0, 0))],
            out_specs=rows),
        out_shape=jax.ShapeDtypeStruct((n_slots * ROW_SUB, LANES), U32),
        compiler_params=_cparams("arbitrary"),
        name="moe_experts",
    )(block_e, n_used, xs, wg, wu, wd)


def _combine_kernel(dest_ref, dest_next_ref, h_ref, x1_ref, mod_ref, wts_ref, wsg_ref, wsu_ref, wsd_ref, ys_ref,
                    out_ref, buf_ref, sems):
    i = pl.program_id(0)
    slot = lax.rem(i, 2)
    tm = h_ref.shape[0] // ROW_SUB

    def gather(idx_ref, into):
        def issue(j, carry):
            for k in range(TOP_K):
                src = ys_ref.at[pl.ds(pl.multiple_of(idx_ref[k, j], ROW_SUB), ROW_SUB), :]
                dst = buf_ref.at[into, k, pl.ds(pl.multiple_of(j * ROW_SUB, ROW_SUB), ROW_SUB), :]
                pltpu.make_async_copy(src, dst, sems.at[into]).start()
            return carry

        lax.fori_loop(0, tm, issue, 0, unroll=ISSUE_UNROLL)

    @pl.when(i == 0)
    def _():
        gather(dest_ref, 0)

    @pl.when(i + 1 < pl.num_programs(0))
    def _():
        gather(dest_next_ref, 1 - slot)

    hb = _load_token_rows(h_ref).astype(BF16)
    act = _silu(_dot(hb, wsg_ref[...])) * _dot(hb, wsu_ref[...])
    y = _dot(act.astype(BF16), wsd_ref[...])
    for k in range(TOP_K):
        pltpu.make_async_copy(ys_ref.at[pl.ds(0, tm * ROW_SUB), :], buf_ref.at[slot, k], sems.at[slot]).wait()
    wts = wts_ref[...]
    for k in range(TOP_K):
        y = y + wts[:, k:k + 1] * _load_token_rows(buf_ref.at[slot, k])
    out_ref[...] = x1_ref[...] + mod_ref[5:6, :] * y


def _combine(dest_t, hp, x1, mod, wts, ys, p, bt, s):
    n, d = x1.shape
    tm = min(TM_COMBINE, s)
    tiles = s // tm
    steps = n // tm
    wsg, wsu, wsd = (p[k].astype(BF16) for k in ("w_s_gate", "w_s_up", "w_s_down"))
    full = lambda a: pl.BlockSpec(a.shape, lambda i: (0,) * a.ndim)
    rows = pl.BlockSpec((tm, d), lambda i: (i, 0))
    out = pl.pallas_call(
        _combine_kernel,
        grid=(steps,),
        in_specs=[
            pl.BlockSpec((8, tm), lambda i: (0, i), memory_space=pltpu.SMEM),
            pl.BlockSpec((8, tm), lambda i: (0, jnp.minimum(i + 1, steps - 1)), memory_space=pltpu.SMEM),
            pl.BlockSpec((tm * ROW_SUB, LANES), lambda i: (i, 0)),
            rows,
            pl.BlockSpec((None, 6, d), lambda i: (i // tiles, 0, 0)),
            pl.BlockSpec((tm, 8), lambda i: (i, 0)),
            full(wsg), full(wsu), full(wsd),
            pl.BlockSpec(memory_space=pl.ANY),
        ],
        out_specs=rows,
        out_shape=jax.ShapeDtypeStruct((n, d), F32),
        scratch_shapes=[pltpu.VMEM((2, TOP_K, tm * ROW_SUB, LANES), U32), pltpu.SemaphoreType.DMA((2,))],
        compiler_params=_cparams("arbitrary"),
        name="moe_combine",
    )(dest_t, dest_t, hp, x1, mod, wts, wsg, wsu, wsd, ys)
    return out.reshape(bt, s, d)


def _layer(x, c, p):
    bt, s, d = x.shape
    n = bt * s
    mod = _adaln(c, p["w_ada"], p["b_ada"]).reshape(bt, 6, d)
    qa, ka, va, qb, kb, vb = _inproj(x, mod, p)
    oa = _dilated_attention(qa, ka, va, p)
    ob = _latent_attention(qb, kb, vb, p)
    x1, hp, idx_t, wts_t, rank_t, cnt = _outproj(oa, ob, x, mod, p)

    counts = cnt[:, 0].astype(I32)
    padded = (counts + EXPERT_BLOCK - 1) // EXPERT_BLOCK * EXPERT_BLOCK
    pend = jnp.cumsum(padded)
    pstart = pend - padded
    n_blocks = (n * TOP_K + N_EXPERTS * (EXPERT_BLOCK - 1)) // EXPERT_BLOCK
    block_first = jnp.arange(n_blocks, dtype=I32) * EXPERT_BLOCK
    block_e = jnp.minimum(jnp.sum((pend[None, :] <= block_first[:, None]).astype(I32), axis=1), N_EXPERTS - 1)
    n_used = (pend[-1:] // EXPERT_BLOCK).astype(I32)

    dest_t = _dest(pstart.astype(I32), idx_t, rank_t)
    xs = _dispatch(((pstart + counts) * ROW_SUB).astype(I32), (padded - counts).astype(I32), n_used, dest_t, hp,
                   n_blocks * EXPERT_BLOCK)
    ys = _experts(block_e, n_used, xs, p["w_e_gate"].astype(BF16), p["w_e_up"].astype(BF16),
                  p["w_e_down"].astype(BF16))
    return _combine(dest_t, hp, x1, mod, wts_t.T, ys, p, bt, s)


_PARAM_NAMES = ("w_ada", "b_ada", "g_norm1", "w_in", "g_qa", "g_ka", "g_qlat", "w_qb", "g_kvlat", "w_kvb",
                "g_qb", "g_kb", "w_o", "g_norm2", "w_router", "b_router", "w_e_gate", "w_e_up", "w_e_down",
                "w_s_gate", "w_s_up", "w_s_down")


def kernel(x_prompt, x_sample, c_prompt, c_sample, w_ada, b_ada, g_norm1, w_in, g_qa, g_ka, g_qlat, w_qb,
           g_kvlat, w_kvb, g_qb, g_kb, w_o, g_norm2, w_router, b_router, w_e_gate, w_e_up, w_e_down,
           w_s_gate, w_s_up, w_s_down):
    stacked = (w_ada, b_ada, g_norm1, w_in, g_qa, g_ka, g_qlat, w_qb, g_kvlat, w_kvb, g_qb, g_kb, w_o,
               g_norm2, w_router, b_router, w_e_gate, w_e_up, w_e_down, w_s_gate, w_s_up, w_s_down)
    y_prompt, y_sample = x_prompt, x_sample
    for layer in range(w_ada.shape[0]):
        p = {name: w[layer] for name, w in zip(_PARAM_NAMES, stacked)}
        y_prompt = _layer(y_prompt, c_prompt, p)
        y_sample = _layer(y_sample, c_sample, p)
    return (y_prompt, y_sample)
```

```python
import functools

import jax
import jax.numpy as jnp
from jax import lax
from jax.experimental import pallas as pl
from jax.experimental.pallas import tpu as pltpu

F32 = jnp.float32
BF16 = jnp.bfloat16
I32 = jnp.int32
U32 = jnp.uint32

HEAD_DIM = 64
A_HEADS = 12
A_WIDTH = A_HEADS * HEAD_DIM
DILATED_BRANCHES = ((128, 1), (512, 4), (2048, 16))
ROT_DIMS_A = HEAD_DIM // 4
B_HEADS = 4
QK_NOPE = 64
QK_ROPE = 32
QK_HEAD = QK_NOPE + QK_ROPE
V_HEAD = 64
Q_LORA = 256
KV_LORA = 256
B_WIDTH = B_HEADS * V_HEAD
ROPE_THETA = 500000.0
N_EXPERTS = 64
TOP_K = 6
N_GROUPS = 8
TOPK_GROUPS = 4
PER_GROUP = N_EXPERTS // N_GROUPS
ROUTED_SCALE = 2.5
EXPERT_BLOCK = 512
EPS = 1e-6

LOG2E = 1.4426950408889634
BOUND_SLACK = 1.02
EXP2_SAFE_RANGE = 120.0

LANES = 128
PAD_HEAD = 128
NEG_BIG = -1e30
VMEM_LIMIT = 56 * 1024 * 1024

TM_IN = 512
TM_OUT = 512
TQ_A = 128
UNROLL_A = 16
NORM_ROWS_A = 512
TQ_B = 512
KC_B = 128
KC_FAST_B = 256
KV_TILES_B = 2
TM_DISPATCH = 512
TM_COMBINE = 256
ISSUE_UNROLL = 4
ROW_SUB = 4


def _nt_dot(a, b):
    return lax.dot_general(a, b, (((1,), (1,)), ((), ())), preferred_element_type=F32)


def _dot(a, b):
    return jnp.dot(a, b, preferred_element_type=F32)


def _split_bf16(x):
    hi = x.astype(BF16)
    lo = (x - hi.astype(F32)).astype(BF16)
    return hi, lo


def _sigmoid(x):
    return 1.0 / (1.0 + jnp.exp(-x))


def _silu(x):
    return x * _sigmoid(x)


def _pack_rows(x):
    w = x.shape[1] // 2
    lo = lax.bitcast_convert_type(x[:, :w].astype(BF16).astype(F32), U32)
    hi = lax.bitcast_convert_type(x[:, w:].astype(BF16).astype(F32), U32)
    return (lo >> 16) | (hi & jnp.uint32(0xFFFF0000))


def _unpack_words(words):
    lo = lax.bitcast_convert_type(words << 16, F32)
    hi = lax.bitcast_convert_type(words & jnp.uint32(0xFFFF0000), F32)
    return lo, hi


def _store_token_rows(ref, words):
    m = ref.shape[0] // ROW_SUB
    for c in range(ROW_SUB):
        ref[pl.ds(c, m, stride=ROW_SUB), :] = words[:, c * LANES:(c + 1) * LANES]


def _load_token_rows(ref):
    m = ref.shape[0] // ROW_SUB
    halves = [_unpack_words(ref[pl.ds(c, m, stride=ROW_SUB), :]) for c in range(ROW_SUB)]
    return jnp.concatenate([lo for lo, _ in halves] + [hi for _, hi in halves], axis=1)


def _zero_unrouted_slots(pad_start_ref, pad_len_ref, nused_ref, xs_ref, zero_ref, pad_sem, n_blocks):
    zero_ref[...] = jnp.zeros_like(zero_ref)
    block_rows = zero_ref.shape[0]

    def slab(ref, first_row):
        return ref.at[pl.ds(pl.multiple_of(first_row, ROW_SUB), ROW_SUB), :]

    def pad_copy(e, r):
        return pltpu.make_async_copy(slab(zero_ref, 0), slab(xs_ref, pad_start_ref[e] + r * ROW_SUB), pad_sem)

    def per_expert(e, carry):
        lax.fori_loop(0, pad_len_ref[e], lambda r, c: (pad_copy(e, r).start(), c)[1], 0)
        lax.fori_loop(0, pad_len_ref[e], lambda r, c: (pad_copy(e, r).wait(), c)[1], 0)
        return carry

    lax.fori_loop(0, N_EXPERTS, per_expert, 0)

    def block_copy(b):
        rows = pl.ds(pl.multiple_of(b * block_rows, block_rows), block_rows)
        return pltpu.make_async_copy(zero_ref, xs_ref.at[rows, :], pad_sem)

    lax.fori_loop(nused_ref[0], n_blocks, lambda b, c: (block_copy(b).start(), c)[1], 0)
    lax.fori_loop(nused_ref[0], n_blocks, lambda b, c: (block_copy(b).wait(), c)[1], 0)


def _cparams(*sem):
    return pltpu.CompilerParams(dimension_semantics=sem, vmem_limit_bytes=VMEM_LIMIT)


def _adaln_kernel(c_ref, w_ref, b_ref, o_ref):
    s = _silu(c_ref[...])
    o_ref[...] = _dot(s.astype(BF16), w_ref[...].astype(BF16)) + b_ref[...]


def _adaln(c, w_ada, b_ada):
    bt, d = c.shape
    n_chunks = w_ada.shape[1] // d
    return pl.pallas_call(
        _adaln_kernel,
        grid=(n_chunks,),
        in_specs=[
            pl.BlockSpec((bt, d), lambda j: (0, 0)),
            pl.BlockSpec((d, d), lambda j: (0, j)),
            pl.BlockSpec((1, d), lambda j: (0, j)),
        ],
        out_specs=pl.BlockSpec((bt, d), lambda j: (0, j)),
        out_shape=jax.ShapeDtypeStruct((bt, w_ada.shape[1]), F32),
        compiler_params=_cparams("arbitrary"),
        name="adaln",
    )(c, w_ada, b_ada.reshape(1, -1))


def _seg_sumsq(x, seg_ref):
    sq = (x * x).astype(BF16)
    width = seg_ref.shape[0]
    outs = [_dot(sq[:, c * width:(c + 1) * width], seg_ref[...]) for c in range(x.shape[1] // width)]
    return outs[0] if len(outs) == 1 else jnp.concatenate(outs, axis=1)


def _tile_lanes(t, reps):
    return t if reps == 1 else jnp.concatenate([t] * reps, axis=1)


def _rope_lanes(x, tab_ref, shift, reps):
    width = x.shape[1]
    cos = _tile_lanes(tab_ref[:, 0:LANES], reps)
    sin_p = _tile_lanes(tab_ref[:, LANES:2 * LANES], reps)
    sin_m = _tile_lanes(tab_ref[:, 2 * LANES:3 * LANES], reps)
    return x * cos + pltpu.roll(x, shift, 1) * sin_p + pltpu.roll(x, width - shift, 1) * sin_m


def _inproj_kernel(x_ref, mod_ref, g1_ref, wqkv_ref, wlat_ref, wkr_ref, wqb_ref, wkn_ref, wv_ref,
                   gqa_ref, gka_ref, gql_ref, gkvl_ref, gqb_ref, gkb_ref, seg64_ref, seg128_ref, qpad_ref, kpad_ref,
                   ropea_ref, ropeb_ref,
                   qa_ref, ka_ref, va_ref, qb_ref, kb_ref, vb_ref):
    x = x_ref[...]
    d = x.shape[1]
    h = x * lax.rsqrt(jnp.sum(x * x, axis=-1, keepdims=True) * (1.0 / d) + EPS) * g1_ref[...]
    h = h * (1.0 + mod_ref[1:2, :]) + mod_ref[0:1, :]
    hb = h.astype(BF16)

    z = _dot(hb, wqkv_ref[...])
    q = z[:, 0:A_WIDTH]
    k = z[:, A_WIDTH:2 * A_WIDTH]
    q = q * lax.rsqrt(_seg_sumsq(q, seg64_ref) * (1.0 / HEAD_DIM) + EPS) * gqa_ref[...]
    k = k * lax.rsqrt(_seg_sumsq(k, seg64_ref) * (1.0 / HEAD_DIM) + EPS) * gka_ref[...]
    reps_a = A_WIDTH // LANES
    q = _rope_lanes(q, ropea_ref, ROT_DIMS_A // 2, reps_a) * (HEAD_DIM ** -0.5 * LOG2E)
    k = _rope_lanes(k, ropea_ref, ROT_DIMS_A // 2, reps_a)
    qa_ref[...] = q
    ka_ref[...] = k
    va_ref[...] = z[:, 2 * A_WIDTH:3 * A_WIDTH]

    zl = _dot(hb, wlat_ref[...])
    ql = zl[:, 0:Q_LORA]
    kvl = zl[:, Q_LORA:Q_LORA + KV_LORA]
    ql = ql * lax.rsqrt(jnp.sum(ql * ql, axis=-1, keepdims=True) * (1.0 / Q_LORA) + EPS) * gql_ref[...]
    kvl = kvl * lax.rsqrt(jnp.sum(kvl * kvl, axis=-1, keepdims=True) * (1.0 / KV_LORA) + EPS) * gkvl_ref[...]
    qlb = ql.astype(BF16)
    kvlb = kvl.astype(BF16)
    reps_b = B_HEADS
    qb = _dot(qlb, wqb_ref[...])
    qb = qb * lax.rsqrt(_seg_sumsq(qb, seg128_ref) * (1.0 / QK_HEAD) + EPS) * gqb_ref[...]
    qb = _rope_lanes(qb, ropeb_ref, QK_ROPE // 2, reps_b) * (QK_HEAD ** -0.5 * LOG2E) + qpad_ref[...]
    qb_ref[...] = qb.T.astype(BF16)
    kr = _dot(hb, wkr_ref[...])
    kb = _dot(kvlb, wkn_ref[...]) + _tile_lanes(kr, reps_b)
    kb = kb * lax.rsqrt(_seg_sumsq(kb, seg128_ref) * (1.0 / QK_HEAD) + EPS) * gkb_ref[...]
    kb = _rope_lanes(kb, ropeb_ref, QK_ROPE // 2, reps_b) + kpad_ref[...]
    kb_ref[...] = kb.astype(BF16)
    vb_ref[...] = _dot(kvlb, wv_ref[...]).T.astype(BF16)


def _score_bound(g_q, g_k, head_dim):
    bound = LOG2E * head_dim ** 0.5 * jnp.max(jnp.abs(g_q)) * jnp.max(jnp.abs(g_k))
    return bound * BOUND_SLACK


def _bound_is_safe(bound):
    return 2.0 * bound < EXP2_SAFE_RANGE


def _rope_table(seq, rot, lane_of_first, period):
    half = rot // 2
    inv = ROPE_THETA ** (-jnp.arange(half, dtype=F32) * 2.0 / rot)
    ang = jnp.arange(seq).astype(F32)[:, None] * inv[None, :]
    cos, sin = jnp.cos(ang), jnp.sin(ang)
    one = jnp.ones((seq, period), F32)
    zero = jnp.zeros((seq, period), F32)
    a, b = lane_of_first, lane_of_first + half
    cos_t = one.at[:, a:a + half].set(cos).at[:, b:b + half].set(cos)
    sin_p = zero.at[:, b:b + half].set(sin)
    sin_m = zero.at[:, a:a + half].set(-sin)
    reps = LANES // period
    return jnp.concatenate([jnp.tile(t, (1, reps)) for t in (cos_t, sin_p, sin_m)], axis=1)


def _pad_heads(w, heads, width, offset=0):
    lead = w.shape[:-1]
    w = w.reshape(lead + (heads, width))
    pad = [(0, 0)] * len(lead) + [(0, 0), (offset, PAD_HEAD - width - offset)]
    return jnp.pad(w, pad).reshape(lead + (heads * PAD_HEAD,))


def _inproj(x, mod, p):
    bt, s, d = x.shape
    tm = min(TM_IN, s)
    ropea = _rope_table(s, ROT_DIMS_A, 0, HEAD_DIM)
    ropeb = _rope_table(s, QK_ROPE, QK_NOPE, PAD_HEAD)
    w_in = p["w_in"]
    wqkv = w_in[:, 0:3 * A_WIDTH].astype(BF16)
    wlat = w_in[:, 3 * A_WIDTH:3 * A_WIDTH + Q_LORA + KV_LORA].astype(BF16)
    wkr = _pad_heads(w_in[:, 3 * A_WIDTH + Q_LORA + KV_LORA:], 1, QK_ROPE, QK_NOPE).astype(BF16)
    wqb = _pad_heads(p["w_qb"], B_HEADS, QK_HEAD).astype(BF16)
    wkv = p["w_kvb"].reshape(KV_LORA, B_HEADS, QK_NOPE + V_HEAD)
    wkn = _pad_heads(wkv[:, :, :QK_NOPE].reshape(KV_LORA, B_HEADS * QK_NOPE), B_HEADS, QK_NOPE).astype(BF16)
    wv = wkv[:, :, QK_NOPE:].reshape(KV_LORA, B_WIDTH).astype(BF16)
    gqa = jnp.tile(p["g_qa"], A_HEADS).reshape(1, -1)
    gka = jnp.tile(p["g_ka"], A_HEADS).reshape(1, -1)
    gqb = _pad_heads(jnp.tile(p["g_qb"], B_HEADS), B_HEADS, QK_HEAD).reshape(1, -1)
    gkb = _pad_heads(jnp.tile(p["g_kb"], B_HEADS), B_HEADS, QK_HEAD).reshape(1, -1)
    seg64 = jnp.kron(jnp.eye(256 // HEAD_DIM, dtype=F32), jnp.ones((HEAD_DIM, HEAD_DIM), F32)).astype(BF16)
    seg128 = jnp.kron(jnp.eye(2, dtype=F32), jnp.ones((PAD_HEAD, PAD_HEAD), F32)).astype(BF16)

    def full(a):
        return pl.BlockSpec(a.shape, lambda i, b: (0,) * a.ndim)

    spare = jnp.zeros((B_HEADS, PAD_HEAD), F32).at[:, QK_HEAD].set(1.0).reshape(1, -1)
    qpad = -_score_bound(p["g_qb"], p["g_kb"], QK_HEAD) * spare
    consts = [p["g_norm1"].reshape(1, -1), wqkv, wlat, wkr, wqb, wkn, wv, gqa, gka,
              p["g_qlat"].reshape(1, -1), p["g_kvlat"].reshape(1, -1), gqb, gkb, seg64, seg128, qpad, spare]
    tok = lambda w: pl.BlockSpec((None, tm, w), lambda i, b: (b, i, 0))
    qw = B_HEADS * PAD_HEAD
    tok_shape = lambda w, dt=F32: jax.ShapeDtypeStruct((bt, s, w), dt)
    return pl.pallas_call(
        _inproj_kernel,
        grid=(s // tm, bt),
        in_specs=[tok(d), pl.BlockSpec((None, 6, d), lambda i, b: (b, 0, 0))]
        + [full(a) for a in consts]
        + [pl.BlockSpec((tm, 3 * LANES), lambda i, b: (i, 0))] * 2,
        out_specs=[tok(A_WIDTH), tok(A_WIDTH), tok(A_WIDTH),
                   pl.BlockSpec((None, qw, tm), lambda i, b: (b, 0, i)),
                   tok(qw),
                   pl.BlockSpec((None, None, B_WIDTH, tm), lambda i, b: (b, i, 0, 0))],
        out_shape=[tok_shape(A_WIDTH), tok_shape(A_WIDTH), tok_shape(A_WIDTH),
                   jax.ShapeDtypeStruct((bt, qw, s), BF16),
                   tok_shape(qw, BF16),
                   jax.ShapeDtypeStruct((bt, s // tm, B_WIDTH, tm), BF16)],
        compiler_params=_cparams("arbitrary", "arbitrary"),
        name="inproj",
    )(x, mod, *consts, ropea, ropeb)


def _rows(start, size, stride):
    return pl.ds(start, size) if stride == 1 else pl.ds(start, size, stride=stride)


def _dilated_kernel(bound_ref, q_ref, k_ref, v_ref, o_ref, stat_ref, bias_ref, *, bounded):
    s_len = q_ref.shape[0]
    first = lax.broadcasted_iota(I32, (TQ_A, LANES), 1) < HEAD_DIM
    for branch, (window, dil) in enumerate(sorted(DILATED_BRANCHES, key=lambda wd: -wd[1])):
        radius = window // (2 * dil)
        win = TQ_A + 2 * radius
        length = s_len // dil
        nq = length // TQ_A
        rel = lax.broadcasted_iota(I32, (TQ_A, win), 0) - lax.broadcasted_iota(I32, (TQ_A, win), 1)
        shift = bound_ref[0] if bounded else 0.0
        for case in range(3):
            bias_ref[case] = jnp.where(jnp.abs(rel + case * radius) <= radius, -shift, NEG_BIG)

        def group(g, carry, branch=branch, dil=dil, radius=radius, win=win, length=length, nq=nq):
            done = []
            for u in range(UNROLL_A):
                idx = g * UNROLL_A + u
                r = idx // nq
                q0 = (idx % nq) * TQ_A
                w0 = jnp.clip(q0 - radius, 0, length - win)
                qrows = _rows(r + dil * q0, TQ_A, dil)
                krows = _rows(r + dil * w0, win, dil)
                q = q_ref[qrows, :]
                kw = k_ref[krows, :].astype(BF16)
                vw = v_ref[krows, :].astype(BF16)
                bias = bias_ref[(q0 - w0) // radius]
                outs, stats = [], []
                for keep in (first, jnp.logical_not(first)):
                    s = _nt_dot(jnp.where(keep, q, 0.0).astype(BF16), kw)
                    if bounded:
                        pexp = jnp.exp2(s + bias)
                        outs.append(_dot(pexp.astype(BF16), vw))
                        stats.append(jnp.sum(pexp, axis=-1, keepdims=True))
                    else:
                        s = s + bias
                        m = jnp.max(s, axis=-1, keepdims=True)
                        pexp = jnp.exp2(s - m)
                        den = jnp.sum(pexp, axis=-1, keepdims=True)
                        outs.append(_dot(pexp.astype(BF16), vw) / den)
                        stats.append(m + jnp.log2(den))
                o_new = jnp.where(first, outs[0], outs[1])
                stat_new = jnp.where(first, stats[0], stats[1])
                if branch > 0:
                    o_old, stat_old = o_ref[qrows, :], stat_ref[qrows, :]
                    if bounded:
                        o_new, stat_new = o_old + o_new, stat_old + stat_new
                    else:
                        mx = jnp.maximum(stat_old, stat_new)
                        w_old, w_new = jnp.exp2(stat_old - mx), jnp.exp2(stat_new - mx)
                        den = w_old + w_new
                        o_new = (w_old * o_old + w_new * o_new) / den
                        stat_new = mx + jnp.log2(den)
                done.append((qrows, o_new, stat_new))
            for qrows, o_new, stat_new in done:
                o_ref[qrows, :] = o_new
                stat_ref[qrows, :] = stat_new
            return carry

        lax.fori_loop(0, dil * nq // UNROLL_A, group, 0)

    if bounded:
        def normalise(i, carry):
            rows = pl.ds(pl.multiple_of(i * NORM_ROWS_A, NORM_ROWS_A), NORM_ROWS_A)
            o_ref[rows, :] = o_ref[rows, :] / stat_ref[rows, :]
            return carry

        lax.fori_loop(0, s_len // NORM_ROWS_A, normalise, 0)


def _dilated_call(bounded, bound, qa, ka, va):
    bt, s, width = qa.shape
    radii = {window // (2 * dil) for window, dil in DILATED_BRANCHES}
    assert len(radii) == 1, "the score-mask scratch assumes one window radius for all branches"
    radius = radii.pop()
    spec = pl.BlockSpec((None, s, LANES), lambda b, j: (b, 0, j))
    return pl.pallas_call(
        functools.partial(_dilated_kernel, bounded=bounded),
        grid=(bt, width // LANES),
        in_specs=[pl.BlockSpec(memory_space=pltpu.SMEM), spec, spec, spec],
        out_specs=spec,
        out_shape=jax.ShapeDtypeStruct((bt, s, width), F32),
        scratch_shapes=[pltpu.VMEM((s, LANES), F32), pltpu.VMEM((3, TQ_A, TQ_A + 2 * radius), F32)],
        compiler_params=_cparams("arbitrary", "arbitrary"),
        name="dilated_bounded" if bounded else "dilated_online",
    )(bound.reshape(1), qa, ka, va)


def _dilated_attention(qa, ka, va, p):
    bound = _score_bound(p["g_qa"], p["g_ka"], HEAD_DIM)
    return lax.cond(_bound_is_safe(bound),
                    functools.partial(_dilated_call, True), functools.partial(_dilated_call, False),
                    bound, qa, ka, va)


def _latent_online_kernel(qt_ref, k_ref, vt_ref, o_ref):
    n_kv, _, tk = vt_ref.shape
    tq = qt_ref.shape[1]
    n_chunks = tk // KC_B
    qts = [qt_ref[h * PAD_HEAD:(h + 1) * PAD_HEAD, :] for h in range(2)]

    def scores(j, h):
        k0 = pl.multiple_of(j * tk, tk)
        return tuple(_dot(k_ref[pl.ds(k0 + c * KC_B, KC_B), h * PAD_HEAD:(h + 1) * PAD_HEAD], qts[h])
                     for c in range(n_chunks))

    def body(j, carry):
        nxt = jnp.minimum(j + 1, n_kv - 1)
        new = []
        for h in range(2):
            m, l, acc, sts = carry[h]
            sts_next = scores(nxt, h)
            for c in range(n_chunks):
                m_new = jnp.maximum(m, jnp.max(sts[c], axis=0, keepdims=True))
                alpha = jnp.exp2(m - m_new)
                pexp = jnp.exp2(sts[c] - m_new)
                l = alpha * l + jnp.sum(pexp, axis=0, keepdims=True)
                vt = vt_ref[j, h * V_HEAD:(h + 1) * V_HEAD, c * KC_B:(c + 1) * KC_B]
                acc = alpha * acc + _dot(vt, pexp.astype(BF16))
                m = m_new
            new.append((m, l, acc, sts_next))
        return tuple(new)

    init = tuple((jnp.full((1, tq), NEG_BIG, F32), jnp.zeros((1, tq), F32), jnp.zeros((V_HEAD, tq), F32),
                  scores(0, h)) for h in range(2))
    (_, l0, acc0, _), (_, l1, acc1, _) = lax.fori_loop(0, n_kv, body, init)
    ot = jnp.concatenate([acc0 / l0, acc1 / l1], axis=0)
    o_ref[...] = ot.T.astype(o_ref.dtype)


def _latent_bounded_kernel(qt_ref, k_ref, vt_ref, o_ref):
    n_kv, _, tk = vt_ref.shape
    tq = qt_ref.shape[1]
    n_chunks = tk // KC_FAST_B
    qts = [qt_ref[h * PAD_HEAD:(h + 1) * PAD_HEAD, :] for h in range(2)]
    group = min(KV_TILES_B, n_kv)

    def probs(jj, h):
        out = []
        for t in range(group):
            k0 = pl.multiple_of((jj * group + t) * tk, tk)
            for c in range(n_chunks):
                kt = k_ref[pl.ds(k0 + c * KC_FAST_B, KC_FAST_B), h * PAD_HEAD:(h + 1) * PAD_HEAD]
                pexp = jnp.exp2(_dot(kt, qts[h]))
                out.append((pexp.astype(BF16), jnp.sum(pexp, axis=0, keepdims=True)))
        return tuple(out)

    def body(jj, carry):
        nxt = jnp.minimum(jj + 1, n_kv // group - 1)
        new = []
        for h in range(2):
            l, acc, cur = carry[h]
            nxt_probs = probs(nxt, h)
            for t in range(group):
                for c in range(n_chunks):
                    pexp, psum = cur[t * n_chunks + c]
                    vt = vt_ref[jj * group + t, h * V_HEAD:(h + 1) * V_HEAD, c * KC_FAST_B:(c + 1) * KC_FAST_B]
                    acc = acc + _dot(vt, pexp)
                    l = l + psum
            new.append((l, acc, nxt_probs))
        return tuple(new)

    init = tuple((jnp.zeros((1, tq), F32), jnp.zeros((V_HEAD, tq), F32), probs(0, h)) for h in range(2))
    (l0, acc0, _), (l1, acc1, _) = lax.fori_loop(0, n_kv // group, body, init)
    ot = jnp.concatenate([acc0 / l0, acc1 / l1], axis=0)
    o_ref[...] = ot.T.astype(o_ref.dtype)


def _latent_call(bounded, qbt, kb, vbt):
    bt, s, _ = kb.shape
    pairs = B_HEADS // 2
    n_kv, tk = vbt.shape[1], vbt.shape[3]
    tq = min(TQ_B, s)
    return pl.pallas_call(
        _latent_bounded_kernel if bounded else _latent_online_kernel,
        grid=(bt, pairs, s // tq),
        in_specs=[
            pl.BlockSpec((None, 2 * PAD_HEAD, tq), lambda b, hp, i: (b, hp, i)),
            pl.BlockSpec((None, s, 2 * PAD_HEAD), lambda b, hp, i: (b, 0, hp)),
            pl.BlockSpec((None, n_kv, 2 * V_HEAD, tk), lambda b, hp, i: (b, 0, hp, 0)),
        ],
        out_specs=pl.BlockSpec((None, tq, 2 * V_HEAD), lambda b, hp, i: (b, i, hp)),
        out_shape=jax.ShapeDtypeStruct((bt, s, B_WIDTH), BF16),
        compiler_params=_cparams("arbitrary", "arbitrary", "arbitrary"),
        name="latent_bounded" if bounded else "latent_online",
    )(qbt, kb, vbt)


def _latent_attention(qbt, kb, vbt, p):
    safe = _bound_is_safe(_score_bound(p["g_qb"], p["g_kb"], QK_HEAD))
    return lax.cond(safe, functools.partial(_latent_call, True), functools.partial(_latent_call, False),
                    qbt, kb, vbt)


def _outproj_kernel(oa_ref, ob_ref, x_ref, mod_ref,
                    woa_ref, wob_ref, g2_ref, wrh_ref, wrl_ref, br_ref, tri_ref,
                    x1_ref, h2_ref, idx_ref, wts_ref, rank_ref, cnt_ref, base_ref):
    first_step = jnp.logical_and(pl.program_id(0) == 0, pl.program_id(1) == 0)

    @pl.when(first_step)
    def _():
        base_ref[...] = jnp.zeros_like(base_ref)

    mixed = _dot(oa_ref[...].astype(BF16), woa_ref[...]) + _dot(ob_ref[...], wob_ref[...])
    x1 = x_ref[...] + mod_ref[2:3, :] * mixed
    x1_ref[...] = x1
    d = x1.shape[1]
    h2 = x1 * lax.rsqrt(jnp.sum(x1 * x1, axis=-1, keepdims=True) * (1.0 / d) + EPS) * g2_ref[...]
    h2 = h2 * (1.0 + mod_ref[4:5, :]) + mod_ref[3:4, :]
    _store_token_rows(h2_ref, _pack_rows(h2))

    hh, hl = _split_bf16(h2)
    logits = _nt_dot(wrh_ref[...], hh) + _nt_dot(wrh_ref[...], hl) + _nt_dot(wrl_ref[...], hh)
    scores = _sigmoid(logits)
    biased = scores + br_ref[...]
    tm = biased.shape[1]
    sub = lax.broadcasted_iota(I32, (PER_GROUP, tm), 0)
    groups, gscore = [], []
    for g in range(N_GROUPS):
        bg = biased[g * PER_GROUP:(g + 1) * PER_GROUP, :]
        m1 = jnp.max(bg, axis=0, keepdims=True)
        i1 = jnp.min(jnp.where(bg == m1, sub, PER_GROUP), axis=0, keepdims=True)
        m2 = jnp.max(jnp.where(sub == i1, -jnp.inf, bg), axis=0, keepdims=True)
        groups.append(bg)
        gscore.append(m1 + m2)
    masked = []
    for g in range(N_GROUPS):
        ahead = jnp.zeros((1, tm), I32)
        for g2 in range(N_GROUPS):
            if g2 == g:
                continue
            beats = (gscore[g2] > gscore[g]) if g2 > g else (gscore[g2] >= gscore[g])
            ahead = ahead + beats.astype(I32)
        masked.append(jnp.where(ahead < TOPK_GROUPS, groups[g], -jnp.inf))
    cand = jnp.concatenate(masked, axis=0)
    eid = lax.broadcasted_iota(I32, (N_EXPERTS, tm), 0)
    chosen = jnp.zeros((N_EXPERTS, tm), F32)
    sel_idx, sel_w, sel_hot = [], [], []
    for _ in range(TOP_K):
        mv = jnp.max(cand, axis=0, keepdims=True)
        ie = jnp.min(jnp.where(cand == mv, eid, N_EXPERTS), axis=0, keepdims=True)
        hot = eid == ie
        sel_idx.append(ie)
        sel_hot.append(hot)
        sel_w.append(jnp.sum(jnp.where(hot, scores, 0.0), axis=0, keepdims=True))
        cand = jnp.where(hot, -jnp.inf, cand)
        chosen = chosen + hot.astype(F32)
    wsum = sel_w[0]
    for w in sel_w[1:]:
        wsum = wsum + w
    within = _dot(chosen.astype(BF16), tri_ref[...]) + base_ref[...]
    ranks = [jnp.sum(jnp.where(hot, within, 0.0), axis=0, keepdims=True) for hot in sel_hot]
    pad_i = [jnp.zeros((1, tm), I32)] * (8 - TOP_K)
    pad_f = [jnp.zeros((1, tm), F32)] * (8 - TOP_K)
    idx_ref[...] = jnp.concatenate(sel_idx + pad_i, axis=0)
    wts_ref[...] = jnp.concatenate([w / wsum * ROUTED_SCALE for w in sel_w] + pad_f, axis=0)
    rank_ref[...] = jnp.concatenate([r.astype(I32) for r in ranks] + pad_i, axis=0)
    base_ref[...] = base_ref[...] + jnp.sum(chosen, axis=1, keepdims=True)
    cnt_ref[...] = jnp.broadcast_to(base_ref[...], cnt_ref.shape)


def _outproj(oa, ob, x, mod, p):
    bt, s, d = x.shape
    tm = min(TM_OUT, s)
    n = bt * s
    woa = p["w_o"][:A_WIDTH].astype(BF16)
    wob = p["w_o"][A_WIDTH:].astype(BF16)
    wr_t = p["w_router"].T
    wrh, wrl = _split_bf16(wr_t)
    tri = (jnp.arange(tm)[:, None] < jnp.arange(tm)[None, :]).astype(BF16)
    consts = [woa, wob, p["g_norm2"].reshape(1, -1), wrh, wrl, p["b_router"].reshape(-1, 1), tri]
    tiles = s // tm
    tok = lambda w: pl.BlockSpec((None, tm, w), lambda b, i: (b, i, 0))
    full = lambda a: pl.BlockSpec(a.shape, lambda b, i: (0,) * a.ndim)
    flat = lambda w: pl.BlockSpec((tm, w), lambda b, i: (b * tiles + i, 0))
    slab = pl.BlockSpec((8, tm), lambda b, i: (0, b * tiles + i))
    return pl.pallas_call(
        _outproj_kernel,
        grid=(bt, tiles),
        in_specs=[tok(A_WIDTH), tok(B_WIDTH), tok(d), pl.BlockSpec((None, 6, d), lambda b, i: (b, 0, 0))]
        + [full(a) for a in consts],
        out_specs=[flat(d), pl.BlockSpec((tm * ROW_SUB, LANES), lambda b, i: (b * tiles + i, 0)),
                   slab, slab, slab, pl.BlockSpec((N_EXPERTS, LANES), lambda b, i: (0, 0))],
        out_shape=[jax.ShapeDtypeStruct((n, d), F32), jax.ShapeDtypeStruct((n * ROW_SUB, LANES), U32),
                   jax.ShapeDtypeStruct((8, n), I32), jax.ShapeDtypeStruct((8, n), F32),
                   jax.ShapeDtypeStruct((8, n), I32), jax.ShapeDtypeStruct((N_EXPERTS, LANES), F32)],
        scratch_shapes=[pltpu.VMEM((N_EXPERTS, 1), F32)],
        compiler_params=_cparams("arbitrary", "arbitrary"),
        name="outproj_router",
    )(oa, ob, x, mod, *consts)


def _dest_kernel(pstart_ref, idx_ref, rank_ref, dest_ref):
    idx = idx_ref[...]
    dest = rank_ref[...]
    for e in range(N_EXPERTS):
        dest = dest + jnp.where(idx == e, pstart_ref[e], 0)
    dest_ref[...] = dest * ROW_SUB


def _dest(pstart, idx_t, rank_t):
    n = idx_t.shape[1]
    tn = min(2048, n)
    slab = pl.BlockSpec((8, tn), lambda i, ps: (0, i))
    return pl.pallas_call(
        _dest_kernel,
        grid_spec=pltpu.PrefetchScalarGridSpec(
            num_scalar_prefetch=1, grid=(n // tn,), in_specs=[slab, slab], out_specs=slab),
        out_shape=jax.ShapeDtypeStruct((8, n), I32),
        compiler_params=_cparams("arbitrary"),
        name="moe_dest",
    )(pstart, idx_t, rank_t)


def _dispatch_kernel(pad_start_ref, pad_len_ref, nused_ref, dest_ref, h_ref, xs_ref, stage_ref, zero_ref,
                     load_sems, row_sems, pad_sem):
    i = pl.program_id(0)
    n_steps = pl.num_programs(0)
    tile_rows = stage_ref.shape[1]
    tm = tile_rows // ROW_SUB

    def load(step, slot):
        rows = pl.ds(pl.multiple_of(step * tile_rows, tile_rows), tile_rows)
        return pltpu.make_async_copy(h_ref.at[rows, :], stage_ref.at[slot], load_sems.at[slot])

    def wait_rows(slot):
        for k in range(TOP_K):
            pltpu.make_async_copy(stage_ref.at[slot], xs_ref.at[pl.ds(0, tile_rows), :], row_sems.at[slot]).wait()

    @pl.when(i == 0)
    def _():
        _zero_unrouted_slots(pad_start_ref, pad_len_ref, nused_ref, xs_ref, zero_ref, pad_sem,
                             xs_ref.shape[0] // zero_ref.shape[0])
        load(0, 0).start()

    slot = lax.rem(i, 3)
    load(i, slot).wait()

    @pl.when(i + 1 < n_steps)
    def _():
        load(i + 1, lax.rem(i + 1, 3)).start()

    def issue(j, carry):
        for k in range(TOP_K):
            src = stage_ref.at[slot, pl.ds(pl.multiple_of(j * ROW_SUB, ROW_SUB), ROW_SUB), :]
            dst = xs_ref.at[pl.ds(pl.multiple_of(dest_ref[k, j], ROW_SUB), ROW_SUB), :]
            pltpu.make_async_copy(src, dst, row_sems.at[slot]).start()
        return carry

    lax.fori_loop(0, tm, issue, 0, unroll=ISSUE_UNROLL)

    @pl.when(i > 0)
    def _():
        wait_rows(lax.rem(i + 2, 3))

    @pl.when(i + 1 == n_steps)
    def _():
        wait_rows(slot)


def _dispatch(pad_start, pad_len, n_used, dest_t, hp, n_slots):
    n = hp.shape[0] // ROW_SUB
    tm = min(TM_DISPATCH, n)
    return pl.pallas_call(
        _dispatch_kernel,
        grid_spec=pltpu.PrefetchScalarGridSpec(
            num_scalar_prefetch=3, grid=(n // tm,),
            in_specs=[
                pl.BlockSpec((8, tm), lambda i, ps, pn, nu: (0, i), memory_space=pltpu.SMEM),
                pl.BlockSpec(memory_space=pl.ANY),
            ],
            out_specs=pl.BlockSpec(memory_space=pl.ANY),
            scratch_shapes=[pltpu.VMEM((3, tm * ROW_SUB, LANES), U32),
                            pltpu.VMEM((EXPERT_BLOCK * ROW_SUB, LANES), U32),
                            pltpu.SemaphoreType.DMA((3,)), pltpu.SemaphoreType.DMA((3,)),
                            pltpu.SemaphoreType.DMA(())]),
        out_shape=jax.ShapeDtypeStruct((n_slots * ROW_SUB, LANES), U32),
        compiler_params=_cparams("arbitrary"),
        name="moe_dispatch",
    )(pad_start, pad_len, n_used, dest_t, hp)


def _expert_kernel(be_ref, nused_ref, xs_ref, wg_ref, wu_ref, wd_ref, ys_ref):
    i = pl.program_id(0)

    @pl.when(i < nused_ref[0])
    def _():
        xb = _load_token_rows(xs_ref).astype(BF16)
        a = _dot(xb, wg_ref[...])
        u = _dot(xb, wu_ref[...])
        _store_token_rows(ys_ref, _pack_rows(_dot((_silu(a) * u).astype(BF16), wd_ref[...])))

    @pl.when(i >= nused_ref[0])
    def _():
        ys_ref[...] = jnp.zeros_like(ys_ref)


def _experts(block_e, n_used, xs, wg, wu, wd):
    n_slots = xs.shape[0] // ROW_SUB
    n_blocks = n_slots // EXPERT_BLOCK
    d, de = wg.shape[1], wg.shape[2]
    rows = pl.BlockSpec((EXPERT_BLOCK * ROW_SUB, LANES), lambda i, be, nu: (i, 0))
    used_rows = pl.BlockSpec((EXPERT_BLOCK * ROW_SUB, LANES), lambda i, be, nu: (jnp.minimum(i, nu[0] - 1), 0))
    return pl.pallas_call(
        _expert_kernel,
        grid_spec=pltpu.PrefetchScalarGridSpec(
            num_scalar_prefetch=2, grid=(n_blocks,),
            in_specs=[used_rows,
                      pl.BlockSpec((None, d, de), lambda i, be, nu: (be[i], 0, 0)),
                      pl.BlockSpec((None, d, de), lambda i, be, nu: (be[i], 0, 0)),
                      pl.BlockSpec((None, de, d), lambda i, be, nu: (be[i], 0, 0))],
            out_specs=rows),
        out_shape=jax.ShapeDtypeStruct((n_slots * ROW_SUB, LANES), U32),
        compiler_params=_cparams("arbitrary"),
        name="moe_experts",
    )(block_e, n_used, xs, wg, wu, wd)


def _combine_kernel(dest_ref, dest_next_ref, h_ref, x1_ref, mod_ref, wts_ref, wsg_ref, wsu_ref, wsd_ref, ys_ref,
                    out_ref, buf_ref, sems):
    i = pl.program_id(0)
    slot = lax.rem(i, 2)
    tm = h_ref.shape[0] // ROW_SUB

    def gather(idx_ref, into):
        def issue(j, carry):
            for k in range(TOP_K):
                src = ys_ref.at[pl.ds(pl.multiple_of(idx_ref[k, j], ROW_SUB), ROW_SUB), :]
                dst = buf_ref.at[into, k, pl.ds(pl.multiple_of(j * ROW_SUB, ROW_SUB), ROW_SUB), :]
                pltpu.make_async_copy(src, dst, sems.at[into]).start()
            return carry

        lax.fori_loop(0, tm, issue, 0, unroll=ISSUE_UNROLL)

    @pl.when(i == 0)
    def _():
        gather(dest_ref, 0)

    @pl.when(i + 1 < pl.num_programs(0))
    def _():
        gather(dest_next_ref, 1 - slot)

    hb = _load_token_rows(h_ref).astype(BF16)
    act = _silu(_dot(hb, wsg_ref[...])) * _dot(hb, wsu_ref[...])
    y = _dot(act.astype(BF16), wsd_ref[...])
    for k in range(TOP_K):
        pltpu.make_async_copy(ys_ref.at[pl.ds(0, tm * ROW_SUB), :], buf_ref.at[slot, k], sems.at[slot]).wait()
    wts = wts_ref[...]
    for k in range(TOP_K):
        y = y + wts[:, k:k + 1] * _load_token_rows(buf_ref.at[slot, k])
    out_ref[...] = x1_ref[...] + mod_ref[5:6, :] * y


def _combine(dest_t, hp, x1, mod, wts, ys, p, bt, s):
    n, d = x1.shape
    tm = min(TM_COMBINE, s)
    tiles = s // tm
    steps = n // tm
    wsg, wsu, wsd = (p[k].astype(BF16) for k in ("w_s_gate", "w_s_up", "w_s_down"))
    full = lambda a: pl.BlockSpec(a.shape, lambda i: (0,) * a.ndim)
    rows = pl.BlockSpec((tm, d), lambda i: (i, 0))
    out = pl.pallas_call(
        _combine_kernel,
        grid=(steps,),
        in_specs=[
            pl.BlockSpec((8, tm), lambda i: (0, i), memory_space=pltpu.SMEM),
            pl.BlockSpec((8, tm), lambda i: (0, jnp.minimum(i + 1, steps - 1)), memory_space=pltpu.SMEM),
            pl.BlockSpec((tm * ROW_SUB, LANES), lambda i: (i, 0)),
            rows,
            pl.BlockSpec((None, 6, d), lambda i: (i // tiles, 0, 0)),
            pl.BlockSpec((tm, 8), lambda i: (i, 0)),
            full(wsg), full(wsu), full(wsd),
            pl.BlockSpec(memory_space=pl.ANY),
        ],
        out_specs=rows,
        out_shape=jax.ShapeDtypeStruct((n, d), F32),
        scratch_shapes=[pltpu.VMEM((2, TOP_K, tm * ROW_SUB, LANES), U32), pltpu.SemaphoreType.DMA((2,))],
        compiler_params=_cparams("arbitrary"),
        name="moe_combine",
    )(dest_t, dest_t, hp, x1, mod, wts, wsg, wsu, wsd, ys)
    return out.reshape(bt, s, d)


def _layer(x, c, p):
    bt, s, d = x.shape
    n = bt * s
    mod = _adaln(c, p["w_ada"], p["b_ada"]).reshape(bt, 6, d)
    qa, ka, va, qb, kb, vb = _inproj(x, mod, p)
    oa = _dilated_attention(qa, ka, va, p)
    ob = _latent_attention(qb, kb, vb, p)
    x1, hp, idx_t, wts_t, rank_t, cnt = _outproj(oa, ob, x, mod, p)

    counts = cnt[:, 0].astype(I32)
    padded = (counts + EXPERT_BLOCK - 1) // EXPERT_BLOCK * EXPERT_BLOCK
    pend = jnp.cumsum(padded)
    pstart = pend - padded
    n_blocks = (n * TOP_K + N_EXPERTS * (EXPERT_BLOCK - 1)) // EXPERT_BLOCK
    block_first = jnp.arange(n_blocks, dtype=I32) * EXPERT_BLOCK
    block_e = jnp.minimum(jnp.sum((pend[None, :] <= block_first[:, None]).astype(I32), axis=1), N_EXPERTS - 1)
    n_used = (pend[-1:] // EXPERT_BLOCK).astype(I32)

    dest_t = _dest(pstart.astype(I32), idx_t, rank_t)
    xs = _dispatch(((pstart + counts) * ROW_SUB).astype(I32), (padded - counts).astype(I32), n_used, dest_t, hp,
                   n_blocks * EXPERT_BLOCK)
    ys = _experts(block_e, n_used, xs, p["w_e_gate"].astype(BF16), p["w_e_up"].astype(BF16),
                  p["w_e_down"].astype(BF16))
    return _combine(dest_t, hp, x1, mod, wts_t.T, ys, p, bt, s)


_PARAM_NAMES = ("w_ada", "b_ada", "g_norm1", "w_in", "g_qa", "g_ka", "g_qlat", "w_qb", "g_kvlat", "w_kvb",
                "g_qb", "g_kb", "w_o", "g_norm2", "w_router", "b_router", "w_e_gate", "w_e_up", "w_e_down",
                "w_s_gate", "w_s_up", "w_s_down")


def kernel(x_prompt, x_sample, c_prompt, c_sample, w_ada, b_ada, g_norm1, w_in, g_qa, g_ka, g_qlat, w_qb,
           g_kvlat, w_kvb, g_qb, g_kb, w_o, g_norm2, w_router, b_router, w_e_gate, w_e_up, w_e_down,
           w_s_gate, w_s_up, w_s_down):
    stacked = (w_ada, b_ada, g_norm1, w_in, g_qa, g_ka, g_qlat, w_qb, g_kvlat, w_kvb, g_qb, g_kb, w_o,
               g_norm2, w_router, b_router, w_e_gate, w_e_up, w_e_down, w_s_gate, w_s_up, w_s_down)
    y_prompt, y_sample = x_prompt, x_sample
    for layer in range(w_ada.shape[0]):
        p = {name: w[layer] for name, w in zip(_PARAM_NAMES, stacked)}
        y_prompt = _layer(y_prompt, c_prompt, p)
        y_sample = _layer(y_sample, c_sample, p)
    return (y_prompt, y_sample)
```

```python
import functools

import jax
import jax.numpy as jnp
from jax import lax
from jax.experimental import pallas as pl
from jax.experimental.pallas import tpu as pltpu

F32 = jnp.float32
BF16 = jnp.bfloat16
I32 = jnp.int32
U32 = jnp.uint32

HEAD_DIM = 64
A_HEADS = 12
A_WIDTH = A_HEADS * HEAD_DIM
DILATED_BRANCHES = ((128, 1), (512, 4), (2048, 16))
ROT_DIMS_A = HEAD_DIM // 4
B_HEADS = 4
QK_NOPE = 64
QK_ROPE = 32
QK_HEAD = QK_NOPE + QK_ROPE
V_HEAD = 64
Q_LORA = 256
KV_LORA = 256
B_WIDTH = B_HEADS * V_HEAD
ROPE_THETA = 500000.0
N_EXPERTS = 64
TOP_K = 6
N_GROUPS = 8
TOPK_GROUPS = 4
PER_GROUP = N_EXPERTS // N_GROUPS
ROUTED_SCALE = 2.5
EXPERT_BLOCK = 512
EPS = 1e-6

LOG2E = 1.4426950408889634
BOUND_SLACK = 1.02
EXP2_SAFE_RANGE = 120.0

LANES = 128
PAD_HEAD = 128
NEG_BIG = -1e30
VMEM_LIMIT = 56 * 1024 * 1024

TM_IN = 512
TM_OUT = 512
TQ_A = 128
UNROLL_A = 16
NORM_ROWS_A = 512
TQ_B = 512
KC_B = 128
KC_FAST_B = 256
KV_TILES_B = 2
TM_DISPATCH = 512
TM_COMBINE = 256
ISSUE_UNROLL = 4
ROW_SUB = 4


def _nt_dot(a, b):
    return lax.dot_general(a, b, (((1,), (1,)), ((), ())), preferred_element_type=F32)


def _dot(a, b):
    return jnp.dot(a, b, preferred_element_type=F32)


def _split_bf16(x):
    hi = x.astype(BF16)
    lo = (x - hi.astype(F32)).astype(BF16)
    return hi, lo


def _sigmoid(x):
    return 1.0 / (1.0 + jnp.exp(-x))


def _silu(x):
    return x * _sigmoid(x)


def _pack_rows(x):
    w = x.shape[1] // 2
    lo = lax.bitcast_convert_type(x[:, :w].astype(BF16).astype(F32), U32)
    hi = lax.bitcast_convert_type(x[:, w:].astype(BF16).astype(F32), U32)
    return (lo >> 16) | (hi & jnp.uint32(0xFFFF0000))


def _unpack_words(words):
    lo = lax.bitcast_convert_type(words << 16, F32)
    hi = lax.bitcast_convert_type(words & jnp.uint32(0xFFFF0000), F32)
    return lo, hi


def _store_token_rows(ref, words):
    m = ref.shape[0] // ROW_SUB
    for c in range(ROW_SUB):
        ref[pl.ds(c, m, stride=ROW_SUB), :] = words[:, c * LANES:(c + 1) * LANES]


def _load_token_rows(ref):
    m = ref.shape[0] // ROW_SUB
    halves = [_unpack_words(ref[pl.ds(c, m, stride=ROW_SUB), :]) for c in range(ROW_SUB)]
    return jnp.concatenate([lo for lo, _ in halves] + [hi for _, hi in halves], axis=1)


def _zero_unrouted_slots(pad_start_ref, pad_len_ref, nused_ref, xs_ref, zero_ref, pad_sem, n_blocks):
    zero_ref[...] = jnp.zeros_like(zero_ref)
    block_rows = zero_ref.shape[0]

    def slab(ref, first_row):
        return ref.at[pl.ds(pl.multiple_of(first_row, ROW_SUB), ROW_SUB), :]

    def pad_copy(e, r):
        return pltpu.make_async_copy(slab(zero_ref, 0), slab(xs_ref, pad_start_ref[e] + r * ROW_SUB), pad_sem)

    def per_expert(e, carry):
        lax.fori_loop(0, pad_len_ref[e], lambda r, c: (pad_copy(e, r).start(), c)[1], 0)
        lax.fori_loop(0, pad_len_ref[e], lambda r, c: (pad_copy(e, r).wait(), c)[1], 0)
        return carry

    lax.fori_loop(0, N_EXPERTS, per_expert, 0)

    def block_copy(b):
        rows = pl.ds(pl.multiple_of(b * block_rows, block_rows), block_rows)
        return pltpu.make_async_copy(zero_ref, xs_ref.at[rows, :], pad_sem)

    lax.fori_loop(nused_ref[0], n_blocks, lambda b, c: (block_copy(b).start(), c)[1], 0)
    lax.fori_loop(nused_ref[0], n_blocks, lambda b, c: (block_copy(b).wait(), c)[1], 0)


def _cparams(*sem):
    return pltpu.CompilerParams(dimension_semantics=sem, vmem_limit_bytes=VMEM_LIMIT)


def _adaln_kernel(c_ref, w_ref, b_ref, o_ref):
    s = _silu(c_ref[...])
    o_ref[...] = _dot(s.astype(BF16), w_ref[...].astype(BF16)) + b_ref[...]


def _adaln(c, w_ada, b_ada):
    bt, d = c.shape
    n_chunks = w_ada.shape[1] // d
    return pl.pallas_call(
        _adaln_kernel,
        grid=(n_chunks,),
        in_specs=[
            pl.BlockSpec((bt, d), lambda j: (0, 0)),
            pl.BlockSpec((d, d), lambda j: (0, j)),
            pl.BlockSpec((1, d), lambda j: (0, j)),
        ],
        out_specs=pl.BlockSpec((bt, d), lambda j: (0, j)),
        out_shape=jax.ShapeDtypeStruct((bt, w_ada.shape[1]), F32),
        compiler_params=_cparams("arbitrary"),
        name="adaln",
    )(c, w_ada, b_ada.reshape(1, -1))


def _seg_sumsq(x, seg_ref):
    sq = (x * x).astype(BF16)
    width = seg_ref.shape[0]
    outs = [_dot(sq[:, c * width:(c + 1) * width], seg_ref[...]) for c in range(x.shape[1] // width)]
    return outs[0] if len(outs) == 1 else jnp.concatenate(outs, axis=1)


def _tile_lanes(t, reps):
    return t if reps == 1 else jnp.concatenate([t] * reps, axis=1)


def _rope_lanes(x, tab_ref, shift, reps):
    width = x.shape[1]
    cos = _tile_lanes(tab_ref[:, 0:LANES], reps)
    sin_p = _tile_lanes(tab_ref[:, LANES:2 * LANES], reps)
    sin_m = _tile_lanes(tab_ref[:, 2 * LANES:3 * LANES], reps)
    return x * cos + pltpu.roll(x, shift, 1) * sin_p + pltpu.roll(x, width - shift, 1) * sin_m


def _inproj_kernel(x_ref, mod_ref, g1_ref, wqkv_ref, wlat_ref, wkr_ref, wqb_ref, wkn_ref, wv_ref,
                   gqa_ref, gka_ref, gql_ref, gkvl_ref, gqb_ref, gkb_ref, seg64_ref, seg128_ref, qpad_ref, kpad_ref,
                   ropea_ref, ropeb_ref,
                   qa_ref, ka_ref, va_ref, qb_ref, kb_ref, vb_ref):
    x = x_ref[...]
    d = x.shape[1]
    h = x * lax.rsqrt(jnp.sum(x * x, axis=-1, keepdims=True) * (1.0 / d) + EPS) * g1_ref[...]
    h = h * (1.0 + mod_ref[1:2, :]) + mod_ref[0:1, :]
    hb = h.astype(BF16)

    z = _dot(hb, wqkv_ref[...])
    q = z[:, 0:A_WIDTH]
    k = z[:, A_WIDTH:2 * A_WIDTH]
    q = q * lax.rsqrt(_seg_sumsq(q, seg64_ref) * (1.0 / HEAD_DIM) + EPS) * gqa_ref[...]
    k = k * lax.rsqrt(_seg_sumsq(k, seg64_ref) * (1.0 / HEAD_DIM) + EPS) * gka_ref[...]
    reps_a = A_WIDTH // LANES
    q = _rope_lanes(q, ropea_ref, ROT_DIMS_A // 2, reps_a) * (HEAD_DIM ** -0.5 * LOG2E)
    k = _rope_lanes(k, ropea_ref, ROT_DIMS_A // 2, reps_a)
    qa_ref[...] = q
    ka_ref[...] = k
    va_ref[...] = z[:, 2 * A_WIDTH:3 * A_WIDTH]

    zl = _dot(hb, wlat_ref[...])
    ql = zl[:, 0:Q_LORA]
    kvl = zl[:, Q_LORA:Q_LORA + KV_LORA]
    ql = ql * lax.rsqrt(jnp.sum(ql * ql, axis=-1, keepdims=True) * (1.0 / Q_LORA) + EPS) * gql_ref[...]
    kvl = kvl * lax.rsqrt(jnp.sum(kvl * kvl, axis=-1, keepdims=True) * (1.0 / KV_LORA) + EPS) * gkvl_ref[...]
    qlb = ql.astype(BF16)
    kvlb = kvl.astype(BF16)
    reps_b = B_HEADS
    qb = _dot(qlb, wqb_ref[...])
    qb = qb * lax.rsqrt(_seg_sumsq(qb, seg128_ref) * (1.0 / QK_HEAD) + EPS) * gqb_ref[...]
    qb = _rope_lanes(qb, ropeb_ref, QK_ROPE // 2, reps_b) * (QK_HEAD ** -0.5 * LOG2E) + qpad_ref[...]
    qb_ref[...] = qb.T.astype(BF16)
    kr = _dot(hb, wkr_ref[...])
    kb = _dot(kvlb, wkn_ref[...]) + _tile_lanes(kr, reps_b)
    kb = kb * lax.rsqrt(_seg_sumsq(kb, seg128_ref) * (1.0 / QK_HEAD) + EPS) * gkb_ref[...]
    kb = _rope_lanes(kb, ropeb_ref, QK_ROPE // 2, reps_b) + kpad_ref[...]
    kb_ref[...] = kb.astype(BF16)
    vb_ref[...] = _dot(kvlb, wv_ref[...]).T.astype(BF16)


def _score_bound(g_q, g_k, head_dim):
    bound = LOG2E * head_dim ** 0.5 * jnp.max(jnp.abs(g_q)) * jnp.max(jnp.abs(g_k))
    return bound * BOUND_SLACK


def _bound_is_safe(bound):
    return 2.0 * bound < EXP2_SAFE_RANGE


def _rope_table(seq, rot, lane_of_first, period):
    half = rot // 2
    inv = ROPE_THETA ** (-jnp.arange(half, dtype=F32) * 2.0 / rot)
    ang = jnp.arange(seq).astype(F32)[:, None] * inv[None, :]
    cos, sin = jnp.cos(ang), jnp.sin(ang)
    one = jnp.ones((seq, period), F32)
    zero = jnp.zeros((seq, period), F32)
    a, b = lane_of_first, lane_of_first + half
    cos_t = one.at[:, a:a + half].set(cos).at[:, b:b + half].set(cos)
    sin_p = zero.at[:, b:b + half].set(sin)
    sin_m = zero.at[:, a:a + half].set(-sin)
    reps = LANES // period
    return jnp.concatenate([jnp.tile(t, (1, reps)) for t in (cos_t, sin_p, sin_m)], axis=1)


def _pad_heads(w, heads, width, offset=0):
    lead = w.shape[:-1]
    w = w.reshape(lead + (heads, width))
    pad = [(0, 0)] * len(lead) + [(0, 0), (offset, PAD_HEAD - width - offset)]
    return jnp.pad(w, pad).reshape(lead + (heads * PAD_HEAD,))


def _inproj(x, mod, p):
    bt, s, d = x.shape
    tm = min(TM_IN, s)
    ropea = _rope_table(s, ROT_DIMS_A, 0, HEAD_DIM)
    ropeb = _rope_table(s, QK_ROPE, QK_NOPE, PAD_HEAD)
    w_in = p["w_in"]
    wqkv = w_in[:, 0:3 * A_WIDTH].astype(BF16)
    wlat = w_in[:, 3 * A_WIDTH:3 * A_WIDTH + Q_LORA + KV_LORA].astype(BF16)
    wkr = _pad_heads(w_in[:, 3 * A_WIDTH + Q_LORA + KV_LORA:], 1, QK_ROPE, QK_NOPE).astype(BF16)
    wqb = _pad_heads(p["w_qb"], B_HEADS, QK_HEAD).astype(BF16)
    wkv = p["w_kvb"].reshape(KV_LORA, B_HEADS, QK_NOPE + V_HEAD)
    wkn = _pad_heads(wkv[:, :, :QK_NOPE].reshape(KV_LORA, B_HEADS * QK_NOPE), B_HEADS, QK_NOPE).astype(BF16)
    wv = wkv[:, :, QK_NOPE:].reshape(KV_LORA, B_WIDTH).astype(BF16)
    gqa = jnp.tile(p["g_qa"], A_HEADS).reshape(1, -1)
    gka = jnp.tile(p["g_ka"], A_HEADS).reshape(1, -1)
    gqb = _pad_heads(jnp.tile(p["g_qb"], B_HEADS), B_HEADS, QK_HEAD).reshape(1, -1)
    gkb = _pad_heads(jnp.tile(p["g_kb"], B_HEADS), B_HEADS, QK_HEAD).reshape(1, -1)
    seg64 = jnp.kron(jnp.eye(256 // HEAD_DIM, dtype=F32), jnp.ones((HEAD_DIM, HEAD_DIM), F32)).astype(BF16)
    seg128 = jnp.kron(jnp.eye(2, dtype=F32), jnp.ones((PAD_HEAD, PAD_HEAD), F32)).astype(BF16)

    def full(a):
        return pl.BlockSpec(a.shape, lambda i, b: (0,) * a.ndim)

    spare = jnp.zeros((B_HEADS, PAD_HEAD), F32).at[:, QK_HEAD].set(1.0).reshape(1, -1)
    qpad = -_score_bound(p["g_qb"], p["g_kb"], QK_HEAD) * spare
    consts = [p["g_norm1"].reshape(1, -1), wqkv, wlat, wkr, wqb, wkn, wv, gqa, gka,
              p["g_qlat"].reshape(1, -1), p["g_kvlat"].reshape(1, -1), gqb, gkb, seg64, seg128, qpad, spare]
    tok = lambda w: pl.BlockSpec((None, tm, w), lambda i, b: (b, i, 0))
    qw = B_HEADS * PAD_HEAD
    tok_shape = lambda w, dt=F32: jax.ShapeDtypeStruct((bt, s, w), dt)
    return pl.pallas_call(
        _inproj_kernel,
        grid=(s // tm, bt),
        in_specs=[tok(d), pl.BlockSpec((None, 6, d), lambda i, b: (b, 0, 0))]
        + [full(a) for a in consts]
        + [pl.BlockSpec((tm, 3 * LANES), lambda i, b: (i, 0))] * 2,
        out_specs=[tok(A_WIDTH), tok(A_WIDTH), tok(A_WIDTH),
                   pl.BlockSpec((None, qw, tm), lambda i, b: (b, 0, i)),
                   tok(qw),
                   pl.BlockSpec((None, None, B_WIDTH, tm), lambda i, b: (b, i, 0, 0))],
        out_shape=[tok_shape(A_WIDTH), tok_shape(A_WIDTH), tok_shape(A_WIDTH),
                   jax.ShapeDtypeStruct((bt, qw, s), BF16),
                   tok_shape(qw, BF16),
                   jax.ShapeDtypeStruct((bt, s // tm, B_WIDTH, tm), BF16)],
        compiler_params=_cparams("arbitrary", "arbitrary"),
        name="inproj",
    )(x, mod, *consts, ropea, ropeb)


def _rows(start, size, stride):
    return pl.ds(start, size) if stride == 1 else pl.ds(start, size, stride=stride)


def _dilated_kernel(bound_ref, q_ref, k_ref, v_ref, o_ref, stat_ref, bias_ref, *, bounded):
    s_len = q_ref.shape[0]
    first = lax.broadcasted_iota(I32, (TQ_A, LANES), 1) < HEAD_DIM
    for branch, (window, dil) in enumerate(sorted(DILATED_BRANCHES, key=lambda wd: -wd[1])):
        radius = window // (2 * dil)
        win = TQ_A + 2 * radius
        length = s_len // dil
        nq = length // TQ_A
        rel = lax.broadcasted_iota(I32, (TQ_A, win), 0) - lax.broadcasted_iota(I32, (TQ_A, win), 1)
        shift = bound_ref[0] if bounded else 0.0
        for case in range(3):
            bias_ref[case] = jnp.where(jnp.abs(rel + case * radius) <= radius, -shift, NEG_BIG)

        def group(g, carry, branch=branch, dil=dil, radius=radius, win=win, length=length, nq=nq):
            done = []
            for u in range(UNROLL_A):
                idx = g * UNROLL_A + u
                r = idx // nq
                q0 = (idx % nq) * TQ_A
                w0 = jnp.clip(q0 - radius, 0, length - win)
                qrows = _rows(r + dil * q0, TQ_A, dil)
                krows = _rows(r + dil * w0, win, dil)
                q = q_ref[qrows, :]
                kw = k_ref[krows, :].astype(BF16)
                vw = v_ref[krows, :].astype(BF16)
                bias = bias_ref[(q0 - w0) // radius]
                outs, stats = [], []
                for keep in (first, jnp.logical_not(first)):
                    s = _nt_dot(jnp.where(keep, q, 0.0).astype(BF16), kw)
                    if bounded:
                        pexp = jnp.exp2(s + bias)
                        outs.append(_dot(pexp.astype(BF16), vw))
                        stats.append(jnp.sum(pexp, axis=-1, keepdims=True))
                    else:
                        s = s + bias
                        m = jnp.max(s, axis=-1, keepdims=True)
                        pexp = jnp.exp2(s - m)
                        den = jnp.sum(pexp, axis=-1, keepdims=True)
                        outs.append(_dot(pexp.astype(BF16), vw) / den)
                        stats.append(m + jnp.log2(den))
                o_new = jnp.where(first, outs[0], outs[1])
                stat_new = jnp.where(first, stats[0], stats[1])
                if branch > 0:
                    o_old, stat_old = o_ref[qrows, :], stat_ref[qrows, :]
                    if bounded:
                        o_new, stat_new = o_old + o_new, stat_old + stat_new
                    else:
                        mx = jnp.maximum(stat_old, stat_new)
                        w_old, w_new = jnp.exp2(stat_old - mx), jnp.exp2(stat_new - mx)
                        den = w_old + w_new
                        o_new = (w_old * o_old + w_new * o_new) / den
                        stat_new = mx + jnp.log2(den)
                done.append((qrows, o_new, stat_new))
            for qrows, o_new, stat_new in done:
                o_ref[qrows, :] = o_new
                stat_ref[qrows, :] = stat_new
            return carry

        lax.fori_loop(0, dil * nq // UNROLL_A, group, 0)

    if bounded:
        def normalise(i, carry):
            rows = pl.ds(pl.multiple_of(i * NORM_ROWS_A, NORM_ROWS_A), NORM_ROWS_A)
            o_ref[rows, :] = o_ref[rows, :] / stat_ref[rows, :]
            return carry

        lax.fori_loop(0, s_len // NORM_ROWS_A, normalise, 0)


def _dilated_call(bounded, bound, qa, ka, va):
    bt, s, width = qa.shape
    radii = {window // (2 * dil) for window, dil in DILATED_BRANCHES}
    assert len(radii) == 1, "the score-mask scratch assumes one window radius for all branches"
    radius = radii.pop()
    spec = pl.BlockSpec((None, s, LANES), lambda b, j: (b, 0, j))
    return pl.pallas_call(
        functools.partial(_dilated_kernel, bounded=bounded),
        grid=(bt, width // LANES),
        in_specs=[pl.BlockSpec(memory_space=pltpu.SMEM), spec, spec, spec],
        out_specs=spec,
        out_shape=jax.ShapeDtypeStruct((bt, s, width), F32),
        scratch_shapes=[pltpu.VMEM((s, LANES), F32), pltpu.VMEM((3, TQ_A, TQ_A + 2 * radius), F32)],
        compiler_params=_cparams("arbitrary", "arbitrary"),
        name="dilated_bounded" if bounded else "dilated_online",
    )(bound.reshape(1), qa, ka, va)


def _dilated_attention(qa, ka, va, p):
    bound = _score_bound(p["g_qa"], p["g_ka"], HEAD_DIM)
    return lax.cond(_bound_is_safe(bound),
                    functools.partial(_dilated_call, True), functools.partial(_dilated_call, False),
                    bound, qa, ka, va)


def _latent_online_kernel(qt_ref, k_ref, vt_ref, o_ref):
    n_kv, _, tk = vt_ref.shape
    tq = qt_ref.shape[1]
    n_chunks = tk // KC_B
    qts = [qt_ref[h * PAD_HEAD:(h + 1) * PAD_HEAD, :] for h in range(2)]

    def scores(j, h):
        k0 = pl.multiple_of(j * tk, tk)
        return tuple(_dot(k_ref[pl.ds(k0 + c * KC_B, KC_B), h * PAD_HEAD:(h + 1) * PAD_HEAD], qts[h])
                     for c in range(n_chunks))

    def body(j, carry):
        nxt = jnp.minimum(j + 1, n_kv - 1)
        new = []
        for h in range(2):
            m, l, acc, sts = carry[h]
            sts_next = scores(nxt, h)
            for c in range(n_chunks):
                m_new = jnp.maximum(m, jnp.max(sts[c], axis=0, keepdims=True))
                alpha = jnp.exp2(m - m_new)
                pexp = jnp.exp2(sts[c] - m_new)
                l = alpha * l + jnp.sum(pexp, axis=0, keepdims=True)
                vt = vt_ref[j, h * V_HEAD:(h + 1) * V_HEAD, c * KC_B:(c + 1) * KC_B]
                acc = alpha * acc + _dot(vt, pexp.astype(BF16))
                m = m_new
            new.append((m, l, acc, sts_next))
        return tuple(new)

    init = tuple((jnp.full((1, tq), NEG_BIG, F32), jnp.zeros((1, tq), F32), jnp.zeros((V_HEAD, tq), F32),
                  scores(0, h)) for h in range(2))
    (_, l0, acc0, _), (_, l1, acc1, _) = lax.fori_loop(0, n_kv, body, init)
    ot = jnp.concatenate([acc0 / l0, acc1 / l1], axis=0)
    o_ref[...] = ot.T.astype(o_ref.dtype)


def _latent_bounded_kernel(qt_ref, k_ref, vt_ref, o_ref):
    n_kv, _, tk = vt_ref.shape
    tq = qt_ref.shape[1]
    n_chunks = tk // KC_FAST_B
    qts = [qt_ref[h * PAD_HEAD:(h + 1) * PAD_HEAD, :] for h in range(2)]
    group = min(KV_TILES_B, n_kv)

    def probs(jj, h):
        out = []
        for t in range(group):
            k0 = pl.multiple_of((jj * group + t) * tk, tk)
            for c in range(n_chunks):
                kt = k_ref[pl.ds(k0 + c * KC_FAST_B, KC_FAST_B), h * PAD_HEAD:(h + 1) * PAD_HEAD]
                pexp = jnp.exp2(_dot(kt, qts[h]))
                out.append((pexp.astype(BF16), jnp.sum(pexp, axis=0, keepdims=True)))
        return tuple(out)

    def body(jj, carry):
        nxt = jnp.minimum(jj + 1, n_kv // group - 1)
        new = []
        for h in range(2):
            l, acc, cur = carry[h]
            nxt_probs = probs(nxt, h)
            for t in range(group):
                for c in range(n_chunks):
                    pexp, psum = cur[t * n_chunks + c]
                    vt = vt_ref[jj * group + t, h * V_HEAD:(h + 1) * V_HEAD, c * KC_FAST_B:(c + 1) * KC_FAST_B]
                    acc = acc + _dot(vt, pexp)
                    l = l + psum
            new.append((l, acc, nxt_probs))
        return tuple(new)

    init = tuple((jnp.zeros((1, tq), F32), jnp.zeros((V_HEAD, tq), F32), probs(0, h)) for h in range(2))
    (l0, acc0, _), (l1, acc1, _) = lax.fori_loop(0, n_kv // group, body, init)
    ot = jnp.concatenate([acc0 / l0, acc1 / l1], axis=0)
    o_ref[...] = ot.T.astype(o_ref.dtype)


def _latent_call(bounded, qbt, kb, vbt):
    bt, s, _ = kb.shape
    pairs = B_HEADS // 2
    n_kv, tk = vbt.shape[1], vbt.shape[3]
    tq = min(TQ_B, s)
    return pl.pallas_call(
        _latent_bounded_kernel if bounded else _latent_online_kernel,
        grid=(bt, pairs, s // tq),
        in_specs=[
            pl.BlockSpec((None, 2 * PAD_HEAD, tq), lambda b, hp, i: (b, hp, i)),
            pl.BlockSpec((None, s, 2 * PAD_HEAD), lambda b, hp, i: (b, 0, hp)),
            pl.BlockSpec((None, n_kv, 2 * V_HEAD, tk), lambda b, hp, i: (b, 0, hp, 0)),
        ],
        out_specs=pl.BlockSpec((None, tq, 2 * V_HEAD), lambda b, hp, i: (b, i, hp)),
        out_shape=jax.ShapeDtypeStruct((bt, s, B_WIDTH), BF16),
        compiler_params=_cparams("arbitrary", "arbitrary", "arbitrary"),
        name="latent_bounded" if bounded else "latent_online",
    )(qbt, kb, vbt)


def _latent_attention(qbt, kb, vbt, p):
    safe = _bound_is_safe(_score_bound(p["g_qb"], p["g_kb"], QK_HEAD))
    return lax.cond(safe, functools.partial(_latent_call, True), functools.partial(_latent_call, False),
                    qbt, kb, vbt)


def _outproj_kernel(oa_ref, ob_ref, x_ref, mod_ref,
                    woa_ref, wob_ref, g2_ref, wrh_ref, wrl_ref, br_ref, tri_ref,
                    x1_ref, h2_ref, idx_ref, wts_ref, rank_ref, cnt_ref, base_ref):
    first_step = jnp.logical_and(pl.program_id(0) == 0, pl.program_id(1) == 0)

    @pl.when(first_step)
    def _():
        base_ref[...] = jnp.zeros_like(base_ref)

    mixed = _dot(oa_ref[...].astype(BF16), woa_ref[...]) + _dot(ob_ref[...], wob_ref[...])
    x1 = x_ref[...] + mod_ref[2:3, :] * mixed
    x1_ref[...] = x1
    d = x1.shape[1]
    h2 = x1 * lax.rsqrt(jnp.sum(x1 * x1, axis=-1, keepdims=True) * (1.0 / d) + EPS) * g2_ref[...]
    h2 = h2 * (1.0 + mod_ref[4:5, :]) + mod_ref[3:4, :]
    _store_token_rows(h2_ref, _pack_rows(h2))

    hh, hl = _split_bf16(h2)
    logits = _nt_dot(wrh_ref[...], hh) + _nt_dot(wrh_ref[...], hl) + _nt_dot(wrl_ref[...], hh)
    scores = _sigmoid(logits)
    biased = scores + br_ref[...]
    tm = biased.shape[1]
    sub = lax.broadcasted_iota(I32, (PER_GROUP, tm), 0)
    groups, gscore = [], []
    for g in range(N_GROUPS):
        bg = biased[g * PER_GROUP:(g + 1) * PER_GROUP, :]
        m1 = jnp.max(bg, axis=0, keepdims=True)
        i1 = jnp.min(jnp.where(bg == m1, sub, PER_GROUP), axis=0, keepdims=True)
        m2 = jnp.max(jnp.where(sub == i1, -jnp.inf, bg), axis=0, keepdims=True)
        groups.append(bg)
        gscore.append(m1 + m2)
    masked = []
    for g in range(N_GROUPS):
        ahead = jnp.zeros((1, tm), I32)
        for g2 in range(N_GROUPS):
            if g2 == g:
                continue
            beats = (gscore[g2] > gscore[g]) if g2 > g else (gscore[g2] >= gscore[g])
            ahead = ahead + beats.astype(I32)
        masked.append(jnp.where(ahead < TOPK_GROUPS, groups[g], -jnp.inf))
    cand = jnp.concatenate(masked, axis=0)
    eid = lax.broadcasted_iota(I32, (N_EXPERTS, tm), 0)
    chosen = jnp.zeros((N_EXPERTS, tm), F32)
    sel_idx, sel_w, sel_hot = [], [], []
    for _ in range(TOP_K):
        mv = jnp.max(cand, axis=0, keepdims=True)
        ie = jnp.min(jnp.where(cand == mv, eid, N_EXPERTS), axis=0, keepdims=True)
        hot = eid == ie
        sel_idx.append(ie)
        sel_hot.append(hot)
        sel_w.append(jnp.sum(jnp.where(hot, scores, 0.0), axis=0, keepdims=True))
        cand = jnp.where(hot, -jnp.inf, cand)
        chosen = chosen + hot.astype(F32)
    wsum = sel_w[0]
    for w in sel_w[1:]:
        wsum = wsum + w
    within = _dot(chosen.astype(BF16), tri_ref[...]) + base_ref[...]
    ranks = [jnp.sum(jnp.where(hot, within, 0.0), axis=0, keepdims=True) for hot in sel_hot]
    pad_i = [jnp.zeros((1, tm), I32)] * (8 - TOP_K)
    pad_f = [jnp.zeros((1, tm), F32)] * (8 - TOP_K)
    idx_ref[...] = jnp.concatenate(sel_idx + pad_i, axis=0)
    wts_ref[...] = jnp.concatenate([w / wsum * ROUTED_SCALE for w in sel_w] + pad_f, axis=0)
    rank_ref[...] = jnp.concatenate([r.astype(I32) for r in ranks] + pad_i, axis=0)
    base_ref[...] = base_ref[...] + jnp.sum(chosen, axis=1, keepdims=True)
    cnt_ref[...] = jnp.broadcast_to(base_ref[...], cnt_ref.shape)


def _outproj(oa, ob, x, mod, p):
    bt, s, d = x.shape
    tm = min(TM_OUT, s)
    n = bt * s
    woa = p["w_o"][:A_WIDTH].astype(BF16)
    wob = p["w_o"][A_WIDTH:].astype(BF16)
    wr_t = p["w_router"].T
    wrh, wrl = _split_bf16(wr_t)
    tri = (jnp.arange(tm)[:, None] < jnp.arange(tm)[None, :]).astype(BF16)
    consts = [woa, wob, p["g_norm2"].reshape(1, -1), wrh, wrl, p["b_router"].reshape(-1, 1), tri]
    tiles = s // tm
    tok = lambda w: pl.BlockSpec((None, tm, w), lambda b, i: (b, i, 0))
    full = lambda a: pl.BlockSpec(a.shape, lambda b, i: (0,) * a.ndim)
    flat = lambda w: pl.BlockSpec((tm, w), lambda b, i: (b * tiles + i, 0))
    slab = pl.BlockSpec((8, tm), lambda b, i: (0, b * tiles + i))
    return pl.pallas_call(
        _outproj_kernel,
        grid=(bt, tiles),
        in_specs=[tok(A_WIDTH), tok(B_WIDTH), tok(d), pl.BlockSpec((None, 6, d), lambda b, i: (b, 0, 0))]
        + [full(a) for a in consts],
        out_specs=[flat(d), pl.BlockSpec((tm * ROW_SUB, LANES), lambda b, i: (b * tiles + i, 0)),
                   slab, slab, slab, pl.BlockSpec((N_EXPERTS, LANES), lambda b, i: (0, 0))],
        out_shape=[jax.ShapeDtypeStruct((n, d), F32), jax.ShapeDtypeStruct((n * ROW_SUB, LANES), U32),
                   jax.ShapeDtypeStruct((8, n), I32), jax.ShapeDtypeStruct((8, n), F32),
                   jax.ShapeDtypeStruct((8, n), I32), jax.ShapeDtypeStruct((N_EXPERTS, LANES), F32)],
        scratch_shapes=[pltpu.VMEM((N_EXPERTS, 1), F32)],
        compiler_params=_cparams("arbitrary", "arbitrary"),
        name="outproj_router",
    )(oa, ob, x, mod, *consts)


def _dest_kernel(pstart_ref, idx_ref, rank_ref, dest_ref):
    idx = idx_ref[...]
    dest = rank_ref[...]
    for e in range(N_EXPERTS):
        dest = dest + jnp.where(idx == e, pstart_ref[e], 0)
    dest_ref[...] = dest * ROW_SUB


def _dest(pstart, idx_t, rank_t):
    n = idx_t.shape[1]
    tn = min(2048, n)
    slab = pl.BlockSpec((8, tn), lambda i, ps: (0, i))
    return pl.pallas_call(
        _dest_kernel,
        grid_spec=pltpu.PrefetchScalarGridSpec(
            num_scalar_prefetch=1, grid=(n // tn,), in_specs=[slab, slab], out_specs=slab),
        out_shape=jax.ShapeDtypeStruct((8, n), I32),
        compiler_params=_cparams("arbitrary"),
        name="moe_dest",
    )(pstart, idx_t, rank_t)


def _dispatch_kernel(pad_start_ref, pad_len_ref, nused_ref, dest_ref, h_ref, xs_ref, stage_ref, zero_ref,
                     load_sems, row_sems, pad_sem):
    i = pl.program_id(0)
    n_steps = pl.num_programs(0)
    tile_rows = stage_ref.shape[1]
    tm = tile_rows // ROW_SUB

    def load(step, slot):
        rows = pl.ds(pl.multiple_of(step * tile_rows, tile_rows), tile_rows)
        return pltpu.make_async_copy(h_ref.at[rows, :], stage_ref.at[slot], load_sems.at[slot])

    def wait_rows(slot):
        for k in range(TOP_K):
            pltpu.make_async_copy(stage_ref.at[slot], xs_ref.at[pl.ds(0, tile_rows), :], row_sems.at[slot]).wait()

    @pl.when(i == 0)
    def _():
        _zero_unrouted_slots(pad_start_ref, pad_len_ref, nused_ref, xs_ref, zero_ref, pad_sem,
                             xs_ref.shape[0] // zero_ref.shape[0])
        load(0, 0).start()

    slot = lax.rem(i, 3)
    load(i, slot).wait()

    @pl.when(i + 1 < n_steps)
    def _():
        load(i + 1, lax.rem(i + 1, 3)).start()

    def issue(j, carry):
        for k in range(TOP_K):
            src = stage_ref.at[slot, pl.ds(pl.multiple_of(j * ROW_SUB, ROW_SUB), ROW_SUB), :]
            dst = xs_ref.at[pl.ds(pl.multiple_of(dest_ref[k, j], ROW_SUB), ROW_SUB), :]
            pltpu.make_async_copy(src, dst, row_sems.at[slot]).start(priority=k % 2)
        return carry

    lax.fori_loop(0, tm, issue, 0, unroll=ISSUE_UNROLL)

    @pl.when(i > 0)
    def _():
        wait_rows(lax.rem(i + 2, 3))

    @pl.when(i + 1 == n_steps)
    def _():
        wait_rows(slot)


def _dispatch(pad_start, pad_len, n_used, dest_t, hp, n_slots):
    n = hp.shape[0] // ROW_SUB
    tm = min(TM_DISPATCH, n)
    return pl.pallas_call(
        _dispatch_kernel,
        grid_spec=pltpu.PrefetchScalarGridSpec(
            num_scalar_prefetch=3, grid=(n // tm,),
            in_specs=[
                pl.BlockSpec((8, tm), lambda i, ps, pn, nu: (0, i), memory_space=pltpu.SMEM),
                pl.BlockSpec(memory_space=pl.ANY),
            ],
            out_specs=pl.BlockSpec(memory_space=pl.ANY),
            scratch_shapes=[pltpu.VMEM((3, tm * ROW_SUB, LANES), U32),
                            pltpu.VMEM((EXPERT_BLOCK * ROW_SUB, LANES), U32),
                            pltpu.SemaphoreType.DMA((3,)), pltpu.SemaphoreType.DMA((3,)),
                            pltpu.SemaphoreType.DMA(())]),
        out_shape=jax.ShapeDtypeStruct((n_slots * ROW_SUB, LANES), U32),
        compiler_params=_cparams("arbitrary"),
        name="moe_dispatch",
    )(pad_start, pad_len, n_used, dest_t, hp)


def _expert_kernel(be_ref, nused_ref, xs_ref, wg_ref, wu_ref, wd_ref, ys_ref):
    i = pl.program_id(0)

    @pl.when(i < nused_ref[0])
    def _():
        xb = _load_token_rows(xs_ref).astype(BF16)
        a = _dot(xb, wg_ref[...])
        u = _dot(xb, wu_ref[...])
        _store_token_rows(ys_ref, _pack_rows(_dot((_silu(a) * u).astype(BF16), wd_ref[...])))

    @pl.when(i >= nused_ref[0])
    def _():
        ys_ref[...] = jnp.zeros_like(ys_ref)


def _experts(block_e, n_used, xs, wg, wu, wd):
    n_slots = xs.shape[0] // ROW_SUB
    n_blocks = n_slots // EXPERT_BLOCK
    d, de = wg.shape[1], wg.shape[2]
    rows = pl.BlockSpec((EXPERT_BLOCK * ROW_SUB, LANES), lambda i, be, nu: (i, 0))
    used_rows = pl.BlockSpec((EXPERT_BLOCK * ROW_SUB, LANES), lambda i, be, nu: (jnp.minimum(i, nu[0] - 1), 0))
    return pl.pallas_call(
        _expert_kernel,
        grid_spec=pltpu.PrefetchScalarGridSpec(
            num_scalar_prefetch=2, grid=(n_blocks,),
            in_specs=[used_rows,
                      pl.BlockSpec((None, d, de), lambda i, be, nu: (be[i], 0, 0)),
                      pl.BlockSpec((None, d, de), lambda i, be, nu: (be[i], 0, 0)),
                      pl.BlockSpec((None, de, d), lambda i, be, nu: (be[i], 0, 0))],
            out_specs=rows),
        out_shape=jax.ShapeDtypeStruct((n_slots * ROW_SUB, LANES), U32),
        compiler_params=_cparams("arbitrary"),
        name="moe_experts",
    )(block_e, n_used, xs, wg, wu, wd)


def _combine_kernel(dest_ref, dest_next_ref, h_ref, x1_ref, mod_ref, wts_ref, wsg_ref, wsu_ref, wsd_ref, ys_ref,
                    out_ref, buf_ref, sems):
    i = pl.program_id(0)
    slot = lax.rem(i, 2)
    tm = h_ref.shape[0] // ROW_SUB

    def gather(idx_ref, into):
        def issue(j, carry):
            for k in range(TOP_K):
                src = ys_ref.at[pl.ds(pl.multiple_of(idx_ref[k, j], ROW_SUB), ROW_SUB), :]
                dst = buf_ref.at[into, k, pl.ds(pl.multiple_of(j * ROW_SUB, ROW_SUB), ROW_SUB), :]
                pltpu.make_async_copy(src, dst, sems.at[into]).start(priority=k % 2)
            return carry

        lax.fori_loop(0, tm, issue, 0, unroll=ISSUE_UNROLL)

    @pl.when(i == 0)
    def _():
        gather(dest_ref, 0)

    @pl.when(i + 1 < pl.num_programs(0))
    def _():
        gather(dest_next_ref, 1 - slot)

    hb = _load_token_rows(h_ref).astype(BF16)
    act = _silu(_dot(hb, wsg_ref[...])) * _dot(hb, wsu_ref[...])
    y = _dot(act.astype(BF16), wsd_ref[...])
    for k in range(TOP_K):
        pltpu.make_async_copy(ys_ref.at[pl.ds(0, tm * ROW_SUB), :], buf_ref.at[slot, k], sems.at[slot]).wait()
    wts = wts_ref[...]
    for k in range(TOP_K):
        y = y + wts[:, k:k + 1] * _load_token_rows(buf_ref.at[slot, k])
    out_ref[...] = x1_ref[...] + mod_ref[5:6, :] * y


def _combine(dest_t, hp, x1, mod, wts, ys, p, bt, s):
    n, d = x1.shape
    tm = min(TM_COMBINE, s)
    tiles = s // tm
    steps = n // tm
    wsg, wsu, wsd = (p[k].astype(BF16) for k in ("w_s_gate", "w_s_up", "w_s_down"))
    full = lambda a: pl.BlockSpec(a.shape, lambda i: (0,) * a.ndim)
    rows = pl.BlockSpec((tm, d), lambda i: (i, 0))
    out = pl.pallas_call(
        _combine_kernel,
        grid=(steps,),
        in_specs=[
            pl.BlockSpec((8, tm), lambda i: (0, i), memory_space=pltpu.SMEM),
            pl.BlockSpec((8, tm), lambda i: (0, jnp.minimum(i + 1, steps - 1)), memory_space=pltpu.SMEM),
            pl.BlockSpec((tm * ROW_SUB, LANES), lambda i: (i, 0)),
            rows,
            pl.BlockSpec((None, 6, d), lambda i: (i // tiles, 0, 0)),
            pl.BlockSpec((tm, 8), lambda i: (i, 0)),
            full(wsg), full(wsu), full(wsd),
            pl.BlockSpec(memory_space=pl.ANY),
        ],
        out_specs=rows,
        out_shape=jax.ShapeDtypeStruct((n, d), F32),
        scratch_shapes=[pltpu.VMEM((2, TOP_K, tm * ROW_SUB, LANES), U32), pltpu.SemaphoreType.DMA((2,))],
        compiler_params=_cparams("arbitrary"),
        name="moe_combine",
    )(dest_t, dest_t, hp, x1, mod, wts, wsg, wsu, wsd, ys)
    return out.reshape(bt, s, d)


def _layer(x, c, p):
    bt, s, d = x.shape
    n = bt * s
    mod = _adaln(c, p["w_ada"], p["b_ada"]).reshape(bt, 6, d)
    qa, ka, va, qb, kb, vb = _inproj(x, mod, p)
    oa = _dilated_attention(qa, ka, va, p)
    ob = _latent_attention(qb, kb, vb, p)
    x1, hp, idx_t, wts_t, rank_t, cnt = _outproj(oa, ob, x, mod, p)

    counts = cnt[:, 0].astype(I32)
    padded = (counts + EXPERT_BLOCK - 1) // EXPERT_BLOCK * EXPERT_BLOCK
    pend = jnp.cumsum(padded)
    pstart = pend - padded
    n_blocks = (n * TOP_K + N_EXPERTS * (EXPERT_BLOCK - 1)) // EXPERT_BLOCK
    block_first = jnp.arange(n_blocks, dtype=I32) * EXPERT_BLOCK
    block_e = jnp.minimum(jnp.sum((pend[None, :] <= block_first[:, None]).astype(I32), axis=1), N_EXPERTS - 1)
    n_used = (pend[-1:] // EXPERT_BLOCK).astype(I32)

    dest_t = _dest(pstart.astype(I32), idx_t, rank_t)
    xs = _dispatch(((pstart + counts) * ROW_SUB).astype(I32), (padded - counts).astype(I32), n_used, dest_t, hp,
                   n_blocks * EXPERT_BLOCK)
    ys = _experts(block_e, n_used, xs, p["w_e_gate"].astype(BF16), p["w_e_up"].astype(BF16),
                  p["w_e_down"].astype(BF16))
    return _combine(dest_t, hp, x1, mod, wts_t.T, ys, p, bt, s)


_PARAM_NAMES = ("w_ada", "b_ada", "g_norm1", "w_in", "g_qa", "g_ka", "g_qlat", "w_qb", "g_kvlat", "w_kvb",
                "g_qb", "g_kb", "w_o", "g_norm2", "w_router", "b_router", "w_e_gate", "w_e_up", "w_e_down",
                "w_s_gate", "w_s_up", "w_s_down")


def kernel(x_prompt, x_sample, c_prompt, c_sample, w_ada, b_ada, g_norm1, w_in, g_qa, g_ka, g_qlat, w_qb,
           g_kvlat, w_kvb, g_qb, g_kb, w_o, g_norm2, w_router, b_router, w_e_gate, w_e_up, w_e_down,
           w_s_gate, w_s_up, w_s_down):
    stacked = (w_ada, b_ada, g_norm1, w_in, g_qa, g_ka, g_qlat, w_qb, g_kvlat, w_kvb, g_qb, g_kb, w_o,
               g_norm2, w_router, b_router, w_e_gate, w_e_up, w_e_down, w_s_gate, w_s_up, w_s_down)
    y_prompt, y_sample = x_prompt, x_sample
    for layer in range(w_ada.shape[0]):
        p = {name: w[layer] for name, w in zip(_PARAM_NAMES, stacked)}
        y_prompt = _layer(y_prompt, c_prompt, p)
        y_sample = _layer(y_sample, c_sample, p)
    return (y_prompt, y_sample)
```
